```python
import jax, jax.numpy as jnp
from jax import lax
import numpy as np

D_MODEL = 1024
BATCH = 32
SEQ = 2048
DEPTH = 2
DEC_BATCH = 16
DEC_SEQ = 64
PAST_LEN = 2048

CHUNK = 64
N_MIXERS = 2
N_POOL_LAYERS = (DEPTH + 1) // 2
N_ATTN_LAYERS = DEPTH // 2
POOL_WINDOWS = (2, 4, 8, 16)
N_POOL_GROUPS = 4
POOL_GROUP = D_MODEL // N_POOL_GROUPS
POOL_STATE = max(POOL_WINDOWS) - 1
N_HEADS = 16
HEAD_DIM = D_MODEL // N_HEADS
N_KV_HEADS = 4
KV_GROUP = N_HEADS // N_KV_HEADS
N_IDX_HEADS = 8
IDX_DIM = 64
TOPK_MAX = 256
Q_BLOCK = 128
ROPE_THETA = 10000.0
D_FF = 4 * D_MODEL
EPS = 1e-6
NEG = -1e30
Q_W = N_HEADS * HEAD_DIM
KV_W = N_KV_HEADS * HEAD_DIM
QI_W = N_IDX_HEADS * IDX_DIM
PROJ_W = Q_W + 2 * KV_W + QI_W + IDX_DIM + N_IDX_HEADS

kernel_name = "pool_dsa_hybrid_stream_step"


def _rmsnorm(x, g):
    xf = x.astype(jnp.float32)
    xf = xf * lax.rsqrt(jnp.mean(xf * xf, axis=-1, keepdims=True) + EPS)
    return (xf * g.astype(jnp.float32)).astype(x.dtype)


def _rope(x, pos):
    d = x.shape[-1]
    inv_freq = 1.0 / (ROPE_THETA ** (jnp.arange(0, d, 2, dtype=jnp.float32) / d))
    ang = pos.astype(jnp.float32)[:, None] * inv_freq[None, :]
    c = jnp.cos(ang)[None, :, None, :]
    s = jnp.sin(ang)[None, :, None, :]
    xf = x.astype(jnp.float32)
    x1, x2 = xf[..., : d // 2], xf[..., d // 2:]
    return jnp.concatenate([x1 * c - x2 * s, x2 * c + x1 * s], axis=-1).astype(x.dtype)


def _pool_mixer(h, pos, past, w, scale):
    B, T, D = h.shape
    if past is None:
        past = jnp.zeros((B, POOL_STATE, D), h.dtype)
    padded = jnp.concatenate([past.astype(h.dtype), h], axis=1)
    cs = jnp.cumsum(padded.astype(jnp.float32), axis=1)
    cs0 = jnp.concatenate([jnp.zeros((B, 1, D), jnp.float32), cs], axis=1)
    P = POOL_STATE
    hf = h.astype(jnp.float32)
    outs = []
    for g, win in enumerate(POOL_WINDOWS):
        sl = slice(g * POOL_GROUP, (g + 1) * POOL_GROUP)
        wsum = cs0[:, P + 1: P + 1 + T, sl] - cs0[:, P + 1 - win: P + 1 - win + T, sl]
        cnt = jnp.minimum(pos + 1, win).astype(jnp.float32)[None, :, None]
        diff = (wsum / cnt - hf[..., sl]).astype(h.dtype)
        outs.append(diff @ w[g])
    y = jnp.concatenate(outs, axis=-1) * scale
    return y, padded[:, -POOL_STATE:]


def _dsa_attend(q, qi, wi, qpos, k, v, kidx, kpos, n_sel):
    B, T = q.shape[0], q.shape[1]
    logits = jnp.einsum('bthd,bsd->bths', qi.astype(jnp.float32), kidx.astype(jnp.float32)) * (IDX_DIM ** -0.5)
    score = jnp.einsum('bths,bth->bts', jax.nn.relu(logits), wi.astype(jnp.float32))
    qchunk = qpos // CHUNK
    adm = (kpos[None, :] // CHUNK) <= qchunk[:, None]
    score = jnp.where(adm[None], score, NEG)
    _, idx = lax.top_k(score, n_sel)
    valid = (kpos[idx] // CHUNK) <= qchunk[None, :, None]
    gather = jax.vmap(lambda a, i: a[i])
    kg = gather(k, idx)
    vg = gather(v, idx)
    qg = q.reshape(B, T, N_KV_HEADS, KV_GROUP, HEAD_DIM)
    s = jnp.einsum('btgrd,btkgd->btgrk', qg, kg).astype(jnp.float32) * (HEAD_DIM ** -0.5)
    s = jnp.where(valid[:, :, None, None, :], s, NEG)
    p = jax.nn.softmax(s, axis=-1).astype(v.dtype)
    o = jnp.einsum('btgrk,btkgd->btgrd', p, vg)
    return o.reshape(B, T, N_HEADS * HEAD_DIM)


def _attn_mixer(h, pos, past, w_in, w_o):
    B, T, _ = h.shape
    p = h @ w_in
    o0 = 0
    q = p[..., o0:o0 + Q_W].reshape(B, T, N_HEADS, HEAD_DIM); o0 += Q_W
    k = p[..., o0:o0 + KV_W].reshape(B, T, N_KV_HEADS, HEAD_DIM); o0 += KV_W
    v = p[..., o0:o0 + KV_W].reshape(B, T, N_KV_HEADS, HEAD_DIM); o0 += KV_W
    qi = p[..., o0:o0 + QI_W].reshape(B, T, N_IDX_HEADS, IDX_DIM); o0 += QI_W
    ki = p[..., o0:o0 + IDX_DIM].reshape(B, T, 1, IDX_DIM); o0 += IDX_DIM
    wi = p[..., o0:o0 + N_IDX_HEADS] * (N_IDX_HEADS ** -0.5)
    q = _rope(q, pos)
    k = _rope(k, pos)
    qi = _rope(qi, pos)
    ki = _rope(ki, pos)[:, :, 0]
    if past is None:
        k_all, v_all, ki_all = k, v, ki
    else:
        kc, vc, kic = past
        k_all = jnp.concatenate([kc, k.astype(kc.dtype)], axis=1)
        v_all = jnp.concatenate([vc, v.astype(vc.dtype)], axis=1)
        ki_all = jnp.concatenate([kic, ki.astype(kic.dtype)], axis=1)
    L = k_all.shape[1]
    kpos = jnp.arange(L, dtype=jnp.int32)
    n_sel = min(TOPK_MAX, L // 4)
    if T % Q_BLOCK == 0:
        nb = T // Q_BLOCK
        to_blocks = lambda a: jnp.moveaxis(a.reshape(B, nb, Q_BLOCK, *a.shape[2:]), 1, 0)
        xs = (to_blocks(q), to_blocks(qi), to_blocks(wi), pos.reshape(nb, Q_BLOCK))
        ob = lax.map(lambda a: _dsa_attend(a[0], a[1], a[2], a[3], k_all, v_all, ki_all, kpos, n_sel), xs)
        o = jnp.moveaxis(ob, 0, 1).reshape(B, T, N_HEADS * HEAD_DIM)
    else:
        o = _dsa_attend(q, qi, wi, pos, k_all, v_all, ki_all, kpos, n_sel)
    return o @ w_o, (k, v, ki)


def _mlp(h, w_up, w_down):
    u = jax.nn.relu(h @ w_up)
    return (u * u) @ w_down


def _trunk(x, pos, pool_past, attn_past, norm_mix, norm_mlp, norm_final, pool_w, pool_scale,
           attn_w_in, attn_w_o, mlp_w_up, mlp_w_down):
    pool_new, k_new, v_new, ki_new = [], [], [], []
    for i in range(DEPTH):
        h = _rmsnorm(x, norm_mix[i])
        j = i // N_MIXERS
        if i % N_MIXERS == 0:
            past = None if pool_past is None else pool_past[j]
            y, st = _pool_mixer(h, pos, past, pool_w[j], pool_scale[j])
            pool_new.append(st)
        else:
            past = None if attn_past is None else (attn_past[0][j], attn_past[1][j], attn_past[2][j])
            y, (kn, vn, kin) = _attn_mixer(h, pos, past, attn_w_in[j], attn_w_o[j])
            k_new.append(kn); v_new.append(vn); ki_new.append(kin)
        x = x + y.astype(x.dtype)
        x = x + _mlp(_rmsnorm(x, norm_mlp[i]), mlp_w_up[i], mlp_w_down[i]).astype(x.dtype)
    return (_rmsnorm(x, norm_final), jnp.stack(pool_new), jnp.stack(k_new), jnp.stack(v_new), jnp.stack(ki_new))


def setup_inputs(seed: int = 0) -> dict:
    key = jax.random.key(seed)
    ks = jax.random.split(key, 16)
    f32 = jnp.float32
    nrm = lambda k, shape, s: jax.random.normal(k, shape, f32) * s
    return {
        "x_prompt": nrm(ks[0], (BATCH, SEQ, D_MODEL), 1.0),
        "x_sample": nrm(ks[1], (DEC_BATCH, DEC_SEQ, D_MODEL), 1.0),
        "state_pool": nrm(ks[2], (N_POOL_LAYERS, DEC_BATCH, POOL_STATE, D_MODEL), 1.0),
        "cache_k": nrm(ks[3], (N_ATTN_LAYERS, DEC_BATCH, PAST_LEN, N_KV_HEADS, HEAD_DIM), 1.0),
        "cache_v": nrm(ks[4], (N_ATTN_LAYERS, DEC_BATCH, PAST_LEN, N_KV_HEADS, HEAD_DIM), 1.0),
        "cache_kidx": nrm(ks[5], (N_ATTN_LAYERS, DEC_BATCH, PAST_LEN, IDX_DIM), 1.0),
        "norm_mix": 1.0 + nrm(ks[6], (DEPTH, D_MODEL), 0.02),
        "norm_mlp": 1.0 + nrm(ks[7], (DEPTH, D_MODEL), 0.02),
        "norm_final": 1.0 + nrm(ks[8], (D_MODEL,), 0.02),
        "pool_w": nrm(ks[9], (N_POOL_LAYERS, N_POOL_GROUPS, POOL_GROUP, POOL_GROUP), POOL_GROUP ** -0.5),
        "pool_scale": 1.0 + nrm(ks[10], (N_POOL_LAYERS, D_MODEL), 0.02),
        "attn_w_in": nrm(ks[11], (N_ATTN_LAYERS, D_MODEL, PROJ_W), D_MODEL ** -0.5),
        "attn_w_o": nrm(ks[12], (N_ATTN_LAYERS, Q_W, D_MODEL), Q_W ** -0.5),
        "mlp_w_up": nrm(ks[13], (DEPTH, D_MODEL, D_FF), D_MODEL ** -0.5),
        "mlp_w_down": nrm(ks[14], (DEPTH, D_FF, D_MODEL), D_FF ** -0.5),
    }


def reference(x_prompt, x_sample, state_pool, cache_k, cache_v, cache_kidx, norm_mix, norm_mlp, norm_final,
              pool_w, pool_scale, attn_w_in, attn_w_o, mlp_w_up, mlp_w_down):
    T_p = x_prompt.shape[1]
    T_s = x_sample.shape[1]
    past_len = cache_k.shape[2]
    pos_p = jnp.arange(T_p, dtype=jnp.int32)
    pos_s = past_len + jnp.arange(T_s, dtype=jnp.int32)
    y_prompt, pool_p, k_p, v_p, ki_p = _trunk(
        x_prompt, pos_p, None, None, norm_mix, norm_mlp, norm_final, pool_w, pool_scale,
        attn_w_in, attn_w_o, mlp_w_up, mlp_w_down)
    y_sample, pool_s, k_s, v_s, ki_s = _trunk(
        x_sample, pos_s, state_pool, (cache_k, cache_v, cache_kidx), norm_mix, norm_mlp, norm_final,
        pool_w, pool_scale, attn_w_in, attn_w_o, mlp_w_up, mlp_w_down)
    return (y_prompt, y_sample, pool_p, pool_s, k_p, v_p, ki_p, k_s, v_s, ki_s)
```

```python
import functools

import jax
import jax.numpy as jnp
from jax import lax
from jax.experimental import pallas as pl
from jax.experimental.pallas import tpu as pltpu

D_MODEL = 1024
CHUNK = 64
POOL_WINDOWS = (2, 4, 8, 16)
POOL_GROUP = D_MODEL // len(POOL_WINDOWS)
POOL_STATE = max(POOL_WINDOWS) - 1
N_HEADS = 16
HEAD_DIM = 64
N_KV_HEADS = 4
KV_GROUP = N_HEADS // N_KV_HEADS
N_IDX_HEADS = 8
IDX_DIM = 64
TOPK_MAX = 256
ROPE_THETA = 10000.0
D_FF = 4 * D_MODEL
EPS = 1e-6
NEG = -1e30
Q_W = N_HEADS * HEAD_DIM
KV_W = N_KV_HEADS * HEAD_DIM
QI_W = N_IDX_HEADS * IDX_DIM

LANES = 128
KV_PAD_W = N_KV_HEADS * LANES

OFF_Q = 0
OFF_KC = OFF_Q + Q_W
OFF_VC = OFF_KC + KV_W
OFF_KP = OFF_VC + KV_W
OFF_VP = OFF_KP + KV_PAD_W
OFF_QI = OFF_VP + KV_PAD_W
OFF_KW = OFF_QI + QI_W
PROJ_PACKED_W = OFF_KW + LANES

VMEM_LIMIT = 56 * 1024 * 1024

_NT = (((1,), (1,)), ((), ()))


def _rms(x, g):
    return x * lax.rsqrt(jnp.mean(x * x, axis=-1, keepdims=True) + EPS) * g


def _const_spec(shape):
    n = len(shape)
    return pl.BlockSpec(shape, lambda *_: (0,) * n, pipeline_mode=pl.Buffered(1))


def _pool_kernel(x_ref, past_ref, g_ref, w_ref, scale_ref, x1_ref, st_ref, hp_ref, *, tt, pos0):
    t = pl.program_id(1)
    halo = POOL_STATE + 1

    @pl.when(t == 0)
    def _():
        hp_ref[0:halo, :] = past_ref[0]

    x = x_ref[0]
    h = _rms(x, g_ref[...])
    hp_ref[halo:halo + tt, :] = h
    pos = pos0 + t * tt + lax.broadcasted_iota(jnp.int32, (tt, 1), 0)
    for g, win in enumerate(POOL_WINDOWS):
        c0, c1 = g * POOL_GROUP, (g + 1) * POOL_GROUP
        ws = h[:, c0:c1]
        for j in range(1, win):
            ws = ws + hp_ref[halo - j:halo - j + tt, c0:c1]
        cnt = jnp.minimum(pos + 1, win).astype(jnp.float32)
        diff = ws / cnt - h[:, c0:c1]
        y = jnp.dot(diff.astype(jnp.bfloat16), w_ref[g], preferred_element_type=jnp.float32)
        x1_ref[0, :, c0:c1] = x[:, c0:c1] + y * scale_ref[:, c0:c1]
    tail = hp_ref[tt:tt + halo, :]
    st_ref[0] = tail
    hp_ref[0:halo, :] = tail


def _pool_layer(x, past16, g, w, scale, pos0, tt):
    B, T, D = x.shape
    halo = POOL_STATE + 1
    return pl.pallas_call(
        functools.partial(_pool_kernel, tt=tt, pos0=pos0),
        grid=(B, T // tt),
        in_specs=[
            pl.BlockSpec((1, tt, D), lambda b, t: (b, t, 0)),
            pl.BlockSpec((1, halo, D), lambda b, t: (b, 0, 0)),
            _const_spec((1, D)),
            _const_spec(w.shape),
            _const_spec((1, D)),
        ],
        out_specs=[
            pl.BlockSpec((1, tt, D), lambda b, t: (b, t, 0)),
            pl.BlockSpec((1, halo, D), lambda b, t: (b, 0, 0)),
        ],
        out_shape=[
            jax.ShapeDtypeStruct((B, T, D), jnp.float32),
            jax.ShapeDtypeStruct((B, halo, D), jnp.float32),
        ],
        scratch_shapes=[pltpu.VMEM((halo + tt, D), jnp.float32)],
        compiler_params=pltpu.CompilerParams(
            dimension_semantics=("arbitrary", "arbitrary"), vmem_limit_bytes=VMEM_LIMIT),
        name="pool_mixer",
    )(x, past16, g, w, scale)


def _mlp_kernel(*refs, fc, with_attn, with_final):
    refs = list(refs)
    x_ref = refs.pop(0)
    if with_attn:
        o_ref = refs.pop(0)
        wo_ref = refs.pop(0)
    g_ref = refs.pop(0)
    up_ref = refs.pop(0)
    down_ref = refs.pop(0)
    if with_final:
        gf_ref = refs.pop(0)
    out_ref = refs.pop(0)

    x = x_ref[...]
    if with_attn:
        x = x + jnp.dot(o_ref[...], wo_ref[...], preferred_element_type=jnp.float32)
    h = _rms(x, g_ref[...]).astype(jnp.bfloat16)
    acc = x
    for c in range(D_FF // fc):
        u = jnp.dot(h, up_ref[:, c * fc:(c + 1) * fc], preferred_element_type=jnp.float32)
        u = jnp.maximum(u, 0.0)
        a = (u * u).astype(jnp.bfloat16)
        acc = acc + jnp.dot(a, down_ref[c * fc:(c + 1) * fc, :], preferred_element_type=jnp.float32)
    if with_final:
        acc = _rms(acc, gf_ref[...])
    out_ref[...] = acc


def _mlp_layer(x, g, up, down, tm, fc, attn=None, final_g=None):
    N, D = x.shape
    row = lambda i: (i, 0)
    args, specs = [x], [pl.BlockSpec((tm, D), row)]
    if attn is not None:
        o, wo = attn
        args += [o, wo]
        specs += [pl.BlockSpec((tm, Q_W), row), _const_spec(wo.shape)]
    args += [g, up, down]
    specs += [_const_spec((1, D)), _const_spec(up.shape), _const_spec(down.shape)]
    if final_g is not None:
        args.append(final_g)
        specs.append(_const_spec((1, D)))
    return pl.pallas_call(
        functools.partial(_mlp_kernel, fc=fc, with_attn=attn is not None, with_final=final_g is not None),
        grid=(N // tm,),
        in_specs=specs,
        out_specs=pl.BlockSpec((tm, D), row),
        out_shape=jax.ShapeDtypeStruct((N, D), jnp.float32),
        compiler_params=pltpu.CompilerParams(
            dimension_semantics=("arbitrary",), vmem_limit_bytes=VMEM_LIMIT),
        name="mlp_attn_out" if attn is not None else "mlp",
    )(*args)


def _rope_block(xb, tab_ref, special):
    o = LANES if special else 0
    c = tab_ref[:, o:o + LANES]
    s1 = tab_ref[:, 2 * LANES + o:3 * LANES + o]
    s2 = tab_ref[:, 4 * LANES + o:5 * LANES + o]
    return xb * c + pltpu.roll(xb, LANES - HEAD_DIM // 2, 1) * s1 + pltpu.roll(xb, HEAD_DIM // 2, 1) * s2


def _proj_kernel(x_ref, g_ref, w_ref, tab_ref, q_ref, kc_ref, vc_ref, kp_ref, vp_ref, qi_ref, kw_ref, kib_ref):
    h = _rms(x_ref[0], g_ref[...]).astype(jnp.bfloat16)

    def proj(off, width):
        return jnp.dot(h, w_ref[:, off:off + width], preferred_element_type=jnp.float32)

    def rope(p, special=False):
        return jnp.concatenate(
            [_rope_block(p[:, j:j + LANES], tab_ref, special) for j in range(0, p.shape[1], LANES)], axis=-1)

    q_ref[0] = rope(proj(OFF_Q, Q_W)).astype(q_ref.dtype)
    kc_ref[0] = rope(proj(OFF_KC, KV_W))
    vc_ref[0] = proj(OFF_VC, KV_W)
    kp_ref[0] = rope(proj(OFF_KP, KV_PAD_W)).astype(kp_ref.dtype)
    vp = proj(OFF_VP, KV_PAD_W)
    lane = lax.broadcasted_iota(jnp.int32, vp.shape, 1)
    vp_ref[0] = jnp.where(lane % LANES == HEAD_DIM, 1.0, vp).astype(vp_ref.dtype)
    qi_ref[0] = rope(proj(OFF_QI, QI_W)).astype(qi_ref.dtype)
    kw = rope(proj(OFF_KW, LANES), special=True)
    kw_ref[0] = kw
    kib_ref[0] = kw.astype(kib_ref.dtype)


def _proj_layer(x, g, w_packed, tab, tm):
    B, T, D = x.shape
    blk = lambda w: pl.BlockSpec((1, tm, w), lambda b, t: (b, t, 0))
    bf, f32 = jnp.bfloat16, jnp.float32
    outs = [(Q_W, bf), (KV_W, f32), (KV_W, f32), (KV_PAD_W, bf), (KV_PAD_W, bf), (QI_W, bf), (LANES, f32), (LANES, bf)]
    return pl.pallas_call(
        _proj_kernel,
        grid=(B, T // tm),
        in_specs=[
            blk(D),
            _const_spec((1, D)),
            _const_spec(w_packed.shape),
            pl.BlockSpec((tm, tab.shape[1]), lambda b, t: (t, 0)),
        ],
        out_specs=[blk(w) for w, _ in outs],
        out_shape=[jax.ShapeDtypeStruct((B, T, w), dt) for w, dt in outs],
        compiler_params=pltpu.CompilerParams(
            dimension_semantics=("arbitrary", "arbitrary"), vmem_limit_bytes=VMEM_LIMIT),
        name="attn_in_proj",
    )(x, g, w_packed, tab)


def _attn_kernel(q_ref, qi_ref, kw_ref, kp_ref, vp_ref, kib_ref, o_ref, key_ref, f_ref, *,
                 tq, tq_real, n_keys, n_keys_real, pos0, n_sel, kb):
    t = pl.program_id(1)
    lane_q = lax.broadcasted_iota(jnp.int32, (1, tq), 1)
    qchunk = (pos0 + t * tq + lane_q) // CHUNK
    qvalid = lane_q < tq_real
    wi_t = kw_ref[0].T[IDX_DIM:IDX_DIM + N_IDX_HEADS, :]

    def admissible(k0, n):
        kpos = k0 + lax.broadcasted_iota(jnp.int32, (n, 1), 0)
        return (kpos // CHUNK) <= qchunk, kpos < n_keys_real

    for k0 in range(0, n_keys, kb):
        kx = kib_ref[0, k0:k0 + kb, 0:IDX_DIM]
        acc = jnp.zeros((kb, tq), jnp.float32)
        for hh in range(N_IDX_HEADS):
            qh = qi_ref[0, :, hh * IDX_DIM:(hh + 1) * IDX_DIM]
            lg = lax.dot_general(kx, qh, _NT, preferred_element_type=jnp.float32)
            acc = acc + jnp.maximum(lg, 0.0) * wi_t[hh:hh + 1, :]
        adm, exists = admissible(k0, kb)
        sc = jnp.where(exists, jnp.where(adm, acc, NEG), -jnp.inf)
        bits = lax.bitcast_convert_type(sc, jnp.int32)
        key_ref[k0:k0 + kb, :] = bits ^ ((bits >> 31) & jnp.int32(0x7FFFFFFF))

    def count_ge(cand):
        return jnp.sum((key_ref[...] >= cand).astype(jnp.int32), axis=0, keepdims=True)

    int_min = jnp.int32(-2 ** 31)
    thr = jnp.where(count_ge(jnp.zeros((1, tq), jnp.int32)) >= n_sel, jnp.int32(0), int_min)

    def radix_step(i, thr):
        cand = thr + jnp.left_shift(jnp.int32(1), 30 - i)
        return jnp.where(count_ge(cand) >= n_sel, cand, thr)

    thr = lax.fori_loop(0, 31, radix_step, thr)

    keys = key_ref[...]
    adm_all, _ = admissible(0, n_keys)
    eq = keys == thr
    n_gt = jnp.sum((keys > thr).astype(jnp.int32), axis=0, keepdims=True)
    n_eq = jnp.sum(eq.astype(jnp.int32), axis=0, keepdims=True)
    n_eq_adm = jnp.sum((eq & adm_all).astype(jnp.int32), axis=0, keepdims=True)
    need = n_sel - n_gt
    tie = (n_eq > need) & (n_eq_adm > 0) & qvalid
    any_tie = jnp.max(tie.astype(jnp.int32)) > 0

    def write_mask(exact_ties):
        need_f = need.astype(jnp.float32)
        run = jnp.zeros((1, tq), jnp.float32)
        r = lax.broadcasted_iota(jnp.int32, (LANES, LANES), 0)
        c = lax.broadcasted_iota(jnp.int32, (LANES, LANES), 1)
        tri = jnp.where(r >= c, 1.0, 0.0).astype(jnp.bfloat16)
        for k0 in range(0, n_keys, LANES):
            kblk = key_ref[k0:k0 + LANES, :]
            adm, _ = admissible(k0, LANES)
            if exact_ties:
                eqb = kblk == thr
                eqf = jnp.where(eqb, 1.0, 0.0)
                rank = jnp.dot(tri, eqf.astype(jnp.bfloat16), preferred_element_type=jnp.float32) + run
                run = run + jnp.sum(eqf, axis=0, keepdims=True)
                sel = (kblk > thr) | (eqb & (rank <= need_f))
            else:
                sel = kblk >= thr
            fill = jnp.where(adm, jnp.inf, NEG)
            f_ref[:, k0:k0 + LANES] = jnp.where(sel, fill, -jnp.inf).T

    @pl.when(any_tie)
    def _():
        write_mask(True)

    @pl.when(jnp.logical_not(any_tie))
    def _():
        write_mask(False)

    for g in range(N_KV_HEADS):
        kg = kp_ref[0, :, g * LANES:g * LANES + HEAD_DIM]
        vg = vp_ref[0, :, g * LANES:(g + 1) * LANES]
        outs = []
        for r in range(KV_GROUP):
            hd = g * KV_GROUP + r
            qh = q_ref[0, :, hd * HEAD_DIM:(hd + 1) * HEAD_DIM]
            s = lax.dot_general(qh, kg, _NT, preferred_element_type=jnp.float32)
            s = jnp.minimum(s, f_ref[...])
            m = jnp.max(s, axis=-1, keepdims=True)
            p = jnp.exp(s - m).astype(jnp.bfloat16)
            pv = jnp.dot(p, vg, preferred_element_type=jnp.float32)
            outs.append(pv[:, 0:HEAD_DIM] / pv[:, HEAD_DIM:HEAD_DIM + 1])
        o_ref[0, :, g * KV_GROUP * HEAD_DIM:(g + 1) * KV_GROUP * HEAD_DIM] = (
            jnp.concatenate(outs, axis=-1).astype(o_ref.dtype))


def _attn_layer(q, qi, kw, kp, vp, kib, *, tq, tq_real, n_keys_real, pos0, n_sel, kb):
    B, T, _ = q.shape
    n_keys = kp.shape[1]
    qblk = lambda w: pl.BlockSpec((1, tq, w), lambda b, t: (b, t, 0))
    kblk = lambda w: pl.BlockSpec((1, n_keys, w), lambda b, t: (b, 0, 0))
    return pl.pallas_call(
        functools.partial(_attn_kernel, tq=tq, tq_real=tq_real, n_keys=n_keys, n_keys_real=n_keys_real,
                          pos0=pos0, n_sel=n_sel, kb=kb),
        grid=(B, T // tq),
        in_specs=[qblk(Q_W), qblk(QI_W), qblk(LANES), kblk(KV_PAD_W), kblk(KV_PAD_W), kblk(LANES)],
        out_specs=qblk(Q_W),
        out_shape=jax.ShapeDtypeStruct((B, T, Q_W), jnp.bfloat16),
        scratch_shapes=[pltpu.VMEM((n_keys, tq), jnp.int32), pltpu.VMEM((tq, n_keys), jnp.float32)],
        compiler_params=pltpu.CompilerParams(
            dimension_semantics=("arbitrary", "arbitrary"), vmem_limit_bytes=VMEM_LIMIT),
        name="dsa_attention",
    )(q, qi, kw, kp, vp, kib)


def _pad_heads(a):
    lead = a.shape[:-1]
    a = a.reshape(*lead, N_KV_HEADS, HEAD_DIM)
    a = jnp.pad(a, [(0, 0)] * len(lead) + [(0, 0), (0, LANES - HEAD_DIM)])
    return a.reshape(*lead, KV_PAD_W)


def _pack_w_in(w_in):
    o = 0
    wq = w_in[:, o:o + Q_W] * (HEAD_DIM ** -0.5); o += Q_W
    wk = w_in[:, o:o + KV_W]; o += KV_W
    wv = w_in[:, o:o + KV_W]; o += KV_W
    wqi = w_in[:, o:o + QI_W] * (IDX_DIM ** -0.5); o += QI_W
    wki = w_in[:, o:o + IDX_DIM]; o += IDX_DIM
    wwi = w_in[:, o:o + N_IDX_HEADS]
    wkw = jnp.pad(jnp.concatenate([wki, wwi], axis=1), ((0, 0), (0, LANES - IDX_DIM - N_IDX_HEADS)))
    return jnp.concatenate([wq, wk, wv, _pad_heads(wk), _pad_heads(wv), wqi, wkw], axis=1).astype(jnp.bfloat16)


def _rope_tables(pos):
    half = HEAD_DIM // 2
    inv_freq = 1.0 / (ROPE_THETA ** (jnp.arange(0, HEAD_DIM, 2, dtype=jnp.float32) / HEAD_DIM))
    ang = pos.astype(jnp.float32)[:, None] * inv_freq[None, :]
    c, s = jnp.cos(ang), jnp.sin(ang)
    z = jnp.zeros_like(c)
    n = pos.shape[0]
    zpad = jnp.zeros((n, LANES - HEAD_DIM), jnp.float32)
    wi_scale = jnp.full((n, N_IDX_HEADS), N_IDX_HEADS ** -0.5, jnp.float32)
    c_kw = jnp.concatenate([c, c, wi_scale, jnp.zeros((n, LANES - HEAD_DIM - N_IDX_HEADS), jnp.float32)], axis=1)
    return jnp.concatenate([
        jnp.concatenate([c, c, c, c], axis=1), c_kw,
        jnp.concatenate([-s, z, -s, z], axis=1), jnp.concatenate([-s, z, zpad], axis=1),
        jnp.concatenate([z, s, z, s], axis=1), jnp.concatenate([z, s, zpad], axis=1),
    ], axis=1)


def _round_up(n, m):
    return (n + m - 1) // m * m


def _trunk(x, pos0, pool_past, attn_past, norm_mix, norm_mlp, norm_final, pool_w, pool_scale,
           w_packed, w_o, w_up, w_down):
    B, T, D = x.shape
    bf = jnp.bfloat16
    row = lambda v: v.reshape(1, D)

    if pool_past is None:
        past16 = jnp.zeros((B, POOL_STATE + 1, D), jnp.float32)
    else:
        past16 = jnp.pad(pool_past, ((0, 0), (1, 0), (0, 0)))
    tt = min(T, 512)
    x1, st = _pool_layer(x, past16, row(norm_mix[0]), pool_w[0].astype(bf), row(pool_scale[0]), pos0, tt)
    pool_new = st[:, 1:][None]
    tm = min(B * T, 512)
    x2 = _mlp_layer(x1.reshape(B * T, D), row(norm_mlp[0]), w_up[0], w_down[0], tm, 512).reshape(B, T, D)

    tab = _rope_tables(pos0 + jnp.arange(T, dtype=jnp.int32))
    q, kc, vc, kp, vp, qi, kw, kib = _proj_layer(x2, row(norm_mix[1]), w_packed, tab, min(T, 256))
    k_new = kc.reshape(1, B, T, N_KV_HEADS, HEAD_DIM)
    v_new = vc.reshape(1, B, T, N_KV_HEADS, HEAD_DIM)
    ki_new = kw[:, :, :IDX_DIM][None]

    tq = 128
    kb = 512
    if attn_past is None:
        n_real = T
        kp_all, vp_all, kib_all = kp, vp, kib
    else:
        ck, cv, cki = attn_past
        P = ck.shape[1]
        n_real = P + T
        ones_lane = (jnp.arange(KV_PAD_W) % LANES == HEAD_DIM).astype(bf)
        kp_all = jnp.concatenate([_pad_heads(ck.reshape(B, P, KV_W)).astype(bf), kp], axis=1)
        vp_all = jnp.concatenate([_pad_heads(cv.reshape(B, P, KV_W)).astype(bf) + ones_lane, vp], axis=1)
        kib_all = jnp.concatenate([jnp.pad(cki, ((0, 0), (0, 0), (0, LANES - IDX_DIM))).astype(bf), kib], axis=1)
    n_keys = _round_up(n_real, kb)
    padk = lambda a: jnp.pad(a, ((0, 0), (0, n_keys - n_real), (0, 0)))
    tq_pad = _round_up(T, tq)
    padq = lambda a: jnp.pad(a, ((0, 0), (0, tq_pad - T), (0, 0)))
    n_sel = min(TOPK_MAX, n_real // 4)
    o = _attn_layer(padq(q), padq(qi), padq(kw), padk(kp_all), padk(vp_all), padk(kib_all),
                    tq=tq, tq_real=min(T, tq), n_keys_real=n_real, pos0=pos0, n_sel=n_sel, kb=kb)
    o = o[:, :T].reshape(B * T, Q_W)

    y = _mlp_layer(x2.reshape(B * T, D), row(norm_mlp[1]), w_up[1], w_down[1], tm, 512,
                   attn=(o, w_o), final_g=row(norm_final)).reshape(B, T, D)
    return y, pool_new, k_new, v_new, ki_new


def kernel(x_prompt, x_sample, state_pool, cache_k, cache_v, cache_kidx, norm_mix, norm_mlp, norm_final,
           pool_w, pool_scale, attn_w_in, attn_w_o, mlp_w_up, mlp_w_down):
    bf = jnp.bfloat16
    shared = (norm_mix, norm_mlp, norm_final, pool_w, pool_scale, _pack_w_in(attn_w_in[0]),
              attn_w_o[0].astype(bf), mlp_w_up.astype(bf), mlp_w_down.astype(bf))
    y_p, pool_p, k_p, v_p, ki_p = _trunk(x_prompt, 0, None, None, *shared)
    y_s, pool_s, k_s, v_s, ki_s = _trunk(
        x_sample, cache_k.shape[2], state_pool[0], (cache_k[0], cache_v[0], cache_kidx[0]), *shared)
    return (y_p, y_s, pool_p, pool_s, k_p, v_p, ki_p, k_s, v_s, ki_s)
```

```python
import functools
import math
import struct

import jax
import jax.numpy as jnp
from jax import lax
from jax.experimental import pallas as pl
from jax.experimental.pallas import tpu as pltpu

D_MODEL = 1024
CHUNK = 64
POOL_WINDOWS = (2, 4, 8, 16)
POOL_GROUP = D_MODEL // len(POOL_WINDOWS)
POOL_STATE = max(POOL_WINDOWS) - 1
N_HEADS = 16
HEAD_DIM = 64
N_KV_HEADS = 4
KV_GROUP = N_HEADS // N_KV_HEADS
N_IDX_HEADS = 8
IDX_DIM = 64
TOPK_MAX = 256
ROPE_THETA = 10000.0
D_FF = 4 * D_MODEL
EPS = 1e-6
NEG = -1e30
Q_W = N_HEADS * HEAD_DIM
KV_W = N_KV_HEADS * HEAD_DIM
QI_W = N_IDX_HEADS * IDX_DIM
LOG2E = math.log2(math.e)

SUBLANES = 8
LANES = 128
KV_PAD_W = N_KV_HEADS * LANES
KEY_BLOCK = 256
Q_TILE = 256

OFF_KC = 0
OFF_VC = OFF_KC + KV_W
OFF_KP = OFF_VC + KV_W
OFF_KW = OFF_KP + KV_PAD_W
PROJ_ROWMAJOR_W = OFF_KW + LANES
ROW_Q = 0
ROW_QI = ROW_Q + Q_W
ROW_V = ROW_QI + QI_W
ROW_WI_IN_V = HEAD_DIM + SUBLANES
PROJ_TRANSPOSED_ROWS = ROW_V + KV_PAD_W

VMEM_LIMIT = 56 * 1024 * 1024

_NT = (((1,), (1,)), ((), ()))


def _f32_order_key(x):
    b = struct.unpack("<i", struct.pack("<f", x))[0]
    return b ^ ((b >> 31) & 0x7FFFFFFF)


NEG_KEY = _f32_order_key(NEG)


def _rms(x, g):
    return x * lax.rsqrt(jnp.mean(x * x, axis=-1, keepdims=True) + EPS) * g


def _const_spec(shape):
    n = len(shape)
    return pl.BlockSpec(shape, lambda *_: (0,) * n, pipeline_mode=pl.Buffered(1))


def _tree(op, parts):
    parts = list(parts)
    while len(parts) > 1:
        parts = [op(parts[i], parts[i + 1]) if i + 1 < len(parts) else parts[i] for i in range(0, len(parts), 2)]
    return parts[0]


def _fold_rows(op, x):
    return _tree(op, [x[r:r + SUBLANES] for r in range(0, x.shape[0], SUBLANES)])


def _pool_kernel(x_ref, past_ref, g_ref, w_ref, scale_ref, x1_ref, st_ref, hp_ref, *, tt, pos0):
    t = pl.program_id(1)
    halo = POOL_STATE + 1

    @pl.when(t == 0)
    def _():
        hp_ref[0:halo, :] = past_ref[0]

    x = x_ref[0]
    h = _rms(x, g_ref[...])
    hp_ref[halo:halo + tt, :] = h
    pos = pos0 + t * tt + lax.broadcasted_iota(jnp.int32, (tt, 1), 0)
    for g, win in enumerate(POOL_WINDOWS):
        c0, c1 = g * POOL_GROUP, (g + 1) * POOL_GROUP
        ws = h[:, c0:c1]
        for j in range(1, win):
            ws = ws + hp_ref[halo - j:halo - j + tt, c0:c1]
        cnt = jnp.minimum(pos + 1, win).astype(jnp.float32)
        diff = ws / cnt - h[:, c0:c1]
        y = jnp.dot(diff.astype(jnp.bfloat16), w_ref[g], preferred_element_type=jnp.float32)
        x1_ref[0, :, c0:c1] = x[:, c0:c1] + y * scale_ref[:, c0:c1]
    tail = hp_ref[tt:tt + halo, :]
    st_ref[0] = tail
    hp_ref[0:halo, :] = tail


def _pool_layer(x, past16, g, w, scale, pos0, tt):
    B, T, D = x.shape
    halo = POOL_STATE + 1
    return pl.pallas_call(
        functools.partial(_pool_kernel, tt=tt, pos0=pos0),
        grid=(B, T // tt),
        in_specs=[
            pl.BlockSpec((1, tt, D), lambda b, t: (b, t, 0)),
            pl.BlockSpec((1, halo, D), lambda b, t: (b, 0, 0)),
            _const_spec((1, D)),
            _const_spec(w.shape),
            _const_spec((1, D)),
        ],
        out_specs=[
            pl.BlockSpec((1, tt, D), lambda b, t: (b, t, 0)),
            pl.BlockSpec((1, halo, D), lambda b, t: (b, 0, 0)),
        ],
        out_shape=[
            jax.ShapeDtypeStruct((B, T, D), jnp.float32),
            jax.ShapeDtypeStruct((B, halo, D), jnp.float32),
        ],
        scratch_shapes=[pltpu.VMEM((halo + tt, D), jnp.float32)],
        compiler_params=pltpu.CompilerParams(
            dimension_semantics=("arbitrary", "arbitrary"), vmem_limit_bytes=VMEM_LIMIT),
        name="pool_mixer",
    )(x, past16, g, w, scale)


def _mlp_kernel(*refs, fc, with_attn, with_final):
    refs = list(refs)
    x_ref = refs.pop(0)
    if with_attn:
        o_ref = refs.pop(0)
        wo_ref = refs.pop(0)
    g_ref = refs.pop(0)
    up_ref = refs.pop(0)
    down_ref = refs.pop(0)
    if with_final:
        gf_ref = refs.pop(0)
    out_ref = refs.pop(0)

    x = x_ref[...]
    if with_attn:
        x = x + jnp.dot(o_ref[...], wo_ref[...], preferred_element_type=jnp.float32)
    h = _rms(x, g_ref[...]).astype(jnp.bfloat16)
    acc = x
    for c in range(D_FF // fc):
        u = jnp.dot(h, up_ref[:, c * fc:(c + 1) * fc], preferred_element_type=jnp.float32)
        u = jnp.maximum(u, 0.0)
        a = (u * u).astype(jnp.bfloat16)
        acc = acc + jnp.dot(a, down_ref[c * fc:(c + 1) * fc, :], preferred_element_type=jnp.float32)
    if with_final:
        acc = _rms(acc, gf_ref[...])
    out_ref[...] = acc


def _mlp_layer(x, g, up, down, tm, fc, attn=None, final_g=None):
    N, D = x.shape
    row = lambda i: (i, 0)
    args, specs = [x], [pl.BlockSpec((tm, D), row)]
    if attn is not None:
        o, wo = attn
        args += [o, wo]
        specs += [pl.BlockSpec((tm, Q_W), row), _const_spec(wo.shape)]
    args += [g, up, down]
    specs += [_const_spec((1, D)), _const_spec(up.shape), _const_spec(down.shape)]
    if final_g is not None:
        args.append(final_g)
        specs.append(_const_spec((1, D)))
    return pl.pallas_call(
        functools.partial(_mlp_kernel, fc=fc, with_attn=attn is not None, with_final=final_g is not None),
        grid=(N // tm,),
        in_specs=specs,
        out_specs=pl.BlockSpec((tm, D), row),
        out_shape=jax.ShapeDtypeStruct((N, D), jnp.float32),
        compiler_params=pltpu.CompilerParams(
            dimension_semantics=("arbitrary",), vmem_limit_bytes=VMEM_LIMIT),
        name="mlp_attn_out" if attn is not None else "mlp",
    )(*args)


def _rope_block(xb, tab_ref, special):
    o = LANES if special else 0
    c = tab_ref[:, o:o + LANES]
    s1 = tab_ref[:, 2 * LANES + o:3 * LANES + o]
    s2 = tab_ref[:, 4 * LANES + o:5 * LANES + o]
    return xb * c + pltpu.roll(xb, LANES - HEAD_DIM // 2, 1) * s1 + pltpu.roll(xb, HEAD_DIM // 2, 1) * s2


def _rope_rows(p, c, s):
    half = HEAD_DIM // 2
    r = p.reshape(p.shape[0] // HEAD_DIM, 2, half, p.shape[1])
    x1, x2 = r[:, 0], r[:, 1]
    out = jnp.stack([x1 * c - x2 * s, x2 * c + x1 * s], axis=1)
    return out.reshape(p.shape)


def _proj_kernel(x_ref, g_ref, w_ref, wt_ref, tab_ref, tabt_ref,
                 kc_ref, vc_ref, kp_ref, kw_ref, kib_ref, qt_ref, qit_ref, wit_ref, vt_ref):
    h = _rms(x_ref[0], g_ref[...]).astype(jnp.bfloat16)

    def proj(off, width):
        return jnp.dot(h, w_ref[:, off:off + width], preferred_element_type=jnp.float32)

    def rope(p, special=False):
        return jnp.concatenate(
            [_rope_block(p[:, j:j + LANES], tab_ref, special) for j in range(0, p.shape[1], LANES)], axis=-1)

    kc_ref[0] = rope(proj(OFF_KC, KV_W))
    vc_ref[0] = proj(OFF_VC, KV_W)
    kp_ref[0] = rope(proj(OFF_KP, KV_PAD_W)).astype(kp_ref.dtype)
    kw = rope(proj(OFF_KW, LANES), special=True)
    kw_ref[0] = kw
    kib_ref[0] = kw.astype(kib_ref.dtype)

    def proj_t(row, n):
        return lax.dot_general(wt_ref[row:row + n, :], h, _NT, preferred_element_type=jnp.float32)

    half = HEAD_DIM // 2
    c, s = tabt_ref[0:half, :], tabt_ref[half:HEAD_DIM, :]
    qt_ref[0] = _rope_rows(proj_t(ROW_Q, Q_W), c * LOG2E, s * LOG2E).astype(qt_ref.dtype)
    qit_ref[0] = _rope_rows(proj_t(ROW_QI, QI_W), c, s).astype(qit_ref.dtype)
    vt = proj_t(ROW_V, KV_PAD_W)
    wit_ref[0] = vt[ROW_WI_IN_V:ROW_WI_IN_V + N_IDX_HEADS, :] * (N_IDX_HEADS ** -0.5)
    slot_row = lax.broadcasted_iota(jnp.int32, vt.shape, 0) % LANES
    vt = jnp.where(slot_row < HEAD_DIM, vt, jnp.where(slot_row == HEAD_DIM, 1.0, 0.0))
    vt_ref[0, 0] = vt.astype(vt_ref.dtype)


def _proj_layer(x, g, w_rowmajor, w_transposed, tab, tabt, tm):
    B, T, D = x.shape
    blk = lambda w: pl.BlockSpec((1, tm, w), lambda b, t: (b, t, 0))
    blk_t = lambda r: pl.BlockSpec((1, r, tm), lambda b, t: (b, 0, t))
    bf, f32 = jnp.bfloat16, jnp.float32
    sds = jax.ShapeDtypeStruct
    return pl.pallas_call(
        _proj_kernel,
        grid=(B, T // tm),
        in_specs=[
            blk(D),
            _const_spec((1, D)),
            _const_spec(w_rowmajor.shape),
            _const_spec(w_transposed.shape),
            pl.BlockSpec((tm, tab.shape[1]), lambda b, t: (t, 0)),
            pl.BlockSpec((HEAD_DIM, tm), lambda b, t: (0, t)),
        ],
        out_specs=[blk(KV_W), blk(KV_W), blk(KV_PAD_W), blk(LANES), blk(LANES),
                   blk_t(Q_W), blk_t(QI_W), blk_t(N_IDX_HEADS),
                   pl.BlockSpec((1, 1, KV_PAD_W, tm), lambda b, t: (b, t, 0, 0))],
        out_shape=[sds((B, T, KV_W), f32), sds((B, T, KV_W), f32), sds((B, T, KV_PAD_W), bf),
                   sds((B, T, LANES), f32), sds((B, T, LANES), bf),
                   sds((B, Q_W, T), bf), sds((B, QI_W, T), bf), sds((B, N_IDX_HEADS, T), f32),
                   sds((B, T // tm, KV_PAD_W, tm), bf)],
        compiler_params=pltpu.CompilerParams(
            dimension_semantics=("arbitrary", "arbitrary"), vmem_limit_bytes=VMEM_LIMIT),
        name="attn_in_proj",
    )(x, g, w_rowmajor, w_transposed, tab, tabt)


def _attn_kernel(qt_ref, qit_ref, wit_ref, kp_ref, vt_ref, kib_ref, o_ref, key_ref, f_ref, s0_ref, s1_ref, oacc_ref, *,
                 tq, tq_real, n_keys_real, pos0, n_sel):
    kb = KEY_BLOCK
    t = pl.program_id(1)
    lane_q = lax.broadcasted_iota(jnp.int32, (1, tq), 1)
    q_limit = ((pos0 + t * tq + lane_q) // CHUNK + 1) * CHUNK
    qvalid = lane_q < tq_real
    n_reach = jnp.minimum(((pos0 + (t + 1) * tq - 1) // CHUNK + 1) * CHUNK, n_keys_real)
    nkb = (n_reach + kb - 1) // kb
    n_unvisited = jnp.maximum(n_keys_real - nkb * kb, 0)
    key_row = lax.broadcasted_iota(jnp.int32, (kb, tq), 0)

    def block_start(j):
        return pl.multiple_of(j * kb, kb)

    def for_blocks(body, carry):
        def two(i, c):
            return body(2 * i + 1, body(2 * i, c))
        carry = lax.fori_loop(0, nkb // 2, two, carry)
        return lax.fori_loop(nkb // 2 * 2, nkb, body, carry)

    def admissible(k0):
        return key_row < q_limit - k0

    wi = wit_ref[0]

    def score_block(j, carry):
        k0 = block_start(j)
        kx = kib_ref[0, pl.ds(k0, kb), 0:IDX_DIM]
        acc = jnp.zeros((kb, tq), jnp.float32)
        for hp in range(N_IDX_HEADS // 2):
            qi2 = jnp.concatenate(
                [qit_ref[0, (2 * hp + u) * IDX_DIM:(2 * hp + u + 1) * IDX_DIM, :] for u in range(2)], axis=1)
            lg = jnp.dot(kx, qi2, preferred_element_type=jnp.float32)
            for u in range(2):
                hh = 2 * hp + u
                acc = acc + jnp.maximum(lg[:, u * tq:(u + 1) * tq], 0.0) * wi[hh:hh + 1, :]
        sc = jnp.where(admissible(k0), acc, NEG)
        if n_keys_real % kb:
            sc = jnp.where(key_row < n_keys_real - k0, sc, -jnp.inf)
        bits = lax.bitcast_convert_type(sc, jnp.int32)
        key_ref[pl.ds(k0, kb), :] = bits ^ ((bits >> 31) & jnp.int32(0x7FFFFFFF))
        return carry

    for_blocks(score_block, 0)

    def count(pred):
        def body(j, acc):
            blk = key_ref[pl.ds(block_start(j), kb), :]
            return acc + _fold_rows(jnp.add, jnp.where(pred(blk, j), 1, 0).astype(jnp.int32))
        acc = for_blocks(body, jnp.zeros((SUBLANES, tq), jnp.int32))
        return jnp.sum(acc, axis=0, keepdims=True)

    def count_ge(cand):
        return count(lambda blk, j: blk >= cand) + jnp.where(cand <= NEG_KEY, n_unvisited, 0)

    int_min = jnp.int32(-2 ** 31)
    thr = jnp.where(count_ge(jnp.zeros((1, tq), jnp.int32)) >= n_sel, jnp.int32(0), int_min)

    def radix_step(i, thr):
        cand = thr + jnp.left_shift(jnp.int32(1), 30 - i)
        return jnp.where(count_ge(cand) >= n_sel, cand, thr)

    thr = lax.fori_loop(0, 31, radix_step, thr)

    n_gt = count(lambda blk, j: blk > thr) + jnp.where(thr < NEG_KEY, n_unvisited, 0)
    n_eq = count(lambda blk, j: blk == thr) + jnp.where(thr == NEG_KEY, n_unvisited, 0)
    n_eq_adm = count(lambda blk, j: (blk == thr) & admissible(block_start(j)))
    need = n_sel - n_gt
    tie = (n_eq > need) & (n_eq_adm > 0) & qvalid
    any_tie = jnp.max(tie.astype(jnp.int32)) > 0

    def write_mask(exact_ties):
        def body(j, run):
            k0 = block_start(j)
            kblk = key_ref[pl.ds(k0, kb), :]
            if exact_ties:
                r = lax.broadcasted_iota(jnp.int32, (kb, kb), 0)
                c = lax.broadcasted_iota(jnp.int32, (kb, kb), 1)
                tri = jnp.where(r >= c, 1.0, 0.0).astype(jnp.bfloat16)
                eqb = kblk == thr
                eqf = jnp.where(eqb, 1.0, 0.0)
                rank = jnp.dot(tri, eqf.astype(jnp.bfloat16), preferred_element_type=jnp.float32) + run
                run = run + jnp.sum(eqf, axis=0, keepdims=True)
                sel = (kblk > thr) | (eqb & (rank <= need.astype(jnp.float32)))
            else:
                sel = kblk >= thr
            f_ref[pl.ds(k0, kb), :] = jnp.where(sel, jnp.where(admissible(k0), jnp.inf, NEG), -jnp.inf)
            return run

        for_blocks(body, jnp.zeros((1, tq), jnp.float32))

    @pl.when(any_tie)
    def _():
        write_mask(True)

    @pl.when(jnp.logical_not(any_tie))
    def _():
        write_mask(False)

    gw = KV_GROUP * tq
    s_refs = (s0_ref, s1_ref)

    def logits_block(g, j):
        k0 = block_start(j)
        q4 = jnp.concatenate(
            [qt_ref[0, (g * KV_GROUP + r) * HEAD_DIM:(g * KV_GROUP + r + 1) * HEAD_DIM, :] for r in range(KV_GROUP)],
            axis=1)
        kblk = kp_ref[0, pl.ds(k0, kb), g * LANES:g * LANES + HEAD_DIM]
        s = jnp.dot(kblk, q4, preferred_element_type=jnp.float32)
        f = f_ref[pl.ds(k0, kb), :]
        s = jnp.minimum(s, jnp.concatenate([f] * KV_GROUP, axis=1))
        s_refs[g % 2][pl.ds(k0, kb), :] = s
        return _fold_rows(jnp.maximum, s)

    def values_block(g, j, m):
        p = jnp.exp2(s_refs[g % 2][pl.ds(block_start(j), kb), :] - m).astype(jnp.bfloat16)
        oacc_ref[...] += jnp.dot(vt_ref[0, j, g * LANES:(g + 1) * LANES, :], p,
                                 preferred_element_type=jnp.float32)

    m_init = jnp.full((SUBLANES, gw), -jnp.inf, jnp.float32)
    m_acc = for_blocks(lambda j, a: jnp.maximum(a, logits_block(0, j)), m_init)
    for g in range(N_KV_HEADS):
        m = jnp.max(m_acc, axis=0, keepdims=True)
        oacc_ref[...] = jnp.zeros(oacc_ref.shape, jnp.float32)

        def fused(j, a, g=g, m=m):
            if g + 1 < N_KV_HEADS:
                a = jnp.maximum(a, logits_block(g + 1, j))
            values_block(g, j, m)
            return a

        m_acc = for_blocks(fused, m_init)
        oacc = oacc_ref[...]
        on = oacc[0:HEAD_DIM] / oacc[HEAD_DIM:HEAD_DIM + 1]
        for u in range(KV_GROUP // 2):
            pair = jnp.concatenate([on[:, (2 * u) * tq:(2 * u + 1) * tq], on[:, (2 * u + 1) * tq:(2 * u + 2) * tq]],
                                   axis=0)
            c0 = (g * KV_GROUP + 2 * u) * HEAD_DIM
            o_ref[0, :, c0:c0 + 2 * HEAD_DIM] = pair.T.astype(o_ref.dtype)


def _attn_layer(qt, qit, wit, kp, vt, kib, *, tq, tq_real, n_keys_real, pos0, n_sel):
    B, _, T = qt.shape
    n_keys = kp.shape[1]
    qblk = lambda r: pl.BlockSpec((1, r, tq), lambda b, t: (b, 0, t))
    kblk = lambda w: pl.BlockSpec((1, n_keys, w), lambda b, t: (b, 0, 0))
    return pl.pallas_call(
        functools.partial(_attn_kernel, tq=tq, tq_real=tq_real, n_keys_real=n_keys_real, pos0=pos0, n_sel=n_sel),
        grid=(B, T // tq),
        in_specs=[qblk(Q_W), qblk(QI_W), qblk(N_IDX_HEADS), kblk(KV_PAD_W),
                  pl.BlockSpec((1,) + vt.shape[1:], lambda b, t: (b, 0, 0, 0)), kblk(LANES)],
        out_specs=pl.BlockSpec((1, tq, Q_W), lambda b, t: (b, t, 0)),
        out_shape=jax.ShapeDtypeStruct((B, T, Q_W), jnp.bfloat16),
        scratch_shapes=[pltpu.VMEM((n_keys, tq), jnp.int32), pltpu.VMEM((n_keys, tq), jnp.float32),
                        pltpu.VMEM((n_keys, KV_GROUP * tq), jnp.float32),
                        pltpu.VMEM((n_keys, KV_GROUP * tq), jnp.float32),
                        pltpu.VMEM((LANES, KV_GROUP * tq), jnp.float32)],
        compiler_params=pltpu.CompilerParams(
            dimension_semantics=("arbitrary", "arbitrary"), vmem_limit_bytes=VMEM_LIMIT),
        name="dsa_attention",
    )(qt, qit, wit, kp, vt, kib)


def _pad_heads(a):
    lead = a.shape[:-1]
    a = a.reshape(*lead, N_KV_HEADS, HEAD_DIM)
    a = jnp.pad(a, [(0, 0)] * len(lead) + [(0, 0), (0, LANES - HEAD_DIM)])
    return a.reshape(*lead, KV_PAD_W)


def _pack_w_in(w_in):
    o = 0
    wq = w_in[:, o:o + Q_W] * (HEAD_DIM ** -0.5); o += Q_W
    wk = w_in[:, o:o + KV_W]; o += KV_W
    wv = w_in[:, o:o + KV_W]; o += KV_W
    wqi = w_in[:, o:o + QI_W] * (IDX_DIM ** -0.5); o += QI_W
    wki = w_in[:, o:o + IDX_DIM]; o += IDX_DIM
    wwi = w_in[:, o:o + N_IDX_HEADS]
    wkw = jnp.pad(jnp.concatenate([wki, wwi], axis=1), ((0, 0), (0, LANES - IDX_DIM - N_IDX_HEADS)))
    rowmajor = jnp.concatenate([wk, wv, _pad_heads(wk), wkw], axis=1)
    wv_slots = _pad_heads(wv)
    wv_slots = wv_slots.at[:, ROW_WI_IN_V:ROW_WI_IN_V + N_IDX_HEADS].set(wwi)
    transposed = jnp.concatenate([wq, wqi, wv_slots], axis=1).T
    return rowmajor.astype(jnp.bfloat16), transposed.astype(jnp.bfloat16)


def _rope_tables(pos):
    inv_freq = 1.0 / (ROPE_THETA ** (jnp.arange(0, HEAD_DIM, 2, dtype=jnp.float32) / HEAD_DIM))
    ang = pos.astype(jnp.float32)[:, None] * inv_freq[None, :]
    c, s = jnp.cos(ang), jnp.sin(ang)
    z = jnp.zeros_like(c)
    n = pos.shape[0]
    zpad = jnp.zeros((n, LANES - HEAD_DIM), jnp.float32)
    wi_scale = jnp.full((n, N_IDX_HEADS), N_IDX_HEADS ** -0.5, jnp.float32)
    c_kw = jnp.concatenate([c, c, wi_scale, jnp.zeros((n, LANES - HEAD_DIM - N_IDX_HEADS), jnp.float32)], axis=1)
    tab = jnp.concatenate([
        jnp.concatenate([c, c, c, c], axis=1), c_kw,
        jnp.concatenate([-s, z, -s, z], axis=1), jnp.concatenate([-s, z, zpad], axis=1),
        jnp.concatenate([z, s, z, s], axis=1), jnp.concatenate([z, s, zpad], axis=1),
    ], axis=1)
    tabt = jnp.concatenate([c, s], axis=1).T
    return tab, tabt


def _round_up(n, m):
    return (n + m - 1) // m * m


def _value_slots_t(v):
    B, n, _ = v.shape
    vt = jnp.transpose(v.reshape(B, n, N_KV_HEADS, HEAD_DIM), (0, 2, 3, 1))
    ones = jnp.ones((B, N_KV_HEADS, 1, n), v.dtype)
    zeros = jnp.zeros((B, N_KV_HEADS, LANES - HEAD_DIM - 1, n), v.dtype)
    return jnp.concatenate([vt, ones, zeros], axis=2).reshape(B, KV_PAD_W, n)


def _trunk(x, pos0, pool_past, attn_past, norm_mix, norm_mlp, norm_final, pool_w, pool_scale,
           w_rowmajor, w_transposed, w_o, w_up, w_down):
    B, T, D = x.shape
    bf = jnp.bfloat16
    row = lambda v: v.reshape(1, D)

    if pool_past is None:
        past16 = jnp.zeros((B, POOL_STATE + 1, D), jnp.float32)
    else:
        past16 = jnp.pad(pool_past, ((0, 0), (1, 0), (0, 0)))
    tt = min(T, 512)
    x1, st = _pool_layer(x, past16, row(norm_mix[0]), pool_w[0].astype(bf), row(pool_scale[0]), pos0, tt)
    pool_new = st[:, 1:][None]
    tm = min(B * T, 512)
    x2 = _mlp_layer(x1.reshape(B * T, D), row(norm_mlp[0]), w_up[0], w_down[0], tm, 512).reshape(B, T, D)

    tab, tabt = _rope_tables(pos0 + jnp.arange(T, dtype=jnp.int32))
    kc, vc, kp, kw, kib, qt, qit, wit, vt = _proj_layer(
        x2, row(norm_mix[1]), w_rowmajor, w_transposed, tab, tabt, min(T, KEY_BLOCK))
    k_new = kc.reshape(1, B, T, N_KV_HEADS, HEAD_DIM)
    v_new = vc.reshape(1, B, T, N_KV_HEADS, HEAD_DIM)
    ki_new = kw[:, :, :IDX_DIM][None]

    if attn_past is None:
        n_real = T
        kp_all, vt_all, kib_all = kp, vt, kib
    else:
        ck, cv, cki = attn_past
        P = ck.shape[1]
        n_real = P + T
        n_keys = _round_up(n_real, KEY_BLOCK)
        padk = lambda a: jnp.pad(a, ((0, 0), (0, n_keys - n_real), (0, 0)))
        kp_all = padk(jnp.concatenate([_pad_heads(ck.reshape(B, P, KV_W)).astype(bf), kp], axis=1))
        kib_all = padk(jnp.concatenate(
            [jnp.pad(cki, ((0, 0), (0, 0), (0, LANES - IDX_DIM))).astype(bf), kib], axis=1))
        v_all = jnp.concatenate([cv.reshape(B, P, KV_W), vc], axis=1)
        vt_all = jnp.pad(_value_slots_t(v_all).astype(bf), ((0, 0), (0, 0), (0, n_keys - n_real)))
        vt_all = jnp.transpose(vt_all.reshape(B, KV_PAD_W, n_keys // KEY_BLOCK, KEY_BLOCK), (0, 2, 1, 3))
    tq = min(Q_TILE, _round_up(T, LANES))
    t_pad = _round_up(T, tq)
    padq = lambda a: jnp.pad(a, ((0, 0), (0, 0), (0, t_pad - T)))
    n_sel = min(TOPK_MAX, n_real // 4)
    o = _attn_layer(padq(qt), padq(qit), padq(wit), kp_all, vt_all, kib_all,
                    tq=tq, tq_real=min(T, tq), n_keys_real=n_real, pos0=pos0, n_sel=n_sel)
    o = o[:, :T].reshape(B * T, Q_W)

    y = _mlp_layer(x2.reshape(B * T, D), row(norm_mlp[1]), w_up[1], w_down[1], tm, 512,
                   attn=(o, w_o), final_g=row(norm_final)).reshape(B, T, D)
    return y, pool_new, k_new, v_new, ki_new


def kernel(x_prompt, x_sample, state_pool, cache_k, cache_v, cache_kidx, norm_mix, norm_mlp, norm_final,
           pool_w, pool_scale, attn_w_in, attn_w_o, mlp_w_up, mlp_w_down):
    bf = jnp.bfloat16
    shared = (norm_mix, norm_mlp, norm_final, pool_w, pool_scale, *_pack_w_in(attn_w_in[0]),
              attn_w_o[0].astype(bf), mlp_w_up.astype(bf), mlp_w_down.astype(bf))
    y_p, pool_p, k_p, v_p, ki_p = _trunk(x_prompt, 0, None, None, *shared)
    y_s, pool_s, k_s, v_s, ki_s = _trunk(
        x_sample, cache_k.shape[2], state_pool[0], (cache_k[0], cache_v[0], cache_kidx[0]), *shared)
    return (y_p, y_s, pool_p, pool_s, k_p, v_p, ki_p, k_s, v_s, ki_s)
```

```python
import functools
import math
import struct

import jax
import jax.numpy as jnp
from jax import lax
from jax.experimental import pallas as pl
from jax.experimental.pallas import tpu as pltpu

D_MODEL = 1024
CHUNK = 64
POOL_WINDOWS = (2, 4, 8, 16)
POOL_GROUP = D_MODEL // len(POOL_WINDOWS)
POOL_STATE = max(POOL_WINDOWS) - 1
N_HEADS = 16
HEAD_DIM = 64
N_KV_HEADS = 4
KV_GROUP = N_HEADS // N_KV_HEADS
N_IDX_HEADS = 8
IDX_DIM = 64
TOPK_MAX = 256
ROPE_THETA = 10000.0
D_FF = 4 * D_MODEL
EPS = 1e-6
NEG = -1e30
Q_W = N_HEADS * HEAD_DIM
KV_W = N_KV_HEADS * HEAD_DIM
QI_W = N_IDX_HEADS * IDX_DIM
LOG2E = math.log2(math.e)

SUBLANES = 8
PACKED_ROWS = 2 * SUBLANES
LANES = 128
KV_PAD_W = N_KV_HEADS * LANES
KEY_BLOCK = 256
Q_TILE = 256

OFF_KC = 0
OFF_VC = OFF_KC + KV_W
OFF_KP = OFF_VC + KV_W
OFF_KW = OFF_KP + KV_PAD_W
PROJ_ROWMAJOR_W = OFF_KW + LANES
ROW_Q = 0
ROW_QI = ROW_Q + Q_W
ROW_V = ROW_QI + QI_W
ROW_WI_IN_V = HEAD_DIM + SUBLANES
PROJ_TRANSPOSED_ROWS = ROW_V + KV_PAD_W

VMEM_LIMIT = 56 * 1024 * 1024

_NT = (((1,), (1,)), ((), ()))


def _f32_order_key(x):
    b = struct.unpack("<i", struct.pack("<f", x))[0]
    return b ^ ((b >> 31) & 0x7FFFFFFF)


def _signed16(v):
    return v - (1 << 16) if v >= (1 << 15) else v


NEG_KEY = _f32_order_key(NEG)
NEG_KEY_HI = NEG_KEY >> 16
NEG_KEY_LO = _signed16((NEG_KEY & 0xFFFF) ^ 0x8000)


def _rms(x, g):
    return x * lax.rsqrt(jnp.mean(x * x, axis=-1, keepdims=True) + EPS) * g


def _const_spec(shape):
    n = len(shape)
    return pl.BlockSpec(shape, lambda *_: (0,) * n, pipeline_mode=pl.Buffered(1))


def _tree(op, parts):
    parts = list(parts)
    while len(parts) > 1:
        parts = [op(parts[i], parts[i + 1]) if i + 1 < len(parts) else parts[i] for i in range(0, len(parts), 2)]
    return parts[0]


def _fold_rows(op, x, rows=SUBLANES):
    return _tree(op, [x[r:r + rows] for r in range(0, x.shape[0], rows)])


def _pool_kernel(x_ref, past_ref, g_ref, w_ref, scale_ref, x1_ref, st_ref, hp_ref, *, tt, pos0):
    t = pl.program_id(1)
    halo = POOL_STATE + 1

    @pl.when(t == 0)
    def _():
        hp_ref[0:halo, :] = past_ref[0]

    x = x_ref[0]
    h = _rms(x, g_ref[...])
    hp_ref[halo:halo + tt, :] = h
    pos = pos0 + t * tt + lax.broadcasted_iota(jnp.int32, (tt, 1), 0)
    for g, win in enumerate(POOL_WINDOWS):
        c0, c1 = g * POOL_GROUP, (g + 1) * POOL_GROUP
        ws = h[:, c0:c1]
        for j in range(1, win):
            ws = ws + hp_ref[halo - j:halo - j + tt, c0:c1]
        cnt = jnp.minimum(pos + 1, win).astype(jnp.float32)
        diff = ws / cnt - h[:, c0:c1]
        y = jnp.dot(diff.astype(jnp.bfloat16), w_ref[g], preferred_element_type=jnp.float32)
        x1_ref[0, :, c0:c1] = x[:, c0:c1] + y * scale_ref[:, c0:c1]
    tail = hp_ref[tt:tt + halo, :]
    st_ref[0] = tail
    hp_ref[0:halo, :] = tail


def _pool_layer(x, past16, g, w, scale, pos0, tt):
    B, T, D = x.shape
    halo = POOL_STATE + 1
    return pl.pallas_call(
        functools.partial(_pool_kernel, tt=tt, pos0=pos0),
        grid=(B, T // tt),
        in_specs=[
            pl.BlockSpec((1, tt, D), lambda b, t: (b, t, 0)),
            pl.BlockSpec((1, halo, D), lambda b, t: (b, 0, 0)),
            _const_spec((1, D)),
            _const_spec(w.shape),
            _const_spec((1, D)),
        ],
        out_specs=[
            pl.BlockSpec((1, tt, D), lambda b, t: (b, t, 0)),
            pl.BlockSpec((1, halo, D), lambda b, t: (b, 0, 0)),
        ],
        out_shape=[
            jax.ShapeDtypeStruct((B, T, D), jnp.float32),
            jax.ShapeDtypeStruct((B, halo, D), jnp.float32),
        ],
        scratch_shapes=[pltpu.VMEM((halo + tt, D), jnp.float32)],
        compiler_params=pltpu.CompilerParams(
            dimension_semantics=("arbitrary", "arbitrary"), vmem_limit_bytes=VMEM_LIMIT),
        name="pool_mixer",
    )(x, past16, g, w, scale)


def _mlp_kernel(*refs, fc, with_attn, with_final):
    refs = list(refs)
    x_ref = refs.pop(0)
    if with_attn:
        o_ref = refs.pop(0)
        wo_ref = refs.pop(0)
    g_ref = refs.pop(0)
    up_ref = refs.pop(0)
    down_ref = refs.pop(0)
    if with_final:
        gf_ref = refs.pop(0)
    out_ref = refs.pop(0)

    x = x_ref[...]
    if with_attn:
        x = x + jnp.dot(o_ref[...], wo_ref[...], preferred_element_type=jnp.float32)
    h = _rms(x, g_ref[...]).astype(jnp.bfloat16)
    acc = x
    for c in range(D_FF // fc):
        u = jnp.dot(h, up_ref[:, c * fc:(c + 1) * fc], preferred_element_type=jnp.float32)
        u = jnp.maximum(u, 0.0)
        a = (u * u).astype(jnp.bfloat16)
        acc = acc + jnp.dot(a, down_ref[c * fc:(c + 1) * fc, :], preferred_element_type=jnp.float32)
    if with_final:
        acc = _rms(acc, gf_ref[...])
    out_ref[...] = acc


def _mlp_layer(x, g, up, down, tm, fc, attn=None, final_g=None):
    N, D = x.shape
    row = lambda i: (i, 0)
    args, specs = [x], [pl.BlockSpec((tm, D), row)]
    if attn is not None:
        o, wo = attn
        args += [o, wo]
        specs += [pl.BlockSpec((tm, Q_W), row), _const_spec(wo.shape)]
    args += [g, up, down]
    specs += [_const_spec((1, D)), _const_spec(up.shape), _const_spec(down.shape)]
    if final_g is not None:
        args.append(final_g)
        specs.append(_const_spec((1, D)))
    return pl.pallas_call(
        functools.partial(_mlp_kernel, fc=fc, with_attn=attn is not None, with_final=final_g is not None),
        grid=(N // tm,),
        in_specs=specs,
        out_specs=pl.BlockSpec((tm, D), row),
        out_shape=jax.ShapeDtypeStruct((N, D), jnp.float32),
        compiler_params=pltpu.CompilerParams(
            dimension_semantics=("arbitrary",), vmem_limit_bytes=VMEM_LIMIT),
        name="mlp_attn_out" if attn is not None else "mlp",
    )(*args)


def _rope_block(xb, tab_ref, special):
    o = LANES if special else 0
    c = tab_ref[:, o:o + LANES]
    s1 = tab_ref[:, 2 * LANES + o:3 * LANES + o]
    s2 = tab_ref[:, 4 * LANES + o:5 * LANES + o]
    return xb * c + pltpu.roll(xb, LANES - HEAD_DIM // 2, 1) * s1 + pltpu.roll(xb, HEAD_DIM // 2, 1) * s2


def _rope_rows(p, c, s):
    half = HEAD_DIM // 2
    r = p.reshape(p.shape[0] // HEAD_DIM, 2, half, p.shape[1])
    x1, x2 = r[:, 0], r[:, 1]
    out = jnp.stack([x1 * c - x2 * s, x2 * c + x1 * s], axis=1)
    return out.reshape(p.shape)


def _proj_kernel(x_ref, g_ref, w_ref, wt_ref, tab_ref, tabt_ref,
                 kc_ref, vc_ref, kp_ref, kw_ref, kib_ref, qt_ref, qit_ref, wit_ref, vt_ref):
    h = _rms(x_ref[0], g_ref[...]).astype(jnp.bfloat16)

    def proj(off, width):
        return jnp.dot(h, w_ref[:, off:off + width], preferred_element_type=jnp.float32)

    def rope(p, special=False):
        return jnp.concatenate(
            [_rope_block(p[:, j:j + LANES], tab_ref, special) for j in range(0, p.shape[1], LANES)], axis=-1)

    kc_ref[0] = rope(proj(OFF_KC, KV_W))
    vc_ref[0] = proj(OFF_VC, KV_W)
    kp_ref[0] = rope(proj(OFF_KP, KV_PAD_W)).astype(kp_ref.dtype)
    kw = rope(proj(OFF_KW, LANES), special=True)
    kw_ref[0] = kw
    kib_ref[0] = kw.astype(kib_ref.dtype)

    def proj_t(row, n):
        return lax.dot_general(wt_ref[row:row + n, :], h, _NT, preferred_element_type=jnp.float32)

    half = HEAD_DIM // 2
    c, s = tabt_ref[0:half, :], tabt_ref[half:HEAD_DIM, :]
    qt_ref[0] = _rope_rows(proj_t(ROW_Q, Q_W), c * LOG2E, s * LOG2E).astype(qt_ref.dtype)
    qit_ref[0] = _rope_rows(proj_t(ROW_QI, QI_W), c, s).astype(qit_ref.dtype)
    vt = proj_t(ROW_V, KV_PAD_W)
    wit_ref[0] = vt[ROW_WI_IN_V:ROW_WI_IN_V + N_IDX_HEADS, :] * (N_IDX_HEADS ** -0.5)
    slot_row = lax.broadcasted_iota(jnp.int32, vt.shape, 0) % LANES
    vt = jnp.where(slot_row < HEAD_DIM, vt, jnp.where(slot_row == HEAD_DIM, 1.0, 0.0))
    vt_ref[0, 0] = vt.astype(vt_ref.dtype)


def _proj_layer(x, g, w_rowmajor, w_transposed, tab, tabt, tm):
    B, T, D = x.shape
    blk = lambda w: pl.BlockSpec((1, tm, w), lambda b, t: (b, t, 0))
    blk_t = lambda r: pl.BlockSpec((1, r, tm), lambda b, t: (b, 0, t))
    bf, f32 = jnp.bfloat16, jnp.float32
    sds = jax.ShapeDtypeStruct
    return pl.pallas_call(
        _proj_kernel,
        grid=(B, T // tm),
        in_specs=[
            blk(D),
            _const_spec((1, D)),
            _const_spec(w_rowmajor.shape),
            _const_spec(w_transposed.shape),
            pl.BlockSpec((tm, tab.shape[1]), lambda b, t: (t, 0)),
            pl.BlockSpec((HEAD_DIM, tm), lambda b, t: (0, t)),
        ],
        out_specs=[blk(KV_W), blk(KV_W), blk(KV_PAD_W), blk(LANES), blk(LANES),
                   blk_t(Q_W), blk_t(QI_W), blk_t(N_IDX_HEADS),
                   pl.BlockSpec((1, 1, KV_PAD_W, tm), lambda b, t: (b, t, 0, 0))],
        out_shape=[sds((B, T, KV_W), f32), sds((B, T, KV_W), f32), sds((B, T, KV_PAD_W), bf),
                   sds((B, T, LANES), f32), sds((B, T, LANES), bf),
                   sds((B, Q_W, T), bf), sds((B, QI_W, T), bf), sds((B, N_IDX_HEADS, T), f32),
                   sds((B, T // tm, KV_PAD_W, tm), bf)],
        compiler_params=pltpu.CompilerParams(
            dimension_semantics=("arbitrary", "arbitrary"), vmem_limit_bytes=VMEM_LIMIT),
        name="attn_in_proj",
    )(x, g, w_rowmajor, w_transposed, tab, tabt)


def _attn_kernel(qt_ref, qit_ref, wit_ref, kp_ref, vt_ref, kib_ref, o_ref,
                 key_ref, hi_ref, lo_ref, lo2_ref, f_ref, s0_ref, s1_ref, oacc_ref, *,
                 tq, tq_real, n_keys_real, pos0, n_sel):
    kb = KEY_BLOCK
    t = pl.program_id(1)
    lane_q = lax.broadcasted_iota(jnp.int32, (1, tq), 1)
    q_limit = ((pos0 + t * tq + lane_q) // CHUNK + 1) * CHUNK
    qvalid = lane_q < tq_real
    n_reach = jnp.minimum(((pos0 + (t + 1) * tq - 1) // CHUNK + 1) * CHUNK, n_keys_real)
    nkb = (n_reach + kb - 1) // kb
    n_unvisited = jnp.maximum(n_keys_real - nkb * kb, 0)
    key_row = lax.broadcasted_iota(jnp.int32, (kb, tq), 0)

    def block_start(j):
        return pl.multiple_of(j * kb, kb)

    def for_blocks(body, carry):
        def two(i, c):
            return body(2 * i + 1, body(2 * i, c))
        carry = lax.fori_loop(0, nkb // 2, two, carry)
        return lax.fori_loop(nkb // 2 * 2, nkb, body, carry)

    def admissible(k0):
        return key_row < q_limit - k0

    wi = wit_ref[0]

    def score_block(j, carry):
        k0 = block_start(j)
        kx = kib_ref[0, pl.ds(k0, kb), 0:IDX_DIM]
        acc = jnp.zeros((kb, tq), jnp.float32)
        for hp in range(N_IDX_HEADS // 2):
            qi2 = jnp.concatenate(
                [qit_ref[0, (2 * hp + u) * IDX_DIM:(2 * hp + u + 1) * IDX_DIM, :] for u in range(2)], axis=1)
            lg = jnp.dot(kx, qi2, preferred_element_type=jnp.float32)
            for u in range(2):
                hh = 2 * hp + u
                acc = acc + jnp.maximum(lg[:, u * tq:(u + 1) * tq], 0.0) * wi[hh:hh + 1, :]
        sc = jnp.where(admissible(k0), acc, NEG)
        if n_keys_real % kb:
            sc = jnp.where(key_row < n_keys_real - k0, sc, -jnp.inf)
        bits = lax.bitcast_convert_type(sc, jnp.int32)
        key = bits ^ ((bits >> 31) & jnp.int32(0x7FFFFFFF))
        key_ref[pl.ds(k0, kb), :] = key
        hi_ref[pl.ds(k0, kb), :] = (key >> 16).astype(jnp.int16)
        lo_ref[pl.ds(k0, kb), :] = (key ^ 0x8000).astype(jnp.int16)
        return carry

    for_blocks(score_block, 0)

    def count(pred):
        def body(j, acc):
            blk = key_ref[pl.ds(block_start(j), kb), :]
            return acc + _fold_rows(jnp.add, jnp.where(pred(blk, j), 1, 0).astype(jnp.int32))
        acc = for_blocks(body, jnp.zeros((SUBLANES, tq), jnp.int32))
        return jnp.sum(acc, axis=0, keepdims=True)

    def count16(ref, pred):
        def body(j, acc):
            hit = jnp.where(pred(ref[pl.ds(block_start(j), kb), :]), jnp.int16(1), jnp.int16(0))
            return acc + _fold_rows(jnp.add, hit, rows=PACKED_ROWS)
        acc = for_blocks(body, jnp.zeros((PACKED_ROWS, tq), jnp.int16))
        return jnp.sum(acc.astype(jnp.int32), axis=0, keepdims=True)

    def as_plane(v):
        tile = jnp.broadcast_to(v, (PACKED_ROWS, tq)).astype(jnp.int16)
        return jnp.concatenate([tile] * (kb // PACKED_ROWS), axis=0)

    def radix16(ref, n_wanted, unvisited_ge):
        def step(i, thr):
            cand = thr + jnp.left_shift(jnp.int32(1), 15 - i)
            plane = as_plane(cand)
            n_ge = count16(ref, lambda blk: blk >= plane) + unvisited_ge(cand)
            return jnp.where(n_ge >= n_wanted, cand, thr)
        return lax.fori_loop(0, 16, step, jnp.full((1, tq), -2 ** 15, jnp.int32))

    thr_hi = radix16(hi_ref, n_sel, lambda cand: jnp.where(cand <= NEG_KEY_HI, n_unvisited, 0))
    hi_plane = as_plane(thr_hi)
    n_gt_hi = count16(hi_ref, lambda blk: blk > hi_plane) + jnp.where(thr_hi < NEG_KEY_HI, n_unvisited, 0)

    def park_block(j, carry):
        rows = pl.ds(block_start(j), kb)
        lo2_ref[rows, :] = jnp.where(hi_ref[rows, :] == hi_plane, lo_ref[rows, :], jnp.int16(-2 ** 15))
        return carry

    for_blocks(park_block, 0)
    thr_lo = radix16(lo2_ref, n_sel - n_gt_hi,
                     lambda cand: jnp.where((thr_hi == NEG_KEY_HI) & (cand <= NEG_KEY_LO), n_unvisited, 0))
    thr = thr_hi * 65536 + (thr_lo + 2 ** 15)

    n_gt = count(lambda blk, j: blk > thr) + jnp.where(thr < NEG_KEY, n_unvisited, 0)
    n_eq = count(lambda blk, j: blk == thr) + jnp.where(thr == NEG_KEY, n_unvisited, 0)
    n_eq_adm = count(lambda blk, j: (blk == thr) & admissible(block_start(j)))
    need = n_sel - n_gt
    tie = (n_eq > need) & (n_eq_adm > 0) & qvalid
    any_tie = jnp.max(tie.astype(jnp.int32)) > 0

    def write_mask(exact_ties):
        def body(j, run):
            k0 = block_start(j)
            kblk = key_ref[pl.ds(k0, kb), :]
            if exact_ties:
                r = lax.broadcasted_iota(jnp.int32, (kb, kb), 0)
                c = lax.broadcasted_iota(jnp.int32, (kb, kb), 1)
                tri = jnp.where(r >= c, 1.0, 0.0).astype(jnp.bfloat16)
                eqb = kblk == thr
                eqf = jnp.where(eqb, 1.0, 0.0)
                rank = jnp.dot(tri, eqf.astype(jnp.bfloat16), preferred_element_type=jnp.float32) + run
                run = run + jnp.sum(eqf, axis=0, keepdims=True)
                sel = (kblk > thr) | (eqb & (rank <= need.astype(jnp.float32)))
            else:
                sel = kblk >= thr
            f_ref[pl.ds(k0, kb), :] = jnp.where(sel, jnp.where(admissible(k0), jnp.inf, NEG), -jnp.inf)
            return run

        for_blocks(body, jnp.zeros((1, tq), jnp.float32))

    @pl.when(any_tie)
    def _():
        write_mask(True)

    @pl.when(jnp.logical_not(any_tie))
    def _():
        write_mask(False)

    gw = KV_GROUP * tq
    s_refs = (s0_ref, s1_ref)

    def logits_block(g, j):
        k0 = block_start(j)
        q4 = jnp.concatenate(
            [qt_ref[0, (g * KV_GROUP + r) * HEAD_DIM:(g * KV_GROUP + r + 1) * HEAD_DIM, :] for r in range(KV_GROUP)],
            axis=1)
        kblk = kp_ref[0, pl.ds(k0, kb), g * LANES:g * LANES + HEAD_DIM]
        s = jnp.dot(kblk, q4, preferred_element_type=jnp.float32)
        f = f_ref[pl.ds(k0, kb), :]
        s = jnp.minimum(s, jnp.concatenate([f] * KV_GROUP, axis=1))
        s_refs[g % 2][pl.ds(k0, kb), :] = s
        return _fold_rows(jnp.maximum, s)

    def values_block(g, j, m):
        p = jnp.exp2(s_refs[g % 2][pl.ds(block_start(j), kb), :] - m).astype(jnp.bfloat16)
        oacc_ref[...] += jnp.dot(vt_ref[0, j, g * LANES:(g + 1) * LANES, :], p,
                                 preferred_element_type=jnp.float32)

    m_init = jnp.full((SUBLANES, gw), -jnp.inf, jnp.float32)
    m_acc = for_blocks(lambda j, a: jnp.maximum(a, logits_block(0, j)), m_init)
    for g in range(N_KV_HEADS):
        m = jnp.max(m_acc, axis=0, keepdims=True)
        oacc_ref[...] = jnp.zeros(oacc_ref.shape, jnp.float32)

        def fused(j, a, g=g, m=m):
            if g + 1 < N_KV_HEADS:
                a = jnp.maximum(a, logits_block(g + 1, j))
            values_block(g, j, m)
            return a

        m_acc = for_blocks(fused, m_init)
        oacc = oacc_ref[...]
        on = oacc[0:HEAD_DIM] / oacc[HEAD_DIM:HEAD_DIM + 1]
        for u in range(KV_GROUP // 2):
            pair = jnp.concatenate([on[:, (2 * u) * tq:(2 * u + 1) * tq], on[:, (2 * u + 1) * tq:(2 * u + 2) * tq]],
                                   axis=0)
            c0 = (g * KV_GROUP + 2 * u) * HEAD_DIM
            o_ref[0, :, c0:c0 + 2 * HEAD_DIM] = pair.T.astype(o_ref.dtype)


def _attn_layer(qt, qit, wit, kp, vt, kib, *, tq, tq_real, n_keys_real, pos0, n_sel):
    B, _, T = qt.shape
    n_keys = kp.shape[1]
    qblk = lambda r: pl.BlockSpec((1, r, tq), lambda b, t: (b, 0, t))
    kblk = lambda w: pl.BlockSpec((1, n_keys, w), lambda b, t: (b, 0, 0))
    return pl.pallas_call(
        functools.partial(_attn_kernel, tq=tq, tq_real=tq_real, n_keys_real=n_keys_real, pos0=pos0, n_sel=n_sel),
        grid=(B, T // tq),
        in_specs=[qblk(Q_W), qblk(QI_W), qblk(N_IDX_HEADS), kblk(KV_PAD_W),
                  pl.BlockSpec((1,) + vt.shape[1:], lambda b, t: (b, 0, 0, 0)), kblk(LANES)],
        out_specs=pl.BlockSpec((1, tq, Q_W), lambda b, t: (b, t, 0)),
        out_shape=jax.ShapeDtypeStruct((B, T, Q_W), jnp.bfloat16),
        scratch_shapes=[pltpu.VMEM((n_keys, tq), jnp.int32)] + [pltpu.VMEM((n_keys, tq), jnp.int16)] * 3 + [
                        pltpu.VMEM((n_keys, tq), jnp.float32),
                        pltpu.VMEM((n_keys, KV_GROUP * tq), jnp.float32),
                        pltpu.VMEM((n_keys, KV_GROUP * tq), jnp.float32),
                        pltpu.VMEM((LANES, KV_GROUP * tq), jnp.float32)],
        compiler_params=pltpu.CompilerParams(
            dimension_semantics=("arbitrary", "arbitrary"), vmem_limit_bytes=VMEM_LIMIT),
        name="dsa_attention",
    )(qt, qit, wit, kp, vt, kib)


def _pad_heads(a):
    lead = a.shape[:-1]
    a = a.reshape(*lead, N_KV_HEADS, HEAD_DIM)
    a = jnp.pad(a, [(0, 0)] * len(lead) + [(0, 0), (0, LANES - HEAD_DIM)])
    return a.reshape(*lead, KV_PAD_W)


def _pack_w_in(w_in):
    o = 0
    wq = w_in[:, o:o + Q_W] * (HEAD_DIM ** -0.5); o += Q_W
    wk = w_in[:, o:o + KV_W]; o += KV_W
    wv = w_in[:, o:o + KV_W]; o += KV_W
    wqi = w_in[:, o:o + QI_W] * (IDX_DIM ** -0.5); o += QI_W
    wki = w_in[:, o:o + IDX_DIM]; o += IDX_DIM
    wwi = w_in[:, o:o + N_IDX_HEADS]
    wkw = jnp.pad(jnp.concatenate([wki, wwi], axis=1), ((0, 0), (0, LANES - IDX_DIM - N_IDX_HEADS)))
    rowmajor = jnp.concatenate([wk, wv, _pad_heads(wk), wkw], axis=1)
    wv_slots = _pad_heads(wv)
    wv_slots = wv_slots.at[:, ROW_WI_IN_V:ROW_WI_IN_V + N_IDX_HEADS].set(wwi)
    transposed = jnp.concatenate([wq, wqi, wv_slots], axis=1).T
    return rowmajor.astype(jnp.bfloat16), transposed.astype(jnp.bfloat16)


def _rope_tables(pos):
    inv_freq = 1.0 / (ROPE_THETA ** (jnp.arange(0, HEAD_DIM, 2, dtype=jnp.float32) / HEAD_DIM))
    ang = pos.astype(jnp.float32)[:, None] * inv_freq[None, :]
    c, s = jnp.cos(ang), jnp.sin(ang)
    z = jnp.zeros_like(c)
    n = pos.shape[0]
    zpad = jnp.zeros((n, LANES - HEAD_DIM), jnp.float32)
    wi_scale = jnp.full((n, N_IDX_HEADS), N_IDX_HEADS ** -0.5, jnp.float32)
    c_kw = jnp.concatenate([c, c, wi_scale, jnp.zeros((n, LANES - HEAD_DIM - N_IDX_HEADS), jnp.float32)], axis=1)
    tab = jnp.concatenate([
        jnp.concatenate([c, c, c, c], axis=1), c_kw,
        jnp.concatenate([-s, z, -s, z], axis=1), jnp.concatenate([-s, z, zpad], axis=1),
        jnp.concatenate([z, s, z, s], axis=1), jnp.concatenate([z, s, zpad], axis=1),
    ], axis=1)
    tabt = jnp.concatenate([c, s], axis=1).T
    return tab, tabt


def _round_up(n, m):
    return (n + m - 1) // m * m


def _value_slots_t(v):
    B, n, _ = v.shape
    vt = jnp.transpose(v.reshape(B, n, N_KV_HEADS, HEAD_DIM), (0, 2, 3, 1))
    ones = jnp.ones((B, N_KV_HEADS, 1, n), v.dtype)
    zeros = jnp.zeros((B, N_KV_HEADS, LANES - HEAD_DIM - 1, n), v.dtype)
    return jnp.concatenate([vt, ones, zeros], axis=2).reshape(B, KV_PAD_W, n)


def _trunk(x, pos0, pool_past, attn_past, norm_mix, norm_mlp, norm_final, pool_w, pool_scale,
           w_rowmajor, w_transposed, w_o, w_up, w_down):
    B, T, D = x.shape
    bf = jnp.bfloat16
    row = lambda v: v.reshape(1, D)

    if pool_past is None:
        past16 = jnp.zeros((B, POOL_STATE + 1, D), jnp.float32)
    else:
        past16 = jnp.pad(pool_past, ((0, 0), (1, 0), (0, 0)))
    tt = min(T, 512)
    x1, st = _pool_layer(x, past16, row(norm_mix[0]), pool_w[0].astype(bf), row(pool_scale[0]), pos0, tt)
    pool_new = st[:, 1:][None]
    tm = min(B * T, 512)
    x2 = _mlp_layer(x1.reshape(B * T, D), row(norm_mlp[0]), w_up[0], w_down[0], tm, 512).reshape(B, T, D)

    tab, tabt = _rope_tables(pos0 + jnp.arange(T, dtype=jnp.int32))
    kc, vc, kp, kw, kib, qt, qit, wit, vt = _proj_layer(
        x2, row(norm_mix[1]), w_rowmajor, w_transposed, tab, tabt, min(T, KEY_BLOCK))
    k_new = kc.reshape(1, B, T, N_KV_HEADS, HEAD_DIM)
    v_new = vc.reshape(1, B, T, N_KV_HEADS, HEAD_DIM)
    ki_new = kw[:, :, :IDX_DIM][None]

    if attn_past is None:
        n_real = T
        kp_all, vt_all, kib_all = kp, vt, kib
    else:
        ck, cv, cki = attn_past
        P = ck.shape[1]
        n_real = P + T
        n_keys = _round_up(n_real, KEY_BLOCK)
        padk = lambda a: jnp.pad(a, ((0, 0), (0, n_keys - n_real), (0, 0)))
        kp_all = padk(jnp.concatenate([_pad_heads(ck.reshape(B, P, KV_W)).astype(bf), kp], axis=1))
        kib_all = padk(jnp.concatenate(
            [jnp.pad(cki, ((0, 0), (0, 0), (0, LANES - IDX_DIM))).astype(bf), kib], axis=1))
        v_all = jnp.concatenate([cv.reshape(B, P, KV_W), vc], axis=1)
        vt_all = jnp.pad(_value_slots_t(v_all).astype(bf), ((0, 0), (0, 0), (0, n_keys - n_real)))
        vt_all = jnp.transpose(vt_all.reshape(B, KV_PAD_W, n_keys // KEY_BLOCK, KEY_BLOCK), (0, 2, 1, 3))
    tq = min(Q_TILE, _round_up(T, LANES))
    t_pad = _round_up(T, tq)
    padq = lambda a: jnp.pad(a, ((0, 0), (0, 0), (0, t_pad - T)))
    n_sel = min(TOPK_MAX, n_real // 4)
    o = _attn_layer(padq(qt), padq(qit), padq(wit), kp_all, vt_all, kib_all,
                    tq=tq, tq_real=min(T, tq), n_keys_real=n_real, pos0=pos0, n_sel=n_sel)
    o = o[:, :T].reshape(B * T, Q_W)

    y = _mlp_layer(x2.reshape(B * T, D), row(norm_mlp[1]), w_up[1], w_down[1], tm, 512,
                   attn=(o, w_o), final_g=row(norm_final)).reshape(B, T, D)
    return y, pool_new, k_new, v_new, ki_new


def kernel(x_prompt, x_sample, state_pool, cache_k, cache_v, cache_kidx, norm_mix, norm_mlp, norm_final,
           pool_w, pool_scale, attn_w_in, attn_w_o, mlp_w_up, mlp_w_down):
    bf = jnp.bfloat16
    shared = (norm_mix, norm_mlp, norm_final, pool_w, pool_scale, *_pack_w_in(attn_w_in[0]),
              attn_w_o[0].astype(bf), mlp_w_up.astype(bf), mlp_w_down.astype(bf))
    y_p, pool_p, k_p, v_p, ki_p = _trunk(x_prompt, 0, None, None, *shared)
    y_s, pool_s, k_s, v_s, ki_s = _trunk(
        x_sample, cache_k.shape[2], state_pool[0], (cache_k[0], cache_v[0], cache_kidx[0]), *shared)
    return (y_p, y_s, pool_p, pool_s, k_p, v_p, ki_p, k_s, v_s, ki_s)
```

```python
import functools
import math
import struct

import jax
import jax.numpy as jnp
from jax import lax
from jax.experimental import pallas as pl
from jax.experimental.pallas import tpu as pltpu

D_MODEL = 1024
CHUNK = 64
POOL_WINDOWS = (2, 4, 8, 16)
POOL_GROUP = D_MODEL // len(POOL_WINDOWS)
POOL_STATE = max(POOL_WINDOWS) - 1
N_HEADS = 16
HEAD_DIM = 64
N_KV_HEADS = 4
KV_GROUP = N_HEADS // N_KV_HEADS
N_IDX_HEADS = 8
IDX_DIM = 64
TOPK_MAX = 256
ROPE_THETA = 10000.0
D_FF = 4 * D_MODEL
EPS = 1e-6
NEG = -1e30
Q_W = N_HEADS * HEAD_DIM
KV_W = N_KV_HEADS * HEAD_DIM
QI_W = N_IDX_HEADS * IDX_DIM
LOG2E = math.log2(math.e)

SUBLANES = 8
PACKED_ROWS = 2 * SUBLANES
LANES = 128
KV_PAD_W = N_KV_HEADS * LANES
KEY_BLOCK = 256
Q_TILE = 256
ROW_TILE = 512
FF_CHUNK = 512

OFF_KC = 0
OFF_VC = OFF_KC + KV_W
OFF_KW = OFF_VC + KV_W
PROJ_ROWMAJOR_W = OFF_KW + LANES
ROW_Q = 0
ROW_QI = ROW_Q + Q_W
ROW_V = ROW_QI + QI_W
ROW_WI = ROW_V + KV_W
PROJ_TRANSPOSED_ROWS = ROW_WI + PACKED_ROWS

VMEM_LIMIT = 56 * 1024 * 1024

_NT = (((1,), (1,)), ((), ()))


def _f32_order_key(x):
    b = struct.unpack("<i", struct.pack("<f", x))[0]
    return b ^ ((b >> 31) & 0x7FFFFFFF)


def _signed16(v):
    return v - (1 << 16) if v >= (1 << 15) else v


NEG_KEY = _f32_order_key(NEG)
NEG_KEY_HI = NEG_KEY >> 16
NEG_KEY_LO = _signed16((NEG_KEY & 0xFFFF) ^ 0x8000)


def _rms(x, g):
    return x * lax.rsqrt(jnp.mean(x * x, axis=-1, keepdims=True) + EPS) * g


def _const_spec(shape):
    n = len(shape)
    return pl.BlockSpec(shape, lambda *_: (0,) * n, pipeline_mode=pl.Buffered(1))


def _tree(op, parts):
    parts = list(parts)
    while len(parts) > 1:
        parts = [op(parts[i], parts[i + 1]) if i + 1 < len(parts) else parts[i] for i in range(0, len(parts), 2)]
    return parts[0]


def _fold_rows(op, x, rows=SUBLANES):
    return _tree(op, [x[r:r + rows] for r in range(0, x.shape[0], rows)])


def _pool_kernel(x_ref, past_ref, g_ref, w_ref, scale_ref, x1_ref, st_ref, hp_ref, sa_ref, sb_ref, *, tt, pos0):
    t = pl.program_id(1)
    halo = POOL_STATE + 1
    lead = SUBLANES
    top = lead + halo

    @pl.when(t == 0)
    def _():
        zeros = jnp.zeros((lead, hp_ref.shape[1]), jnp.float32)
        hp_ref[0:lead, :] = zeros
        sa_ref[0:lead, :] = zeros
        sb_ref[0:lead, :] = zeros
        hp_ref[lead:top, :] = past_ref[0]

    x = x_ref[0]
    h = _rms(x, g_ref[...])
    hp_ref[top:top + tt, :] = h

    n = halo + tt
    g1, g2, g3 = POOL_GROUP, 2 * POOL_GROUP, 3 * POOL_GROUP
    sa_ref[lead:lead + n, :] = hp_ref[lead:lead + n, :] + hp_ref[lead - 1:lead - 1 + n, :]
    sb_ref[lead:lead + n, g1:] = sa_ref[lead:lead + n, g1:] + sa_ref[lead - 2:lead - 2 + n, g1:]
    sa_ref[lead:lead + n, g2:] = sb_ref[lead:lead + n, g2:] + sb_ref[lead - 4:lead - 4 + n, g2:]
    window_sums = (
        sa_ref[top:top + tt, 0:g1],
        sb_ref[top:top + tt, g1:g2],
        sa_ref[top:top + tt, g2:g3],
        sa_ref[top:top + tt, g3:] + sa_ref[top - 8:top - 8 + tt, g3:],
    )
    pos = pos0 + t * tt + lax.broadcasted_iota(jnp.int32, (tt, 1), 0)
    for g, win in enumerate(POOL_WINDOWS):
        c0, c1 = g * POOL_GROUP, (g + 1) * POOL_GROUP
        cnt = jnp.minimum(pos + 1, win).astype(jnp.float32)
        diff = window_sums[g] / cnt - h[:, c0:c1]
        y = jnp.dot(diff.astype(jnp.bfloat16), w_ref[g], preferred_element_type=jnp.float32)
        x1_ref[0, :, c0:c1] = x[:, c0:c1] + y * scale_ref[:, c0:c1]
    tail = hp_ref[lead + tt:top + tt, :]
    st_ref[0] = tail
    hp_ref[lead:top, :] = tail


def _pool_layer(x, past16, g, w, scale, pos0, tt):
    B, T, D = x.shape
    halo = POOL_STATE + 1
    plane = pltpu.VMEM((SUBLANES + halo + tt, D), jnp.float32)
    return pl.pallas_call(
        functools.partial(_pool_kernel, tt=tt, pos0=pos0),
        grid=(B, T // tt),
        in_specs=[
            pl.BlockSpec((1, tt, D), lambda b, t: (b, t, 0)),
            pl.BlockSpec((1, halo, D), lambda b, t: (b, 0, 0)),
            _const_spec((1, D)),
            _const_spec(w.shape),
            _const_spec((1, D)),
        ],
        out_specs=[
            pl.BlockSpec((1, tt, D), lambda b, t: (b, t, 0)),
            pl.BlockSpec((1, halo, D), lambda b, t: (b, 0, 0)),
        ],
        out_shape=[
            jax.ShapeDtypeStruct((B, T, D), jnp.float32),
            jax.ShapeDtypeStruct((B, halo, D), jnp.float32),
        ],
        scratch_shapes=[plane, plane, plane],
        compiler_params=pltpu.CompilerParams(
            dimension_semantics=("arbitrary", "arbitrary"), vmem_limit_bytes=VMEM_LIMIT),
        name="pool_mixer",
    )(x, past16, g, w, scale)


def _mlp_body(x, g_ref, up_ref, down_ref, fc):
    h = _rms(x, g_ref[...]).astype(jnp.bfloat16)
    acc = x
    for c in range(D_FF // fc):
        u = jnp.dot(h, up_ref[:, c * fc:(c + 1) * fc], preferred_element_type=jnp.float32)
        u = jnp.maximum(u, 0.0)
        a = (u * u).astype(jnp.bfloat16)
        acc = acc + jnp.dot(a, down_ref[c * fc:(c + 1) * fc, :], preferred_element_type=jnp.float32)
    return acc


def _mlp_kernel(*refs, fc, with_attn, with_final):
    refs = list(refs)
    x_ref = refs.pop(0)
    if with_attn:
        o_ref = refs.pop(0)
        wo_ref = refs.pop(0)
    g_ref = refs.pop(0)
    up_ref = refs.pop(0)
    down_ref = refs.pop(0)
    if with_final:
        gf_ref = refs.pop(0)
    out_ref = refs.pop(0)

    x = x_ref[...]
    if with_attn:
        x = x + jnp.dot(o_ref[...], wo_ref[...], preferred_element_type=jnp.float32)
    acc = _mlp_body(x, g_ref, up_ref, down_ref, fc)
    if with_final:
        acc = _rms(acc, gf_ref[...])
    out_ref[...] = acc


def _mlp_layer(x, g, up, down, tm, fc, attn=None, final_g=None):
    N, D = x.shape
    row = lambda i: (i, 0)
    args, specs = [x], [pl.BlockSpec((tm, D), row)]
    if attn is not None:
        o, wo = attn
        args += [o, wo]
        specs += [pl.BlockSpec((tm, Q_W), row), _const_spec(wo.shape)]
    args += [g, up, down]
    specs += [_const_spec((1, D)), _const_spec(up.shape), _const_spec(down.shape)]
    if final_g is not None:
        args.append(final_g)
        specs.append(_const_spec((1, D)))
    return pl.pallas_call(
        functools.partial(_mlp_kernel, fc=fc, with_attn=attn is not None, with_final=final_g is not None),
        grid=(N // tm,),
        in_specs=specs,
        out_specs=pl.BlockSpec((tm, D), row),
        out_shape=jax.ShapeDtypeStruct((N, D), jnp.float32),
        compiler_params=pltpu.CompilerParams(
            dimension_semantics=("arbitrary",), vmem_limit_bytes=VMEM_LIMIT),
        name="mlp_attn_out" if attn is not None else "mlp",
    )(*args)


def _rope_block(xb, tab_ref, special):
    o = LANES if special else 0
    c = tab_ref[:, o:o + LANES]
    s1 = tab_ref[:, 2 * LANES + o:3 * LANES + o]
    s2 = tab_ref[:, 4 * LANES + o:5 * LANES + o]
    return xb * c + pltpu.roll(xb, LANES - HEAD_DIM // 2, 1) * s1 + pltpu.roll(xb, HEAD_DIM // 2, 1) * s2


def _rope_rows(p, c, s):
    half = HEAD_DIM // 2
    r = p.reshape(p.shape[0] // HEAD_DIM, 2, half, p.shape[1])
    x1, x2 = r[:, 0], r[:, 1]
    out = jnp.stack([x1 * c - x2 * s, x2 * c + x1 * s], axis=1)
    return out.reshape(p.shape)


def _proj_kernel(x_ref, g_ref, w_ref, wt_ref, tab_ref, tabt_ref,
                 kc_ref, vc_ref, kp_ref, kw_ref, kib_ref, qt_ref, qit_ref, wit_ref, vt_ref):
    h = _rms(x_ref[0], g_ref[...]).astype(jnp.bfloat16)

    def proj(off, width):
        return jnp.dot(h, w_ref[:, off:off + width], preferred_element_type=jnp.float32)

    def rope(p, special=False):
        return jnp.concatenate(
            [_rope_block(p[:, j:j + LANES], tab_ref, special) for j in range(0, p.shape[1], LANES)], axis=-1)

    kc = rope(proj(OFF_KC, KV_W))
    kc_ref[0] = kc
    vc_ref[0] = proj(OFF_VC, KV_W)
    gap = jnp.zeros((kc.shape[0], LANES - HEAD_DIM), jnp.float32)
    kp = jnp.concatenate(
        [piece for g in range(N_KV_HEADS) for piece in (kc[:, g * HEAD_DIM:(g + 1) * HEAD_DIM], gap)], axis=1)
    kp_ref[0] = kp.astype(kp_ref.dtype)
    kw = rope(proj(OFF_KW, LANES), special=True)
    kw_ref[0] = kw
    kib_ref[0] = kw.astype(kib_ref.dtype)

    def proj_t(row, n):
        return lax.dot_general(wt_ref[row:row + n, :], h, _NT, preferred_element_type=jnp.float32)

    half = HEAD_DIM // 2
    c, s = tabt_ref[0:half, :], tabt_ref[half:HEAD_DIM, :]
    qt_ref[0] = _rope_rows(proj_t(ROW_Q, Q_W), c * LOG2E, s * LOG2E).astype(qt_ref.dtype)
    qit_ref[0] = _rope_rows(proj_t(ROW_QI, QI_W), c, s).astype(qit_ref.dtype)
    vw = proj_t(ROW_V, KV_W + PACKED_ROWS)
    wit_ref[0] = vw[KV_W:KV_W + N_IDX_HEADS, :] * (N_IDX_HEADS ** -0.5)
    n = vw.shape[1]
    tail = jnp.where(lax.broadcasted_iota(jnp.int32, (LANES - HEAD_DIM, n), 0) == 0, 1.0, 0.0)
    vt = jnp.concatenate(
        [piece for g in range(N_KV_HEADS) for piece in (vw[g * HEAD_DIM:(g + 1) * HEAD_DIM], tail)], axis=0)
    vt_ref[0, 0] = vt.astype(vt_ref.dtype)


def _proj_layer(x, g, w_rowmajor, w_transposed, tab, tabt, tm):
    B, T, D = x.shape
    blk = lambda w: pl.BlockSpec((1, tm, w), lambda b, t: (b, t, 0))
    blk_t = lambda r: pl.BlockSpec((1, r, tm), lambda b, t: (b, 0, t))
    bf, f32 = jnp.bfloat16, jnp.float32
    sds = jax.ShapeDtypeStruct
    return pl.pallas_call(
        _proj_kernel,
        grid=(B, T // tm),
        in_specs=[
            blk(D),
            _const_spec((1, D)),
            _const_spec(w_rowmajor.shape),
            _const_spec(w_transposed.shape),
            pl.BlockSpec((tm, tab.shape[1]), lambda b, t: (t, 0)),
            pl.BlockSpec((HEAD_DIM, tm), lambda b, t: (0, t)),
        ],
        out_specs=[blk(KV_W), blk(KV_W), blk(KV_PAD_W), blk(LANES), blk(LANES),
                   blk_t(Q_W), blk_t(QI_W), blk_t(N_IDX_HEADS),
                   pl.BlockSpec((1, 1, KV_PAD_W, tm), lambda b, t: (b, t, 0, 0))],
        out_shape=[sds((B, T, KV_W), f32), sds((B, T, KV_W), f32), sds((B, T, KV_PAD_W), bf),
                   sds((B, T, LANES), f32), sds((B, T, LANES), bf),
                   sds((B, Q_W, T), bf), sds((B, QI_W, T), bf), sds((B, N_IDX_HEADS, T), f32),
                   sds((B, T // tm, KV_PAD_W, tm), bf)],
        compiler_params=pltpu.CompilerParams(
            dimension_semantics=("arbitrary", "arbitrary"), vmem_limit_bytes=VMEM_LIMIT),
        name="attn_in_proj",
    )(x, g, w_rowmajor, w_transposed, tab, tabt)


def _attn_kernel(qt_ref, qit_ref, wit_ref, kp_ref, vt_ref, kib_ref, o_ref,
                 key_ref, hi_ref, lo_ref, lo2_ref, f_ref, s0_ref, s1_ref, oacc_ref, *,
                 tq, tq_real, n_keys_real, pos0, n_sel):
    kb = KEY_BLOCK
    t = pl.program_id(1)
    lane_q = lax.broadcasted_iota(jnp.int32, (1, tq), 1)
    q_limit = ((pos0 + t * tq + lane_q) // CHUNK + 1) * CHUNK
    qvalid = lane_q < tq_real
    n_reach = jnp.minimum(((pos0 + (t + 1) * tq - 1) // CHUNK + 1) * CHUNK, n_keys_real)
    nkb = (n_reach + kb - 1) // kb
    n_unvisited = jnp.maximum(n_keys_real - nkb * kb, 0)
    key_row = lax.broadcasted_iota(jnp.int32, (kb, tq), 0)

    def block_start(j):
        return pl.multiple_of(j * kb, kb)

    def for_blocks(body, carry):
        def two(i, c):
            return body(2 * i + 1, body(2 * i, c))
        carry = lax.fori_loop(0, nkb // 2, two, carry)
        return lax.fori_loop(nkb // 2 * 2, nkb, body, carry)

    def admissible(k0):
        return key_row < q_limit - k0

    wi = wit_ref[0]

    def score_block(j, carry):
        k0 = block_start(j)
        kx = kib_ref[0, pl.ds(k0, kb), 0:IDX_DIM]
        acc = jnp.zeros((kb, tq), jnp.float32)
        for hp in range(N_IDX_HEADS // 2):
            qi2 = jnp.concatenate(
                [qit_ref[0, (2 * hp + u) * IDX_DIM:(2 * hp + u + 1) * IDX_DIM, :] for u in range(2)], axis=1)
            lg = jnp.dot(kx, qi2, preferred_element_type=jnp.float32)
            for u in range(2):
                hh = 2 * hp + u
                acc = acc + jnp.maximum(lg[:, u * tq:(u + 1) * tq], 0.0) * wi[hh:hh + 1, :]
        sc = jnp.where(admissible(k0), acc, NEG)
        if n_keys_real % kb:
            sc = jnp.where(key_row < n_keys_real - k0, sc, -jnp.inf)
        bits = lax.bitcast_convert_type(sc, jnp.int32)
        key = bits ^ ((bits >> 31) & jnp.int32(0x7FFFFFFF))
        key_ref[pl.ds(k0, kb), :] = key
        hi_ref[pl.ds(k0, kb), :] = (key >> 16).astype(jnp.int16)
        lo_ref[pl.ds(k0, kb), :] = (key ^ 0x8000).astype(jnp.int16)
        return carry

    for_blocks(score_block, 0)

    def count(pred):
        def body(j, acc):
            blk = key_ref[pl.ds(block_start(j), kb), :]
            return acc + _fold_rows(jnp.add, jnp.where(pred(blk, j), 1, 0).astype(jnp.int32))
        acc = for_blocks(body, jnp.zeros((SUBLANES, tq), jnp.int32))
        return jnp.sum(acc, axis=0, keepdims=True)

    def count16(ref, pred):
        def body(j, acc):
            hit = jnp.where(pred(ref[pl.ds(block_start(j), kb), :]), jnp.int16(1), jnp.int16(0))
            return acc + _fold_rows(jnp.add, hit, rows=PACKED_ROWS)
        acc = for_blocks(body, jnp.zeros((PACKED_ROWS, tq), jnp.int16))
        return jnp.sum(acc.astype(jnp.int32), axis=0, keepdims=True)

    def as_plane(v):
        tile = jnp.broadcast_to(v, (PACKED_ROWS, tq)).astype(jnp.int16)
        return jnp.concatenate([tile] * (kb // PACKED_ROWS), axis=0)

    def radix16(ref, n_wanted, n_all, unvisited_ge):
        def step(i, carry):
            thr, n_at = carry
            cand = thr + jnp.left_shift(jnp.int32(1), 15 - i)
            plane = as_plane(cand)
            n_ge = count16(ref, lambda blk: blk >= plane) + unvisited_ge(cand)
            take = n_ge >= n_wanted
            return jnp.where(take, cand, thr), jnp.where(take, n_ge, n_at)
        return lax.fori_loop(0, 16, step, (jnp.full((1, tq), -2 ** 15, jnp.int32), n_all))

    n_all = jnp.broadcast_to(nkb * kb + n_unvisited, (1, tq))
    thr_hi, n_ge_hi = radix16(hi_ref, n_sel, n_all, lambda cand: jnp.where(cand <= NEG_KEY_HI, n_unvisited, 0))
    hi_plane = as_plane(thr_hi)
    n_gt_hi = count16(hi_ref, lambda blk: blk > hi_plane) + jnp.where(thr_hi < NEG_KEY_HI, n_unvisited, 0)

    def park_block(j, carry):
        rows = pl.ds(block_start(j), kb)
        lo2_ref[rows, :] = jnp.where(hi_ref[rows, :] == hi_plane, lo_ref[rows, :], jnp.int16(-2 ** 15))
        return carry

    for_blocks(park_block, 0)
    neg_in_bucket = thr_hi == NEG_KEY_HI
    thr_lo, n_ge_lo = radix16(lo2_ref, n_sel - n_gt_hi, n_ge_hi - n_gt_hi,
                              lambda cand: jnp.where(neg_in_bucket & (cand <= NEG_KEY_LO), n_unvisited, 0))
    thr = thr_hi * 65536 + (thr_lo + 2 ** 15)
    lo_plane = as_plane(thr_lo)
    n_gt_lo = (count16(lo2_ref, lambda blk: blk > lo_plane)
               + jnp.where(neg_in_bucket & (thr_lo < NEG_KEY_LO), n_unvisited, 0))
    n_gt = n_gt_hi + n_gt_lo
    n_eq = n_ge_lo - n_gt_lo
    n_eq_adm = count(lambda blk, j: (blk == thr) & admissible(block_start(j)))
    need = n_sel - n_gt
    tie = (n_eq > need) & (n_eq_adm > 0) & qvalid
    any_tie = jnp.max(tie.astype(jnp.int32)) > 0

    def write_mask(exact_ties):
        def body(j, run):
            k0 = block_start(j)
            kblk = key_ref[pl.ds(k0, kb), :]
            if exact_ties:
                r = lax.broadcasted_iota(jnp.int32, (kb, kb), 0)
                c = lax.broadcasted_iota(jnp.int32, (kb, kb), 1)
                tri = jnp.where(r >= c, 1.0, 0.0).astype(jnp.bfloat16)
                eqb = kblk == thr
                eqf = jnp.where(eqb, 1.0, 0.0)
                rank = jnp.dot(tri, eqf.astype(jnp.bfloat16), preferred_element_type=jnp.float32) + run
                run = run + jnp.sum(eqf, axis=0, keepdims=True)
                sel = (kblk > thr) | (eqb & (rank <= need.astype(jnp.float32)))
            else:
                sel = kblk >= thr
            f_ref[pl.ds(k0, kb), :] = jnp.where(sel, jnp.where(admissible(k0), jnp.inf, NEG), -jnp.inf)
            return run

        for_blocks(body, jnp.zeros((1, tq), jnp.float32))

    @pl.when(any_tie)
    def _():
        write_mask(True)

    @pl.when(jnp.logical_not(any_tie))
    def _():
        write_mask(False)

    gw = KV_GROUP * tq
    s_refs = (s0_ref, s1_ref)

    def logits_block(g, j):
        k0 = block_start(j)
        q4 = jnp.concatenate(
            [qt_ref[0, (g * KV_GROUP + r) * HEAD_DIM:(g * KV_GROUP + r + 1) * HEAD_DIM, :] for r in range(KV_GROUP)],
            axis=1)
        kblk = kp_ref[0, pl.ds(k0, kb), g * LANES:g * LANES + HEAD_DIM]
        s = jnp.dot(kblk, q4, preferred_element_type=jnp.float32)
        f = f_ref[pl.ds(k0, kb), :]
        s = jnp.minimum(s, jnp.concatenate([f] * KV_GROUP, axis=1))
        s_refs[g % 2][pl.ds(k0, kb), :] = s
        return _fold_rows(jnp.maximum, s)

    def values_block(g, j, m):
        p = jnp.exp2(s_refs[g % 2][pl.ds(block_start(j), kb), :] - m).astype(jnp.bfloat16)
        oacc_ref[...] += jnp.dot(vt_ref[0, j, g * LANES:(g + 1) * LANES, :], p,
                                 preferred_element_type=jnp.float32)

    m_init = jnp.full((SUBLANES, gw), -jnp.inf, jnp.float32)
    m_acc = for_blocks(lambda j, a: jnp.maximum(a, logits_block(0, j)), m_init)
    for g in range(N_KV_HEADS):
        m = jnp.max(m_acc, axis=0, keepdims=True)
        oacc_ref[...] = jnp.zeros(oacc_ref.shape, jnp.float32)

        def fused(j, a, g=g, m=m):
            if g + 1 < N_KV_HEADS:
                a = jnp.maximum(a, logits_block(g + 1, j))
            values_block(g, j, m)
            return a

        m_acc = for_blocks(fused, m_init)
        oacc = oacc_ref[...]
        on = oacc[0:HEAD_DIM] / oacc[HEAD_DIM:HEAD_DIM + 1]
        for u in range(KV_GROUP // 2):
            pair = jnp.concatenate([on[:, (2 * u) * tq:(2 * u + 1) * tq], on[:, (2 * u + 1) * tq:(2 * u + 2) * tq]],
                                   axis=0)
            c0 = (g * KV_GROUP + 2 * u) * HEAD_DIM
            o_ref[0, :, c0:c0 + 2 * HEAD_DIM] = pair.T.astype(o_ref.dtype)


def _attn_layer(qt, qit, wit, kp, vt, kib, *, tq, tq_real, n_keys_real, pos0, n_sel):
    B, _, T = qt.shape
    n_keys = kp.shape[1]
    qblk = lambda r: pl.BlockSpec((1, r, tq), lambda b, t: (b, 0, t))
    kblk = lambda w: pl.BlockSpec((1, n_keys, w), lambda b, t: (b, 0, 0))
    return pl.pallas_call(
        functools.partial(_attn_kernel, tq=tq, tq_real=tq_real, n_keys_real=n_keys_real, pos0=pos0, n_sel=n_sel),
        grid=(B, T // tq),
        in_specs=[qblk(Q_W), qblk(QI_W), qblk(N_IDX_HEADS), kblk(KV_PAD_W),
                  pl.BlockSpec((1,) + vt.shape[1:], lambda b, t: (b, 0, 0, 0)), kblk(LANES)],
        out_specs=pl.BlockSpec((1, tq, Q_W), lambda b, t: (b, t, 0)),
        out_shape=jax.ShapeDtypeStruct((B, T, Q_W), jnp.bfloat16),
        scratch_shapes=[pltpu.VMEM((n_keys, tq), jnp.int32)] + [pltpu.VMEM((n_keys, tq), jnp.int16)] * 3 + [
                        pltpu.VMEM((n_keys, tq), jnp.float32),
                        pltpu.VMEM((n_keys, KV_GROUP * tq), jnp.float32),
                        pltpu.VMEM((n_keys, KV_GROUP * tq), jnp.float32),
                        pltpu.VMEM((LANES, KV_GROUP * tq), jnp.float32)],
        compiler_params=pltpu.CompilerParams(
            dimension_semantics=("arbitrary", "arbitrary"), vmem_limit_bytes=VMEM_LIMIT),
        name="dsa_attention",
    )(qt, qit, wit, kp, vt, kib)


def _pad_heads(a):
    lead = a.shape[:-1]
    a = a.reshape(*lead, N_KV_HEADS, HEAD_DIM)
    a = jnp.pad(a, [(0, 0)] * len(lead) + [(0, 0), (0, LANES - HEAD_DIM)])
    return a.reshape(*lead, KV_PAD_W)


def _pack_w_in(w_in):
    o = 0
    wq = w_in[:, o:o + Q_W] * (HEAD_DIM ** -0.5); o += Q_W
    wk = w_in[:, o:o + KV_W]; o += KV_W
    wv = w_in[:, o:o + KV_W]; o += KV_W
    wqi = w_in[:, o:o + QI_W] * (IDX_DIM ** -0.5); o += QI_W
    wki = w_in[:, o:o + IDX_DIM]; o += IDX_DIM
    wwi = w_in[:, o:o + N_IDX_HEADS]
    wkw = jnp.pad(jnp.concatenate([wki, wwi], axis=1), ((0, 0), (0, LANES - IDX_DIM - N_IDX_HEADS)))
    rowmajor = jnp.concatenate([wk, wv, wkw], axis=1)
    wwi_rows = jnp.pad(wwi, ((0, 0), (0, PACKED_ROWS - N_IDX_HEADS)))
    transposed = jnp.concatenate([wq, wqi, wv, wwi_rows], axis=1).T
    return rowmajor.astype(jnp.bfloat16), transposed.astype(jnp.bfloat16)


def _rope_tables(pos):
    inv_freq = 1.0 / (ROPE_THETA ** (jnp.arange(0, HEAD_DIM, 2, dtype=jnp.float32) / HEAD_DIM))
    ang = pos.astype(jnp.float32)[:, None] * inv_freq[None, :]
    c, s = jnp.cos(ang), jnp.sin(ang)
    z = jnp.zeros_like(c)
    n = pos.shape[0]
    zpad = jnp.zeros((n, LANES - HEAD_DIM), jnp.float32)
    wi_scale = jnp.full((n, N_IDX_HEADS), N_IDX_HEADS ** -0.5, jnp.float32)
    c_kw = jnp.concatenate([c, c, wi_scale, jnp.zeros((n, LANES - HEAD_DIM - N_IDX_HEADS), jnp.float32)], axis=1)
    tab = jnp.concatenate([
        jnp.concatenate([c, c, c, c], axis=1), c_kw,
        jnp.concatenate([-s, z, -s, z], axis=1), jnp.concatenate([-s, z, zpad], axis=1),
        jnp.concatenate([z, s, z, s], axis=1), jnp.concatenate([z, s, zpad], axis=1),
    ], axis=1)
    tabt = jnp.concatenate([c, s], axis=1).T
    return tab, tabt


def _round_up(n, m):
    return (n + m - 1) // m * m


def _value_slots_t(v):
    B, n, _ = v.shape
    vt = jnp.transpose(v.reshape(B, n, N_KV_HEADS, HEAD_DIM), (0, 2, 3, 1))
    ones = jnp.ones((B, N_KV_HEADS, 1, n), v.dtype)
    zeros = jnp.zeros((B, N_KV_HEADS, LANES - HEAD_DIM - 1, n), v.dtype)
    return jnp.concatenate([vt, ones, zeros], axis=2).reshape(B, KV_PAD_W, n)


def _trunk(x, pos0, pool_past, attn_past, norm_mix, norm_mlp, norm_final, pool_w, pool_scale,
           w_rowmajor, w_transposed, w_o, w_up, w_down):
    B, T, D = x.shape
    bf = jnp.bfloat16
    row = lambda v: v.reshape(1, D)

    if pool_past is None:
        past16 = jnp.zeros((B, POOL_STATE + 1, D), jnp.float32)
    else:
        past16 = jnp.pad(pool_past, ((0, 0), (1, 0), (0, 0)))
    tt = min(T, ROW_TILE)
    tm = min(B * T, ROW_TILE)
    x1, st = _pool_layer(x, past16, row(norm_mix[0]), pool_w[0].astype(bf), row(pool_scale[0]), pos0, tt)
    x2 = _mlp_layer(x1.reshape(B * T, D), row(norm_mlp[0]), w_up[0], w_down[0], tm, FF_CHUNK).reshape(B, T, D)
    pool_new = st[:, 1:][None]

    tab, tabt = _rope_tables(pos0 + jnp.arange(T, dtype=jnp.int32))
    kc, vc, kp, kw, kib, qt, qit, wit, vt = _proj_layer(
        x2, row(norm_mix[1]), w_rowmajor, w_transposed, tab, tabt, min(T, KEY_BLOCK))
    k_new = kc.reshape(1, B, T, N_KV_HEADS, HEAD_DIM)
    v_new = vc.reshape(1, B, T, N_KV_HEADS, HEAD_DIM)
    ki_new = kw[:, :, :IDX_DIM][None]

    if attn_past is None:
        n_real = T
        kp_all, vt_all, kib_all = kp, vt, kib
    else:
        ck, cv, cki = attn_past
        P = ck.shape[1]
        n_real = P + T
        n_keys = _round_up(n_real, KEY_BLOCK)
        padk = lambda a: jnp.pad(a, ((0, 0), (0, n_keys - n_real), (0, 0)))
        kp_all = padk(jnp.concatenate([_pad_heads(ck.reshape(B, P, KV_W)).astype(bf), kp], axis=1))
        kib_all = padk(jnp.concatenate(
            [jnp.pad(cki, ((0, 0), (0, 0), (0, LANES - IDX_DIM))).astype(bf), kib], axis=1))
        v_all = jnp.concatenate([cv.reshape(B, P, KV_W), vc], axis=1)
        vt_all = jnp.pad(_value_slots_t(v_all).astype(bf), ((0, 0), (0, 0), (0, n_keys - n_real)))
        vt_all = jnp.transpose(vt_all.reshape(B, KV_PAD_W, n_keys // KEY_BLOCK, KEY_BLOCK), (0, 2, 1, 3))
    tq = min(Q_TILE, _round_up(T, LANES))
    t_pad = _round_up(T, tq)
    padq = lambda a: jnp.pad(a, ((0, 0), (0, 0), (0, t_pad - T)))
    n_sel = min(TOPK_MAX, n_real // 4)
    o = _attn_layer(padq(qt), padq(qit), padq(wit), kp_all, vt_all, kib_all,
                    tq=tq, tq_real=min(T, tq), n_keys_real=n_real, pos0=pos0, n_sel=n_sel)
    o = o[:, :T].reshape(B * T, Q_W)

    y = _mlp_layer(x2.reshape(B * T, D), row(norm_mlp[1]), w_up[1], w_down[1], tm, FF_CHUNK,
                   attn=(o, w_o), final_g=row(norm_final)).reshape(B, T, D)
    return y, pool_new, k_new, v_new, ki_new


def kernel(x_prompt, x_sample, state_pool, cache_k, cache_v, cache_kidx, norm_mix, norm_mlp, norm_final,
           pool_w, pool_scale, attn_w_in, attn_w_o, mlp_w_up, mlp_w_down):
    bf = jnp.bfloat16
    shared = (norm_mix, norm_mlp, norm_final, pool_w, pool_scale, *_pack_w_in(attn_w_in[0]),
              attn_w_o[0].astype(bf), mlp_w_up.astype(bf), mlp_w_down.astype(bf))
    y_p, pool_p, k_p, v_p, ki_p = _trunk(x_prompt, 0, None, None, *shared)
    y_s, pool_s, k_s, v_s, ki_s = _trunk(
        x_sample, cache_k.shape[2], state_pool[0], (cache_k[0], cache_v[0], cache_kidx[0]), *shared)
    return (y_p, y_s, pool_p, pool_s, k_p, v_p, ki_p, k_s, v_s, ki_s)
```

```python
import functools
import math
import struct

import jax
import jax.numpy as jnp
from jax import lax
from jax.experimental import pallas as pl
from jax.experimental.pallas import tpu as pltpu

D_MODEL = 1024
CHUNK = 64
POOL_WINDOWS = (2, 4, 8, 16)
POOL_GROUP = D_MODEL // len(POOL_WINDOWS)
POOL_STATE = max(POOL_WINDOWS) - 1
N_HEADS = 16
HEAD_DIM = 64
N_KV_HEADS = 4
KV_GROUP = N_HEADS // N_KV_HEADS
N_IDX_HEADS = 8
IDX_DIM = 64
TOPK_MAX = 256
ROPE_THETA = 10000.0
D_FF = 4 * D_MODEL
EPS = 1e-6
NEG = -1e30
Q_W = N_HEADS * HEAD_DIM
KV_W = N_KV_HEADS * HEAD_DIM
QI_W = N_IDX_HEADS * IDX_DIM
LOG2E = math.log2(math.e)

SUBLANES = 8
PACKED_ROWS = 2 * SUBLANES
LANES = 128
KV_PAD_W = N_KV_HEADS * LANES
KEY_BLOCK = 256
Q_TILE = 256
ROW_TILE = 1024
PROJ_TILE = 512
FF_CHUNK = 512

OFF_KC = 0
OFF_VC = OFF_KC + KV_W
OFF_KW = OFF_VC + KV_W
PROJ_ROWMAJOR_W = OFF_KW + LANES
ROW_Q = 0
ROW_QI = ROW_Q + Q_W
ROW_V = ROW_QI + QI_W
ROW_WI = ROW_V + KV_W
PROJ_TRANSPOSED_ROWS = ROW_WI + PACKED_ROWS

VMEM_LIMIT = 56 * 1024 * 1024

_NT = (((1,), (1,)), ((), ()))


def _f32_order_key(x):
    b = struct.unpack("<i", struct.pack("<f", x))[0]
    return b ^ ((b >> 31) & 0x7FFFFFFF)


def _signed16(v):
    return v - (1 << 16) if v >= (1 << 15) else v


NEG_KEY = _f32_order_key(NEG)
NEG_KEY_HI = NEG_KEY >> 16
NEG_KEY_LO = _signed16((NEG_KEY & 0xFFFF) ^ 0x8000)


def _rms(x, g):
    return x * lax.rsqrt(jnp.mean(x * x, axis=-1, keepdims=True) + EPS) * g


def _const_spec(shape):
    n = len(shape)
    return pl.BlockSpec(shape, lambda *_: (0,) * n, pipeline_mode=pl.Buffered(1))


def _tree(op, parts):
    parts = list(parts)
    while len(parts) > 1:
        parts = [op(parts[i], parts[i + 1]) if i + 1 < len(parts) else parts[i] for i in range(0, len(parts), 2)]
    return parts[0]


def _fold_rows(op, x, rows=SUBLANES):
    return _tree(op, [x[r:r + rows] for r in range(0, x.shape[0], rows)])


def _pool_kernel(x_ref, past_ref, g_ref, w_ref, scale_ref, x1_ref, st_ref, hp_ref, sa_ref, sb_ref, *, tt, pos0):
    t = pl.program_id(1)
    halo = POOL_STATE + 1
    lead = SUBLANES
    top = lead + halo

    @pl.when(t == 0)
    def _():
        zeros = jnp.zeros((lead, hp_ref.shape[1]), jnp.float32)
        hp_ref[0:lead, :] = zeros
        sa_ref[0:lead, :] = zeros
        sb_ref[0:lead, :] = zeros
        hp_ref[lead:top, :] = past_ref[0]

    x = x_ref[0]
    h = _rms(x, g_ref[...])
    hp_ref[top:top + tt, :] = h

    n = halo + tt
    g1, g2, g3 = POOL_GROUP, 2 * POOL_GROUP, 3 * POOL_GROUP
    sa_ref[lead:lead + n, :] = hp_ref[lead:lead + n, :] + hp_ref[lead - 1:lead - 1 + n, :]
    sb_ref[lead:lead + n, g1:] = sa_ref[lead:lead + n, g1:] + sa_ref[lead - 2:lead - 2 + n, g1:]
    sa_ref[lead:lead + n, g2:] = sb_ref[lead:lead + n, g2:] + sb_ref[lead - 4:lead - 4 + n, g2:]
    window_sums = (
        sa_ref[top:top + tt, 0:g1],
        sb_ref[top:top + tt, g1:g2],
        sa_ref[top:top + tt, g2:g3],
        sa_ref[top:top + tt, g3:] + sa_ref[top - 8:top - 8 + tt, g3:],
    )
    pos = pos0 + t * tt + lax.broadcasted_iota(jnp.int32, (tt, 1), 0)
    for g, win in enumerate(POOL_WINDOWS):
        c0, c1 = g * POOL_GROUP, (g + 1) * POOL_GROUP
        cnt = jnp.minimum(pos + 1, win).astype(jnp.float32)
        diff = window_sums[g] / cnt - h[:, c0:c1]
        y = jnp.dot(diff.astype(jnp.bfloat16), w_ref[g], preferred_element_type=jnp.float32)
        x1_ref[0, :, c0:c1] = x[:, c0:c1] + y * scale_ref[:, c0:c1]
    tail = hp_ref[lead + tt:top + tt, :]
    st_ref[0] = tail
    hp_ref[lead:top, :] = tail


def _pool_layer(x, past16, g, w, scale, pos0, tt):
    B, T, D = x.shape
    halo = POOL_STATE + 1
    plane = pltpu.VMEM((SUBLANES + halo + tt, D), jnp.float32)
    return pl.pallas_call(
        functools.partial(_pool_kernel, tt=tt, pos0=pos0),
        grid=(B, T // tt),
        in_specs=[
            pl.BlockSpec((1, tt, D), lambda b, t: (b, t, 0)),
            pl.BlockSpec((1, halo, D), lambda b, t: (b, 0, 0)),
            _const_spec((1, D)),
            _const_spec(w.shape),
            _const_spec((1, D)),
        ],
        out_specs=[
            pl.BlockSpec((1, tt, D), lambda b, t: (b, t, 0)),
            pl.BlockSpec((1, halo, D), lambda b, t: (b, 0, 0)),
        ],
        out_shape=[
            jax.ShapeDtypeStruct((B, T, D), jnp.float32),
            jax.ShapeDtypeStruct((B, halo, D), jnp.float32),
        ],
        scratch_shapes=[plane, plane, plane],
        compiler_params=pltpu.CompilerParams(
            dimension_semantics=("arbitrary", "arbitrary"), vmem_limit_bytes=VMEM_LIMIT),
        name="pool_mixer",
    )(x, past16, g, w, scale)


def _mlp_body(x, g_ref, up_ref, down_ref, fc):
    h = _rms(x, g_ref[...]).astype(jnp.bfloat16)
    acc = x
    for c in range(D_FF // fc):
        u = jnp.dot(h, up_ref[:, c * fc:(c + 1) * fc], preferred_element_type=jnp.float32)
        u = jnp.maximum(u, 0.0)
        a = (u * u).astype(jnp.bfloat16)
        acc = acc + jnp.dot(a, down_ref[c * fc:(c + 1) * fc, :], preferred_element_type=jnp.float32)
    return acc


def _mlp_kernel(*refs, fc, with_attn, with_final):
    refs = list(refs)
    x_ref = refs.pop(0)
    if with_attn:
        o_ref = refs.pop(0)
        wo_ref = refs.pop(0)
    g_ref = refs.pop(0)
    up_ref = refs.pop(0)
    down_ref = refs.pop(0)
    if with_final:
        gf_ref = refs.pop(0)
    out_ref = refs.pop(0)

    x = x_ref[...]
    if with_attn:
        x = x + jnp.dot(o_ref[...], wo_ref[...], preferred_element_type=jnp.float32)
    acc = _mlp_body(x, g_ref, up_ref, down_ref, fc)
    if with_final:
        acc = _rms(acc, gf_ref[...])
    out_ref[...] = acc


def _mlp_layer(x, g, up, down, tm, fc, attn=None, final_g=None):
    N, D = x.shape
    row = lambda i: (i, 0)
    args, specs = [x], [pl.BlockSpec((tm, D), row)]
    if attn is not None:
        o, wo = attn
        args += [o, wo]
        specs += [pl.BlockSpec((tm, Q_W), row), _const_spec(wo.shape)]
    args += [g, up, down]
    specs += [_const_spec((1, D)), _const_spec(up.shape), _const_spec(down.shape)]
    if final_g is not None:
        args.append(final_g)
        specs.append(_const_spec((1, D)))
    return pl.pallas_call(
        functools.partial(_mlp_kernel, fc=fc, with_attn=attn is not None, with_final=final_g is not None),
        grid=(N // tm,),
        in_specs=specs,
        out_specs=pl.BlockSpec((tm, D), row),
        out_shape=jax.ShapeDtypeStruct((N, D), jnp.float32),
        compiler_params=pltpu.CompilerParams(
            dimension_semantics=("arbitrary",), vmem_limit_bytes=VMEM_LIMIT),
        name="mlp_attn_out" if attn is not None else "mlp",
    )(*args)


def _rope_block(xb, tab_ref, special):
    o = LANES if special else 0
    c = tab_ref[:, o:o + LANES]
    s1 = tab_ref[:, 2 * LANES + o:3 * LANES + o]
    s2 = tab_ref[:, 4 * LANES + o:5 * LANES + o]
    return xb * c + pltpu.roll(xb, LANES - HEAD_DIM // 2, 1) * s1 + pltpu.roll(xb, HEAD_DIM // 2, 1) * s2


def _rope_rows(p, c, s):
    half = HEAD_DIM // 2
    r = p.reshape(p.shape[0] // HEAD_DIM, 2, half, p.shape[1])
    x1, x2 = r[:, 0], r[:, 1]
    out = jnp.stack([x1 * c - x2 * s, x2 * c + x1 * s], axis=1)
    return out.reshape(p.shape)


def _proj_kernel(x_ref, g_ref, w_ref, wt_ref, tab_ref, tabt_ref,
                 kc_ref, vc_ref, kp_ref, kw_ref, kib_ref, qt_ref, qit_ref, wit_ref, vt_ref):
    h = _rms(x_ref[0], g_ref[...]).astype(jnp.bfloat16)

    def proj(off, width):
        return jnp.dot(h, w_ref[:, off:off + width], preferred_element_type=jnp.float32)

    def rope(p, special=False):
        return jnp.concatenate(
            [_rope_block(p[:, j:j + LANES], tab_ref, special) for j in range(0, p.shape[1], LANES)], axis=-1)

    kc = rope(proj(OFF_KC, KV_W))
    kc_ref[0] = kc
    vc_ref[0] = proj(OFF_VC, KV_W)
    gap = jnp.zeros((kc.shape[0], LANES - HEAD_DIM), jnp.float32)
    kp = jnp.concatenate(
        [piece for g in range(N_KV_HEADS) for piece in (kc[:, g * HEAD_DIM:(g + 1) * HEAD_DIM], gap)], axis=1)
    kp_ref[0] = kp.astype(kp_ref.dtype)
    kw = rope(proj(OFF_KW, LANES), special=True)
    kw_ref[0] = kw
    kib_ref[0] = kw.astype(kib_ref.dtype)

    def proj_t(row, n):
        return lax.dot_general(wt_ref[row:row + n, :], h, _NT, preferred_element_type=jnp.float32)

    half = HEAD_DIM // 2
    c, s = tabt_ref[0:half, :], tabt_ref[half:HEAD_DIM, :]
    qt_ref[0] = _rope_rows(proj_t(ROW_Q, Q_W), c * LOG2E, s * LOG2E).astype(qt_ref.dtype)
    qit_ref[0] = _rope_rows(proj_t(ROW_QI, QI_W), c, s).astype(qit_ref.dtype)
    vw = proj_t(ROW_V, KV_W + PACKED_ROWS)
    wit_ref[0] = vw[KV_W:KV_W + N_IDX_HEADS, :] * (N_IDX_HEADS ** -0.5)
    n = vw.shape[1]
    tail = jnp.where(lax.broadcasted_iota(jnp.int32, (LANES - HEAD_DIM, n), 0) == 0, 1.0, 0.0)
    vt = jnp.concatenate(
        [piece for g in range(N_KV_HEADS) for piece in (vw[g * HEAD_DIM:(g + 1) * HEAD_DIM], tail)], axis=0)
    kbw = vt_ref.shape[3]
    for c in range(vt_ref.shape[1]):
        vt_ref[0, c] = vt[:, c * kbw:(c + 1) * kbw].astype(vt_ref.dtype)


def _proj_layer(x, g, w_rowmajor, w_transposed, tab, tabt, tm):
    B, T, D = x.shape
    blk = lambda w: pl.BlockSpec((1, tm, w), lambda b, t: (b, t, 0))
    blk_t = lambda r: pl.BlockSpec((1, r, tm), lambda b, t: (b, 0, t))
    bf, f32 = jnp.bfloat16, jnp.float32
    sds = jax.ShapeDtypeStruct
    kbw = min(tm, KEY_BLOCK)
    return pl.pallas_call(
        _proj_kernel,
        grid=(B, T // tm),
        in_specs=[
            blk(D),
            _const_spec((1, D)),
            _const_spec(w_rowmajor.shape),
            _const_spec(w_transposed.shape),
            pl.BlockSpec((tm, tab.shape[1]), lambda b, t: (t, 0)),
            pl.BlockSpec((HEAD_DIM, tm), lambda b, t: (0, t)),
        ],
        out_specs=[blk(KV_W), blk(KV_W), blk(KV_PAD_W), blk(LANES), blk(LANES),
                   blk_t(Q_W), blk_t(QI_W), blk_t(N_IDX_HEADS),
                   pl.BlockSpec((1, tm // kbw, KV_PAD_W, kbw), lambda b, t: (b, t, 0, 0))],
        out_shape=[sds((B, T, KV_W), f32), sds((B, T, KV_W), f32), sds((B, T, KV_PAD_W), bf),
                   sds((B, T, LANES), f32), sds((B, T, LANES), bf),
                   sds((B, Q_W, T), bf), sds((B, QI_W, T), bf), sds((B, N_IDX_HEADS, T), f32),
                   sds((B, T // kbw, KV_PAD_W, kbw), bf)],
        compiler_params=pltpu.CompilerParams(
            dimension_semantics=("arbitrary", "arbitrary"), vmem_limit_bytes=VMEM_LIMIT),
        name="attn_in_proj",
    )(x, g, w_rowmajor, w_transposed, tab, tabt)


def _attn_kernel(qt_ref, qit_ref, wit_ref, kp_ref, vt_ref, kib_ref, o_ref,
                 key_ref, hi_ref, lo_ref, lo2_ref, f_ref, s0_ref, s1_ref, oacc_ref, *,
                 tq, tq_real, n_keys_real, pos0, n_sel):
    kb = KEY_BLOCK
    t = pl.program_id(1)
    lane_q = lax.broadcasted_iota(jnp.int32, (1, tq), 1)
    q_limit = ((pos0 + t * tq + lane_q) // CHUNK + 1) * CHUNK
    qvalid = lane_q < tq_real
    n_reach = jnp.minimum(((pos0 + (t + 1) * tq - 1) // CHUNK + 1) * CHUNK, n_keys_real)
    nkb = (n_reach + kb - 1) // kb
    n_unvisited = jnp.maximum(n_keys_real - nkb * kb, 0)
    key_row = lax.broadcasted_iota(jnp.int32, (kb, tq), 0)

    def block_start(j):
        return pl.multiple_of(j * kb, kb)

    def for_blocks(body, carry):
        def two(i, c):
            return body(2 * i + 1, body(2 * i, c))
        carry = lax.fori_loop(0, nkb // 2, two, carry)
        return lax.fori_loop(nkb // 2 * 2, nkb, body, carry)

    def admissible(k0):
        return key_row < q_limit - k0

    wi = wit_ref[0]

    def score_block(j, carry):
        k0 = block_start(j)
        kx = kib_ref[0, pl.ds(k0, kb), 0:IDX_DIM]
        acc = jnp.zeros((kb, tq), jnp.float32)
        for hp in range(N_IDX_HEADS // 2):
            qi2 = jnp.concatenate(
                [qit_ref[0, (2 * hp + u) * IDX_DIM:(2 * hp + u + 1) * IDX_DIM, :] for u in range(2)], axis=1)
            lg = jnp.dot(kx, qi2, preferred_element_type=jnp.float32)
            for u in range(2):
                hh = 2 * hp + u
                acc = acc + jnp.maximum(lg[:, u * tq:(u + 1) * tq], 0.0) * wi[hh:hh + 1, :]
        sc = jnp.where(admissible(k0), acc, NEG)
        if n_keys_real % kb:
            sc = jnp.where(key_row < n_keys_real - k0, sc, -jnp.inf)
        bits = lax.bitcast_convert_type(sc, jnp.int32)
        key = bits ^ ((bits >> 31) & jnp.int32(0x7FFFFFFF))
        key_ref[pl.ds(k0, kb), :] = key
        hi_ref[pl.ds(k0, kb), :] = (key >> 16).astype(jnp.int16)
        lo_ref[pl.ds(k0, kb), :] = (key ^ 0x8000).astype(jnp.int16)
        return carry

    for_blocks(score_block, 0)

    def count(pred):
        def body(j, acc):
            blk = key_ref[pl.ds(block_start(j), kb), :]
            return acc + _fold_rows(jnp.add, jnp.where(pred(blk, j), 1, 0).astype(jnp.int32))
        acc = for_blocks(body, jnp.zeros((SUBLANES, tq), jnp.int32))
        return jnp.sum(acc, axis=0, keepdims=True)

    def count16(ref, pred):
        def body(j, acc):
            hit = jnp.where(pred(ref[pl.ds(block_start(j), kb), :]), jnp.int16(1), jnp.int16(0))
            return acc + _fold_rows(jnp.add, hit, rows=PACKED_ROWS)
        acc = for_blocks(body, jnp.zeros((PACKED_ROWS, tq), jnp.int16))
        return jnp.sum(acc.astype(jnp.int32), axis=0, keepdims=True)

    def as_plane(v):
        tile = jnp.broadcast_to(v, (PACKED_ROWS, tq)).astype(jnp.int16)
        return jnp.concatenate([tile] * (kb // PACKED_ROWS), axis=0)

    def radix16(ref, n_wanted, n_all, unvisited_ge):
        def step(i, carry):
            thr, n_at = carry
            cand = thr + jnp.left_shift(jnp.int32(1), 15 - i)
            plane = as_plane(cand)
            n_ge = count16(ref, lambda blk: blk >= plane) + unvisited_ge(cand)
            take = n_ge >= n_wanted
            return jnp.where(take, cand, thr), jnp.where(take, n_ge, n_at)
        return lax.fori_loop(0, 16, step, (jnp.full((1, tq), -2 ** 15, jnp.int32), n_all))

    n_all = jnp.broadcast_to(nkb * kb + n_unvisited, (1, tq))
    thr_hi, n_ge_hi = radix16(hi_ref, n_sel, n_all, lambda cand: jnp.where(cand <= NEG_KEY_HI, n_unvisited, 0))
    hi_plane = as_plane(thr_hi)
    n_gt_hi = count16(hi_ref, lambda blk: blk > hi_plane) + jnp.where(thr_hi < NEG_KEY_HI, n_unvisited, 0)

    def park_block(j, carry):
        rows = pl.ds(block_start(j), kb)
        lo2_ref[rows, :] = jnp.where(hi_ref[rows, :] == hi_plane, lo_ref[rows, :], jnp.int16(-2 ** 15))
        return carry

    for_blocks(park_block, 0)
    neg_in_bucket = thr_hi == NEG_KEY_HI
    thr_lo, n_ge_lo = radix16(lo2_ref, n_sel - n_gt_hi, n_ge_hi - n_gt_hi,
                              lambda cand: jnp.where(neg_in_bucket & (cand <= NEG_KEY_LO), n_unvisited, 0))
    thr = thr_hi * 65536 + (thr_lo + 2 ** 15)
    lo_plane = as_plane(thr_lo)
    n_gt_lo = (count16(lo2_ref, lambda blk: blk > lo_plane)
               + jnp.where(neg_in_bucket & (thr_lo < NEG_KEY_LO), n_unvisited, 0))
    n_gt = n_gt_hi + n_gt_lo
    n_eq = n_ge_lo - n_gt_lo
    n_eq_adm = count(lambda blk, j: (blk == thr) & admissible(block_start(j)))
    need = n_sel - n_gt
    tie = (n_eq > need) & (n_eq_adm > 0) & qvalid
    any_tie = jnp.max(tie.astype(jnp.int32)) > 0

    def write_mask(exact_ties):
        def body(j, run):
            k0 = block_start(j)
            kblk = key_ref[pl.ds(k0, kb), :]
            if exact_ties:
                r = lax.broadcasted_iota(jnp.int32, (kb, kb), 0)
                c = lax.broadcasted_iota(jnp.int32, (kb, kb), 1)
                tri = jnp.where(r >= c, 1.0, 0.0).astype(jnp.bfloat16)
                eqb = kblk == thr
                eqf = jnp.where(eqb, 1.0, 0.0)
                rank = jnp.dot(tri, eqf.astype(jnp.bfloat16), preferred_element_type=jnp.float32) + run
                run = run + jnp.sum(eqf, axis=0, keepdims=True)
                sel = (kblk > thr) | (eqb & (rank <= need.astype(jnp.float32)))
            else:
                sel = kblk >= thr
            f_ref[pl.ds(k0, kb), :] = jnp.where(sel, jnp.where(admissible(k0), jnp.inf, NEG), -jnp.inf)
            return run

        for_blocks(body, jnp.zeros((1, tq), jnp.float32))

    @pl.when(any_tie)
    def _():
        write_mask(True)

    @pl.when(jnp.logical_not(any_tie))
    def _():
        write_mask(False)

    gw = KV_GROUP * tq
    s_refs = (s0_ref, s1_ref)

    def logits_block(g, j):
        k0 = block_start(j)
        q4 = jnp.concatenate(
            [qt_ref[0, (g * KV_GROUP + r) * HEAD_DIM:(g * KV_GROUP + r + 1) * HEAD_DIM, :] for r in range(KV_GROUP)],
            axis=1)
        kblk = kp_ref[0, pl.ds(k0, kb), g * LANES:g * LANES + HEAD_DIM]
        s = jnp.dot(kblk, q4, preferred_element_type=jnp.float32)
        f = f_ref[pl.ds(k0, kb), :]
        s = jnp.minimum(s, jnp.concatenate([f] * KV_GROUP, axis=1))
        s_refs[g % 2][pl.ds(k0, kb), :] = s
        return _fold_rows(jnp.maximum, s)

    def values_block(g, j, m):
        p = jnp.exp2(s_refs[g % 2][pl.ds(block_start(j), kb), :] - m).astype(jnp.bfloat16)
        oacc_ref[...] += jnp.dot(vt_ref[0, j, g * LANES:(g + 1) * LANES, :], p,
                                 preferred_element_type=jnp.float32)

    m_init = jnp.full((SUBLANES, gw), -jnp.inf, jnp.float32)
    m_acc = for_blocks(lambda j, a: jnp.maximum(a, logits_block(0, j)), m_init)
    for g in range(N_KV_HEADS):
        m = jnp.max(m_acc, axis=0, keepdims=True)
        oacc_ref[...] = jnp.zeros(oacc_ref.shape, jnp.float32)

        def fused(j, a, g=g, m=m):
            if g + 1 < N_KV_HEADS:
                a = jnp.maximum(a, logits_block(g + 1, j))
            values_block(g, j, m)
            return a

        m_acc = for_blocks(fused, m_init)
        oacc = oacc_ref[...]
        on = oacc[0:HEAD_DIM] / oacc[HEAD_DIM:HEAD_DIM + 1]
        for u in range(KV_GROUP // 2):
            pair = jnp.concatenate([on[:, (2 * u) * tq:(2 * u + 1) * tq], on[:, (2 * u + 1) * tq:(2 * u + 2) * tq]],
                                   axis=0)
            c0 = (g * KV_GROUP + 2 * u) * HEAD_DIM
            o_ref[0, :, c0:c0 + 2 * HEAD_DIM] = pair.T.astype(o_ref.dtype)


def _attn_layer(qt, qit, wit, kp, vt, kib, *, tq, tq_real, n_keys_real, pos0, n_sel):
    B, _, T = qt.shape
    n_keys = kp.shape[1]
    qblk = lambda r: pl.BlockSpec((1, r, tq), lambda b, t: (b, 0, t))
    kblk = lambda w: pl.BlockSpec((1, n_keys, w), lambda b, t: (b, 0, 0))
    return pl.pallas_call(
        functools.partial(_attn_kernel, tq=tq, tq_real=tq_real, n_keys_real=n_keys_real, pos0=pos0, n_sel=n_sel),
        grid=(B, T // tq),
        in_specs=[qblk(Q_W), qblk(QI_W), qblk(N_IDX_HEADS), kblk(KV_PAD_W),
                  pl.BlockSpec((1,) + vt.shape[1:], lambda b, t: (b, 0, 0, 0)), kblk(LANES)],
        out_specs=pl.BlockSpec((1, tq, Q_W), lambda b, t: (b, t, 0)),
        out_shape=jax.ShapeDtypeStruct((B, T, Q_W), jnp.bfloat16),
        scratch_shapes=[pltpu.VMEM((n_keys, tq), jnp.int32)] + [pltpu.VMEM((n_keys, tq), jnp.int16)] * 3 + [
                        pltpu.VMEM((n_keys, tq), jnp.float32),
                        pltpu.VMEM((n_keys, KV_GROUP * tq), jnp.float32),
                        pltpu.VMEM((n_keys, KV_GROUP * tq), jnp.float32),
                        pltpu.VMEM((LANES, KV_GROUP * tq), jnp.float32)],
        compiler_params=pltpu.CompilerParams(
            dimension_semantics=("arbitrary", "arbitrary"), vmem_limit_bytes=VMEM_LIMIT),
        name="dsa_attention",
    )(qt, qit, wit, kp, vt, kib)


def _pad_heads(a):
    lead = a.shape[:-1]
    a = a.reshape(*lead, N_KV_HEADS, HEAD_DIM)
    a = jnp.pad(a, [(0, 0)] * len(lead) + [(0, 0), (0, LANES - HEAD_DIM)])
    return a.reshape(*lead, KV_PAD_W)


def _pack_w_in(w_in):
    o = 0
    wq = w_in[:, o:o + Q_W] * (HEAD_DIM ** -0.5); o += Q_W
    wk = w_in[:, o:o + KV_W]; o += KV_W
    wv = w_in[:, o:o + KV_W]; o += KV_W
    wqi = w_in[:, o:o + QI_W] * (IDX_DIM ** -0.5); o += QI_W
    wki = w_in[:, o:o + IDX_DIM]; o += IDX_DIM
    wwi = w_in[:, o:o + N_IDX_HEADS]
    wkw = jnp.pad(jnp.concatenate([wki, wwi], axis=1), ((0, 0), (0, LANES - IDX_DIM - N_IDX_HEADS)))
    rowmajor = jnp.concatenate([wk, wv, wkw], axis=1)
    wwi_rows = jnp.pad(wwi, ((0, 0), (0, PACKED_ROWS - N_IDX_HEADS)))
    transposed = jnp.concatenate([wq, wqi, wv, wwi_rows], axis=1).T
    return rowmajor.astype(jnp.bfloat16), transposed.astype(jnp.bfloat16)


def _rope_tables(pos):
    inv_freq = 1.0 / (ROPE_THETA ** (jnp.arange(0, HEAD_DIM, 2, dtype=jnp.float32) / HEAD_DIM))
    ang = pos.astype(jnp.float32)[:, None] * inv_freq[None, :]
    c, s = jnp.cos(ang), jnp.sin(ang)
    z = jnp.zeros_like(c)
    n = pos.shape[0]
    zpad = jnp.zeros((n, LANES - HEAD_DIM), jnp.float32)
    wi_scale = jnp.full((n, N_IDX_HEADS), N_IDX_HEADS ** -0.5, jnp.float32)
    c_kw = jnp.concatenate([c, c, wi_scale, jnp.zeros((n, LANES - HEAD_DIM - N_IDX_HEADS), jnp.float32)], axis=1)
    tab = jnp.concatenate([
        jnp.concatenate([c, c, c, c], axis=1), c_kw,
        jnp.concatenate([-s, z, -s, z], axis=1), jnp.concatenate([-s, z, zpad], axis=1),
        jnp.concatenate([z, s, z, s], axis=1), jnp.concatenate([z, s, zpad], axis=1),
    ], axis=1)
    tabt = jnp.concatenate([c, s], axis=1).T
    return tab, tabt


def _round_up(n, m):
    return (n + m - 1) // m * m


def _value_blocks_t(v):
    B, n, _ = v.shape
    nb = n // KEY_BLOCK
    vt = jnp.transpose(v.reshape(B, nb, KEY_BLOCK, N_KV_HEADS, HEAD_DIM), (0, 1, 3, 4, 2))
    ones = jnp.ones((B, nb, N_KV_HEADS, 1, KEY_BLOCK), v.dtype)
    zeros = jnp.zeros((B, nb, N_KV_HEADS, LANES - HEAD_DIM - 1, KEY_BLOCK), v.dtype)
    return jnp.concatenate([vt, ones, zeros], axis=3).reshape(B, nb, KV_PAD_W, KEY_BLOCK)


def _trunk(x, pos0, pool_past, attn_past, norm_mix, norm_mlp, norm_final, pool_w, pool_scale,
           w_rowmajor, w_transposed, w_o, w_up, w_down):
    B, T, D = x.shape
    bf = jnp.bfloat16
    row = lambda v: v.reshape(1, D)

    if pool_past is None:
        past16 = jnp.zeros((B, POOL_STATE + 1, D), jnp.float32)
    else:
        past16 = jnp.pad(pool_past, ((0, 0), (1, 0), (0, 0)))
    tt = min(T, ROW_TILE)
    tm = min(B * T, ROW_TILE)
    x1, st = _pool_layer(x, past16, row(norm_mix[0]), pool_w[0].astype(bf), row(pool_scale[0]), pos0, tt)
    x2 = _mlp_layer(x1.reshape(B * T, D), row(norm_mlp[0]), w_up[0], w_down[0], tm, FF_CHUNK).reshape(B, T, D)
    pool_new = st[:, 1:][None]

    tab, tabt = _rope_tables(pos0 + jnp.arange(T, dtype=jnp.int32))
    kc, vc, kp, kw, kib, qt, qit, wit, vt = _proj_layer(
        x2, row(norm_mix[1]), w_rowmajor, w_transposed, tab, tabt, min(T, PROJ_TILE))
    k_new = kc.reshape(1, B, T, N_KV_HEADS, HEAD_DIM)
    v_new = vc.reshape(1, B, T, N_KV_HEADS, HEAD_DIM)
    ki_new = kw[:, :, :IDX_DIM][None]

    if attn_past is None:
        n_real = T
        kp_all, vt_all, kib_all = kp, vt, kib
    else:
        ck, cv, cki = attn_past
        P = ck.shape[1]
        n_real = P + T
        n_keys = _round_up(n_real, KEY_BLOCK)
        fill = lambda w: jnp.zeros((B, n_keys - n_real, w), bf)
        kp_all = jnp.concatenate([_pad_heads(ck.reshape(B, P, KV_W).astype(bf)), kp, fill(KV_PAD_W)], axis=1)
        kib_all = jnp.concatenate(
            [jnp.pad(cki.astype(bf), ((0, 0), (0, 0), (0, LANES - IDX_DIM))), kib, fill(LANES)], axis=1)
        vt_all = _value_blocks_t(
            jnp.concatenate([cv.reshape(B, P, KV_W).astype(bf), vc.astype(bf), fill(KV_W)], axis=1))
    tq = min(Q_TILE, _round_up(T, LANES))
    t_pad = _round_up(T, tq)
    padq = lambda a: jnp.pad(a, ((0, 0), (0, 0), (0, t_pad - T)))
    n_sel = min(TOPK_MAX, n_real // 4)
    o = _attn_layer(padq(qt), padq(qit), padq(wit), kp_all, vt_all, kib_all,
                    tq=tq, tq_real=min(T, tq), n_keys_real=n_real, pos0=pos0, n_sel=n_sel)
    o = o[:, :T].reshape(B * T, Q_W)

    y = _mlp_layer(x2.reshape(B * T, D), row(norm_mlp[1]), w_up[1], w_down[1], tm, FF_CHUNK,
                   attn=(o, w_o), final_g=row(norm_final)).reshape(B, T, D)
    return y, pool_new, k_new, v_new, ki_new


def kernel(x_prompt, x_sample, state_pool, cache_k, cache_v, cache_kidx, norm_mix, norm_mlp, norm_final,
           pool_w, pool_scale, attn_w_in, attn_w_o, mlp_w_up, mlp_w_down):
    bf = jnp.bfloat16
    shared = (norm_mix, norm_mlp, norm_final, pool_w, pool_scale, *_pack_w_in(attn_w_in[0]),
              attn_w_o[0].astype(bf), mlp_w_up.astype(bf), mlp_w_down.astype(bf))
    y_p, pool_p, k_p, v_p, ki_p = _trunk(x_prompt, 0, None, None, *shared)
    y_s, pool_s, k_s, v_s, ki_s = _trunk(
        x_sample, cache_k.shape[2], state_pool[0], (cache_k[0], cache_v[0], cache_kidx[0]), *shared)
    return (y_p, y_s, pool_p, pool_s, k_p, v_p, ki_p, k_s, v_s, ki_s)
```

```python
import functools
import math
import struct

import jax
import jax.numpy as jnp
from jax import lax
from jax.experimental import pallas as pl
from jax.experimental.pallas import tpu as pltpu

D_MODEL = 1024
CHUNK = 64
POOL_WINDOWS = (2, 4, 8, 16)
POOL_GROUP = D_MODEL // len(POOL_WINDOWS)
POOL_STATE = max(POOL_WINDOWS) - 1
N_HEADS = 16
HEAD_DIM = 64
N_KV_HEADS = 4
KV_GROUP = N_HEADS // N_KV_HEADS
N_IDX_HEADS = 8
IDX_DIM = 64
TOPK_MAX = 256
ROPE_THETA = 10000.0
D_FF = 4 * D_MODEL
EPS = 1e-6
NEG = -1e30
Q_W = N_HEADS * HEAD_DIM
KV_W = N_KV_HEADS * HEAD_DIM
QI_W = N_IDX_HEADS * IDX_DIM
LOG2E = math.log2(math.e)

SUBLANES = 8
PACKED_ROWS = 2 * SUBLANES
LANES = 128
KV_PAD_W = N_KV_HEADS * LANES
KEY_BLOCK = 256
Q_TILE = 256
ROW_TILE = 1024
PROJ_TILE = 512
FF_CHUNK = 512

OFF_KC = 0
OFF_VC = OFF_KC + KV_W
OFF_KW = OFF_VC + KV_W
PROJ_ROWMAJOR_W = OFF_KW + LANES
ROW_Q = 0
ROW_QI = ROW_Q + Q_W
ROW_V = ROW_QI + QI_W
ROW_WI = ROW_V + KV_W
PROJ_TRANSPOSED_ROWS = ROW_WI + PACKED_ROWS

VMEM_LIMIT = 56 * 1024 * 1024

_NT = (((1,), (1,)), ((), ()))


def _f32_order_key(x):
    b = struct.unpack("<i", struct.pack("<f", x))[0]
    return b ^ ((b >> 31) & 0x7FFFFFFF)


def _signed16(v):
    return v - (1 << 16) if v >= (1 << 15) else v


NEG_KEY = _f32_order_key(NEG)
NEG_KEY_HI = NEG_KEY >> 16
NEG_KEY_LO = _signed16((NEG_KEY & 0xFFFF) ^ 0x8000)


def _rms(x, g):
    return x * lax.rsqrt(jnp.mean(x * x, axis=-1, keepdims=True) + EPS) * g


def _const_spec(shape):
    n = len(shape)
    return pl.BlockSpec(shape, lambda *_: (0,) * n, pipeline_mode=pl.Buffered(1))


def _tree(op, parts):
    parts = list(parts)
    while len(parts) > 1:
        parts = [op(parts[i], parts[i + 1]) if i + 1 < len(parts) else parts[i] for i in range(0, len(parts), 2)]
    return parts[0]


def _fold_rows(op, x, rows=SUBLANES):
    return _tree(op, [x[r:r + rows] for r in range(0, x.shape[0], rows)])


def _pool_kernel(x_ref, past_ref, g_ref, w_ref, scale_ref, x1_ref, st_ref, hp_ref, sa_ref, sb_ref, *, tt, pos0):
    t = pl.program_id(1)
    halo = POOL_STATE + 1
    lead = SUBLANES
    top = lead + halo

    @pl.when(t == 0)
    def _():
        zeros = jnp.zeros((lead, hp_ref.shape[1]), jnp.float32)
        hp_ref[0:lead, :] = zeros
        sa_ref[0:lead, :] = zeros
        sb_ref[0:lead, :] = zeros
        hp_ref[lead:top, :] = past_ref[0]

    x = x_ref[0]
    h = _rms(x, g_ref[...])
    hp_ref[top:top + tt, :] = h

    n = halo + tt
    g1, g2, g3 = POOL_GROUP, 2 * POOL_GROUP, 3 * POOL_GROUP
    sa_ref[lead:lead + n, :] = hp_ref[lead:lead + n, :] + hp_ref[lead - 1:lead - 1 + n, :]
    sb_ref[lead:lead + n, g1:] = sa_ref[lead:lead + n, g1:] + sa_ref[lead - 2:lead - 2 + n, g1:]
    sa_ref[lead:lead + n, g2:] = sb_ref[lead:lead + n, g2:] + sb_ref[lead - 4:lead - 4 + n, g2:]
    window_sums = (
        sa_ref[top:top + tt, 0:g1],
        sb_ref[top:top + tt, g1:g2],
        sa_ref[top:top + tt, g2:g3],
        sa_ref[top:top + tt, g3:] + sa_ref[top - 8:top - 8 + tt, g3:],
    )
    pos = pos0 + t * tt + lax.broadcasted_iota(jnp.int32, (tt, 1), 0)
    for g, win in enumerate(POOL_WINDOWS):
        c0, c1 = g * POOL_GROUP, (g + 1) * POOL_GROUP
        cnt = jnp.minimum(pos + 1, win).astype(jnp.float32)
        diff = window_sums[g] / cnt - h[:, c0:c1]
        y = jnp.dot(diff.astype(jnp.bfloat16), w_ref[g], preferred_element_type=jnp.float32)
        x1_ref[0, :, c0:c1] = x[:, c0:c1] + y * scale_ref[:, c0:c1]
    tail = hp_ref[lead + tt:top + tt, :]
    st_ref[0] = tail
    hp_ref[lead:top, :] = tail


def _pool_layer(x, past16, g, w, scale, pos0, tt):
    B, T, D = x.shape
    halo = POOL_STATE + 1
    plane = pltpu.VMEM((SUBLANES + halo + tt, D), jnp.float32)
    return pl.pallas_call(
        functools.partial(_pool_kernel, tt=tt, pos0=pos0),
        grid=(B, T // tt),
        in_specs=[
            pl.BlockSpec((1, tt, D), lambda b, t: (b, t, 0)),
            pl.BlockSpec((1, halo, D), lambda b, t: (b, 0, 0)),
            _const_spec((1, D)),
            _const_spec(w.shape),
            _const_spec((1, D)),
        ],
        out_specs=[
            pl.BlockSpec((1, tt, D), lambda b, t: (b, t, 0)),
            pl.BlockSpec((1, halo, D), lambda b, t: (b, 0, 0)),
        ],
        out_shape=[
            jax.ShapeDtypeStruct((B, T, D), jnp.float32),
            jax.ShapeDtypeStruct((B, halo, D), jnp.float32),
        ],
        scratch_shapes=[plane, plane, plane],
        compiler_params=pltpu.CompilerParams(
            dimension_semantics=("arbitrary", "arbitrary"), vmem_limit_bytes=VMEM_LIMIT),
        name="pool_mixer",
    )(x, past16, g, w, scale)


def _mlp_body(x, g_ref, up_ref, down_ref, fc):
    h = _rms(x, g_ref[...]).astype(jnp.bfloat16)
    acc = x
    for c in range(D_FF // fc):
        u = jnp.dot(h, up_ref[:, c * fc:(c + 1) * fc], preferred_element_type=jnp.float32)
        u = jnp.maximum(u, 0.0)
        a = (u * u).astype(jnp.bfloat16)
        acc = acc + jnp.dot(a, down_ref[c * fc:(c + 1) * fc, :], preferred_element_type=jnp.float32)
    return acc


def _mlp_kernel(*refs, fc, with_attn, with_final):
    refs = list(refs)
    x_ref = refs.pop(0)
    if with_attn:
        o_ref = refs.pop(0)
        wo_ref = refs.pop(0)
    g_ref = refs.pop(0)
    up_ref = refs.pop(0)
    down_ref = refs.pop(0)
    if with_final:
        gf_ref = refs.pop(0)
    out_ref = refs.pop(0)

    x = x_ref[...]
    if with_attn:
        x = x + jnp.dot(o_ref[...], wo_ref[...], preferred_element_type=jnp.float32)
    acc = _mlp_body(x, g_ref, up_ref, down_ref, fc)
    if with_final:
        acc = _rms(acc, gf_ref[...])
    out_ref[...] = acc


def _mlp_layer(x, g, up, down, tm, fc, attn=None, final_g=None):
    N, D = x.shape
    row = lambda i: (i, 0)
    args, specs = [x], [pl.BlockSpec((tm, D), row)]
    if attn is not None:
        o, wo = attn
        args += [o, wo]
        specs += [pl.BlockSpec((tm, Q_W), row), _const_spec(wo.shape)]
    args += [g, up, down]
    specs += [_const_spec((1, D)), _const_spec(up.shape), _const_spec(down.shape)]
    if final_g is not None:
        args.append(final_g)
        specs.append(_const_spec((1, D)))
    return pl.pallas_call(
        functools.partial(_mlp_kernel, fc=fc, with_attn=attn is not None, with_final=final_g is not None),
        grid=(N // tm,),
        in_specs=specs,
        out_specs=pl.BlockSpec((tm, D), row),
        out_shape=jax.ShapeDtypeStruct((N, D), jnp.float32),
        compiler_params=pltpu.CompilerParams(
            dimension_semantics=("arbitrary",), vmem_limit_bytes=VMEM_LIMIT),
        name="mlp_attn_out" if attn is not None else "mlp",
    )(*args)


def _rope_block(xb, tab_ref, special):
    o = LANES if special else 0
    c = tab_ref[:, o:o + LANES]
    s1 = tab_ref[:, 2 * LANES + o:3 * LANES + o]
    s2 = tab_ref[:, 4 * LANES + o:5 * LANES + o]
    return xb * c + pltpu.roll(xb, LANES - HEAD_DIM // 2, 1) * s1 + pltpu.roll(xb, HEAD_DIM // 2, 1) * s2


def _rope_rows(p, c, s):
    half = HEAD_DIM // 2
    r = p.reshape(p.shape[0] // HEAD_DIM, 2, half, p.shape[1])
    x1, x2 = r[:, 0], r[:, 1]
    out = jnp.stack([x1 * c - x2 * s, x2 * c + x1 * s], axis=1)
    return out.reshape(p.shape)


def _proj_kernel(x_ref, g_ref, w_ref, wt_ref, tab_ref, tabt_ref,
                 kc_ref, vc_ref, kp_ref, kw_ref, kib_ref, qt_ref, qit_ref, wit_ref, vt_ref):
    h = _rms(x_ref[0], g_ref[...]).astype(jnp.bfloat16)

    def proj(off, width):
        return jnp.dot(h, w_ref[:, off:off + width], preferred_element_type=jnp.float32)

    def rope(p, special=False):
        return jnp.concatenate(
            [_rope_block(p[:, j:j + LANES], tab_ref, special) for j in range(0, p.shape[1], LANES)], axis=-1)

    kc = rope(proj(OFF_KC, KV_W))
    kc_ref[0] = kc
    vc_ref[0] = proj(OFF_VC, KV_W)
    gap = jnp.zeros((kc.shape[0], LANES - HEAD_DIM), jnp.float32)
    kp = jnp.concatenate(
        [piece for g in range(N_KV_HEADS) for piece in (kc[:, g * HEAD_DIM:(g + 1) * HEAD_DIM], gap)], axis=1)
    kp_ref[0] = kp.astype(kp_ref.dtype)
    kw = rope(proj(OFF_KW, LANES), special=True)
    kw_ref[0] = kw
    kib_ref[0] = kw.astype(kib_ref.dtype)

    def proj_t(row, n):
        return lax.dot_general(wt_ref[row:row + n, :], h, _NT, preferred_element_type=jnp.float32)

    half = HEAD_DIM // 2
    c, s = tabt_ref[0:half, :], tabt_ref[half:HEAD_DIM, :]
    qt_ref[0] = _rope_rows(proj_t(ROW_Q, Q_W), c * LOG2E, s * LOG2E).astype(qt_ref.dtype)
    qit_ref[0] = _rope_rows(proj_t(ROW_QI, QI_W), c, s).astype(qit_ref.dtype)
    vw = proj_t(ROW_V, KV_W + PACKED_ROWS)
    wit_ref[0] = vw[KV_W:KV_W + N_IDX_HEADS, :] * (N_IDX_HEADS ** -0.5)
    n = vw.shape[1]
    tail = jnp.where(lax.broadcasted_iota(jnp.int32, (LANES - HEAD_DIM, n), 0) == 0, 1.0, 0.0)
    vt = jnp.concatenate(
        [piece for g in range(N_KV_HEADS) for piece in (vw[g * HEAD_DIM:(g + 1) * HEAD_DIM], tail)], axis=0)
    kbw = vt_ref.shape[3]
    for c in range(vt_ref.shape[1]):
        vt_ref[0, c] = vt[:, c * kbw:(c + 1) * kbw].astype(vt_ref.dtype)


def _proj_layer(x, g, w_rowmajor, w_transposed, tab, tabt, tm):
    B, T, D = x.shape
    blk = lambda w: pl.BlockSpec((1, tm, w), lambda b, t: (b, t, 0))
    blk_t = lambda r: pl.BlockSpec((1, r, tm), lambda b, t: (b, 0, t))
    bf, f32 = jnp.bfloat16, jnp.float32
    sds = jax.ShapeDtypeStruct
    kbw = min(tm, KEY_BLOCK)
    return pl.pallas_call(
        _proj_kernel,
        grid=(B, T // tm),
        in_specs=[
            blk(D),
            _const_spec((1, D)),
            _const_spec(w_rowmajor.shape),
            _const_spec(w_transposed.shape),
            pl.BlockSpec((tm, tab.shape[1]), lambda b, t: (t, 0)),
            pl.BlockSpec((HEAD_DIM, tm), lambda b, t: (0, t)),
        ],
        out_specs=[blk(KV_W), blk(KV_W), blk(KV_PAD_W), blk(LANES), blk(LANES),
                   blk_t(Q_W), blk_t(QI_W), blk_t(N_IDX_HEADS),
                   pl.BlockSpec((1, tm // kbw, KV_PAD_W, kbw), lambda b, t: (b, t, 0, 0))],
        out_shape=[sds((B, T, KV_W), f32), sds((B, T, KV_W), f32), sds((B, T, KV_PAD_W), bf),
                   sds((B, T, LANES), f32), sds((B, T, LANES), bf),
                   sds((B, Q_W, T), bf), sds((B, QI_W, T), bf), sds((B, N_IDX_HEADS, T), f32),
                   sds((B, T // kbw, KV_PAD_W, kbw), bf)],
        compiler_params=pltpu.CompilerParams(
            dimension_semantics=("arbitrary", "arbitrary"), vmem_limit_bytes=VMEM_LIMIT),
        name="attn_in_proj",
    )(x, g, w_rowmajor, w_transposed, tab, tabt)


def _attn_kernel(qt_ref, qit_ref, wit_ref, kp_ref, vt_ref, kib_ref, o_ref,
                 key_ref, hi_ref, lo_ref, lo2_ref, f_ref, s0_ref, s1_ref, oacc_ref, *,
                 tq, tq_real, n_keys_real, pos0, n_sel):
    kb = KEY_BLOCK
    t = pl.program_id(1)
    lane_q = lax.broadcasted_iota(jnp.int32, (1, tq), 1)
    q_limit = ((pos0 + t * tq + lane_q) // CHUNK + 1) * CHUNK
    qvalid = lane_q < tq_real
    n_reach = jnp.minimum(((pos0 + (t + 1) * tq - 1) // CHUNK + 1) * CHUNK, n_keys_real)
    nkb = (n_reach + kb - 1) // kb
    n_unvisited = jnp.maximum(n_keys_real - nkb * kb, 0)
    key_row = lax.broadcasted_iota(jnp.int32, (kb, tq), 0)

    def block_start(j):
        return pl.multiple_of(j * kb, kb)

    def for_blocks(body, carry, tiers=(2, 1)):
        start = 0
        for n in tiers:
            def trip(i, c, n=n, start=start):
                for u in range(n):
                    c = body(start + n * i + u, c)
                return c
            trips = (nkb - start) // n
            carry = lax.fori_loop(0, trips, trip, carry)
            start = start + trips * n
        return carry

    def admissible(k0):
        return key_row < q_limit - k0

    wi = wit_ref[0]

    def score_block(j, carry):
        k0 = block_start(j)
        kx = kib_ref[0, pl.ds(k0, kb), 0:IDX_DIM]
        acc = jnp.zeros((kb, tq), jnp.float32)
        for hp in range(N_IDX_HEADS // 2):
            qi2 = jnp.concatenate(
                [qit_ref[0, (2 * hp + u) * IDX_DIM:(2 * hp + u + 1) * IDX_DIM, :] for u in range(2)], axis=1)
            lg = jnp.dot(kx, qi2, preferred_element_type=jnp.float32)
            for u in range(2):
                hh = 2 * hp + u
                acc = acc + jnp.maximum(lg[:, u * tq:(u + 1) * tq], 0.0) * wi[hh:hh + 1, :]
        sc = jnp.where(admissible(k0), acc, NEG)
        if n_keys_real % kb:
            sc = jnp.where(key_row < n_keys_real - k0, sc, -jnp.inf)
        bits = lax.bitcast_convert_type(sc, jnp.int32)
        key = bits ^ ((bits >> 31) & jnp.int32(0x7FFFFFFF))
        key_ref[pl.ds(k0, kb), :] = key
        hi_ref[pl.ds(k0, kb), :] = (key >> 16).astype(jnp.int16)
        lo_ref[pl.ds(k0, kb), :] = (key ^ 0x8000).astype(jnp.int16)
        return carry

    for_blocks(score_block, 0)

    def count(pred):
        def body(j, acc):
            blk = key_ref[pl.ds(block_start(j), kb), :]
            return acc + _fold_rows(jnp.add, jnp.where(pred(blk, j), 1, 0).astype(jnp.int32))
        acc = for_blocks(body, jnp.zeros((SUBLANES, tq), jnp.int32))
        return jnp.sum(acc, axis=0, keepdims=True)

    def count16(ref, pred):
        def body(j, acc):
            hit = jnp.where(pred(ref[pl.ds(block_start(j), kb), :]), jnp.int16(1), jnp.int16(0))
            return acc + _fold_rows(jnp.add, hit, rows=PACKED_ROWS)
        acc = for_blocks(body, jnp.zeros((PACKED_ROWS, tq), jnp.int16))
        return jnp.sum(acc.astype(jnp.int32), axis=0, keepdims=True)

    def as_plane(v):
        tile = jnp.broadcast_to(v, (PACKED_ROWS, tq)).astype(jnp.int16)
        return jnp.concatenate([tile] * (kb // PACKED_ROWS), axis=0)

    def radix16(ref, n_wanted, n_all, unvisited_ge):
        def step(i, carry):
            thr, n_at = carry
            cand = thr + jnp.left_shift(jnp.int32(1), 15 - i)
            plane = as_plane(cand)
            n_ge = count16(ref, lambda blk: blk >= plane) + unvisited_ge(cand)
            take = n_ge >= n_wanted
            return jnp.where(take, cand, thr), jnp.where(take, n_ge, n_at)
        return lax.fori_loop(0, 16, step, (jnp.full((1, tq), -2 ** 15, jnp.int32), n_all))

    n_all = jnp.broadcast_to(nkb * kb + n_unvisited, (1, tq))
    thr_hi, n_ge_hi = radix16(hi_ref, n_sel, n_all, lambda cand: jnp.where(cand <= NEG_KEY_HI, n_unvisited, 0))
    hi_plane = as_plane(thr_hi)
    n_gt_hi = count16(hi_ref, lambda blk: blk > hi_plane) + jnp.where(thr_hi < NEG_KEY_HI, n_unvisited, 0)

    def park_block(j, carry):
        rows = pl.ds(block_start(j), kb)
        lo2_ref[rows, :] = jnp.where(hi_ref[rows, :] == hi_plane, lo_ref[rows, :], jnp.int16(-2 ** 15))
        return carry

    for_blocks(park_block, 0)
    neg_in_bucket = thr_hi == NEG_KEY_HI
    thr_lo, n_ge_lo = radix16(lo2_ref, n_sel - n_gt_hi, n_ge_hi - n_gt_hi,
                              lambda cand: jnp.where(neg_in_bucket & (cand <= NEG_KEY_LO), n_unvisited, 0))
    thr = thr_hi * 65536 + (thr_lo + 2 ** 15)
    lo_plane = as_plane(thr_lo)
    n_gt_lo = (count16(lo2_ref, lambda blk: blk > lo_plane)
               + jnp.where(neg_in_bucket & (thr_lo < NEG_KEY_LO), n_unvisited, 0))
    n_gt = n_gt_hi + n_gt_lo
    n_eq = n_ge_lo - n_gt_lo
    n_eq_adm = count(lambda blk, j: (blk == thr) & admissible(block_start(j)))
    need = n_sel - n_gt
    tie = (n_eq > need) & (n_eq_adm > 0) & qvalid
    any_tie = jnp.max(tie.astype(jnp.int32)) > 0

    def write_mask(exact_ties):
        def body(j, run):
            k0 = block_start(j)
            kblk = key_ref[pl.ds(k0, kb), :]
            if exact_ties:
                r = lax.broadcasted_iota(jnp.int32, (kb, kb), 0)
                c = lax.broadcasted_iota(jnp.int32, (kb, kb), 1)
                tri = jnp.where(r >= c, 1.0, 0.0).astype(jnp.bfloat16)
                eqb = kblk == thr
                eqf = jnp.where(eqb, 1.0, 0.0)
                rank = jnp.dot(tri, eqf.astype(jnp.bfloat16), preferred_element_type=jnp.float32) + run
                run = run + jnp.sum(eqf, axis=0, keepdims=True)
                sel = (kblk > thr) | (eqb & (rank <= need.astype(jnp.float32)))
            else:
                sel = kblk >= thr
            f_ref[pl.ds(k0, kb), :] = jnp.where(sel, jnp.where(admissible(k0), jnp.inf, NEG), -jnp.inf)
            return run

        for_blocks(body, jnp.zeros((1, tq), jnp.float32))

    @pl.when(any_tie)
    def _():
        write_mask(True)

    @pl.when(jnp.logical_not(any_tie))
    def _():
        write_mask(False)

    gw = KV_GROUP * tq
    s_refs = (s0_ref, s1_ref)

    def logits_block(g, j):
        k0 = block_start(j)
        q4 = jnp.concatenate(
            [qt_ref[0, (g * KV_GROUP + r) * HEAD_DIM:(g * KV_GROUP + r + 1) * HEAD_DIM, :] for r in range(KV_GROUP)],
            axis=1)
        kblk = kp_ref[0, pl.ds(k0, kb), g * LANES:g * LANES + HEAD_DIM]
        s = jnp.dot(kblk, q4, preferred_element_type=jnp.float32)
        f = f_ref[pl.ds(k0, kb), :]
        s = jnp.minimum(s, jnp.concatenate([f] * KV_GROUP, axis=1))
        s_refs[g % 2][pl.ds(k0, kb), :] = s
        return _fold_rows(jnp.maximum, s)

    def values_block(g, j, m):
        p = jnp.exp2(s_refs[g % 2][pl.ds(block_start(j), kb), :] - m).astype(jnp.bfloat16)
        oacc_ref[...] += jnp.dot(vt_ref[0, j, g * LANES:(g + 1) * LANES, :], p,
                                 preferred_element_type=jnp.float32)

    m_init = jnp.full((SUBLANES, gw), -jnp.inf, jnp.float32)
    m_acc = for_blocks(lambda j, a: jnp.maximum(a, logits_block(0, j)), m_init, tiers=(4, 2, 1))
    for g in range(N_KV_HEADS):
        m = jnp.max(m_acc, axis=0, keepdims=True)
        oacc_ref[...] = jnp.zeros(oacc_ref.shape, jnp.float32)

        def fused(j, a, g=g, m=m):
            if g + 1 < N_KV_HEADS:
                a = jnp.maximum(a, logits_block(g + 1, j))
            values_block(g, j, m)
            return a

        m_acc = for_blocks(fused, m_init, tiers=(4, 2, 1))
        oacc = oacc_ref[...]
        on = oacc[0:HEAD_DIM] / oacc[HEAD_DIM:HEAD_DIM + 1]
        for u in range(KV_GROUP // 2):
            pair = jnp.concatenate([on[:, (2 * u) * tq:(2 * u + 1) * tq], on[:, (2 * u + 1) * tq:(2 * u + 2) * tq]],
                                   axis=0)
            c0 = (g * KV_GROUP + 2 * u) * HEAD_DIM
            o_ref[0, :, c0:c0 + 2 * HEAD_DIM] = pair.T.astype(o_ref.dtype)


def _attn_layer(qt, qit, wit, kp, vt, kib, *, tq, tq_real, n_keys_real, pos0, n_sel):
    B, _, T = qt.shape
    n_keys = kp.shape[1]
    qblk = lambda r: pl.BlockSpec((1, r, tq), lambda b, t: (b, 0, t))
    kblk = lambda w: pl.BlockSpec((1, n_keys, w), lambda b, t: (b, 0, 0))
    return pl.pallas_call(
        functools.partial(_attn_kernel, tq=tq, tq_real=tq_real, n_keys_real=n_keys_real, pos0=pos0, n_sel=n_sel),
        grid=(B, T // tq),
        in_specs=[qblk(Q_W), qblk(QI_W), qblk(N_IDX_HEADS), kblk(KV_PAD_W),
                  pl.BlockSpec((1,) + vt.shape[1:], lambda b, t: (b, 0, 0, 0)), kblk(LANES)],
        out_specs=pl.BlockSpec((1, tq, Q_W), lambda b, t: (b, t, 0)),
        out_shape=jax.ShapeDtypeStruct((B, T, Q_W), jnp.bfloat16),
        scratch_shapes=[pltpu.VMEM((n_keys, tq), jnp.int32)] + [pltpu.VMEM((n_keys, tq), jnp.int16)] * 3 + [
                        pltpu.VMEM((n_keys, tq), jnp.float32),
                        pltpu.VMEM((n_keys, KV_GROUP * tq), jnp.float32),
                        pltpu.VMEM((n_keys, KV_GROUP * tq), jnp.float32),
                        pltpu.VMEM((LANES, KV_GROUP * tq), jnp.float32)],
        compiler_params=pltpu.CompilerParams(
            dimension_semantics=("arbitrary", "arbitrary"), vmem_limit_bytes=VMEM_LIMIT),
        name="dsa_attention",
    )(qt, qit, wit, kp, vt, kib)


def _pad_heads(a):
    lead = a.shape[:-1]
    a = a.reshape(*lead, N_KV_HEADS, HEAD_DIM)
    a = jnp.pad(a, [(0, 0)] * len(lead) + [(0, 0), (0, LANES - HEAD_DIM)])
    return a.reshape(*lead, KV_PAD_W)


def _pack_w_in(w_in):
    o = 0
    wq = w_in[:, o:o + Q_W] * (HEAD_DIM ** -0.5); o += Q_W
    wk = w_in[:, o:o + KV_W]; o += KV_W
    wv = w_in[:, o:o + KV_W]; o += KV_W
    wqi = w_in[:, o:o + QI_W] * (IDX_DIM ** -0.5); o += QI_W
    wki = w_in[:, o:o + IDX_DIM]; o += IDX_DIM
    wwi = w_in[:, o:o + N_IDX_HEADS]
    wkw = jnp.pad(jnp.concatenate([wki, wwi], axis=1), ((0, 0), (0, LANES - IDX_DIM - N_IDX_HEADS)))
    rowmajor = jnp.concatenate([wk, wv, wkw], axis=1)
    wwi_rows = jnp.pad(wwi, ((0, 0), (0, PACKED_ROWS - N_IDX_HEADS)))
    transposed = jnp.concatenate([wq, wqi, wv, wwi_rows], axis=1).T
    return rowmajor.astype(jnp.bfloat16), transposed.astype(jnp.bfloat16)


def _rope_tables(pos):
    inv_freq = 1.0 / (ROPE_THETA ** (jnp.arange(0, HEAD_DIM, 2, dtype=jnp.float32) / HEAD_DIM))
    ang = pos.astype(jnp.float32)[:, None] * inv_freq[None, :]
    c, s = jnp.cos(ang), jnp.sin(ang)
    z = jnp.zeros_like(c)
    n = pos.shape[0]
    zpad = jnp.zeros((n, LANES - HEAD_DIM), jnp.float32)
    wi_scale = jnp.full((n, N_IDX_HEADS), N_IDX_HEADS ** -0.5, jnp.float32)
    c_kw = jnp.concatenate([c, c, wi_scale, jnp.zeros((n, LANES - HEAD_DIM - N_IDX_HEADS), jnp.float32)], axis=1)
    tab = jnp.concatenate([
        jnp.concatenate([c, c, c, c], axis=1), c_kw,
        jnp.concatenate([-s, z, -s, z], axis=1), jnp.concatenate([-s, z, zpad], axis=1),
        jnp.concatenate([z, s, z, s], axis=1), jnp.concatenate([z, s, zpad], axis=1),
    ], axis=1)
    tabt = jnp.concatenate([c, s], axis=1).T
    return tab, tabt


def _round_up(n, m):
    return (n + m - 1) // m * m


def _value_blocks_t(v):
    B, n, _ = v.shape
    nb = n // KEY_BLOCK
    vt = jnp.transpose(v.reshape(B, nb, KEY_BLOCK, N_KV_HEADS, HEAD_DIM), (0, 1, 3, 4, 2))
    ones = jnp.ones((B, nb, N_KV_HEADS, 1, KEY_BLOCK), v.dtype)
    zeros = jnp.zeros((B, nb, N_KV_HEADS, LANES - HEAD_DIM - 1, KEY_BLOCK), v.dtype)
    return jnp.concatenate([vt, ones, zeros], axis=3).reshape(B, nb, KV_PAD_W, KEY_BLOCK)


def _trunk(x, pos0, pool_past, attn_past, norm_mix, norm_mlp, norm_final, pool_w, pool_scale,
           w_rowmajor, w_transposed, w_o, w_up, w_down):
    B, T, D = x.shape
    bf = jnp.bfloat16
    row = lambda v: v.reshape(1, D)

    if pool_past is None:
        past16 = jnp.zeros((B, POOL_STATE + 1, D), jnp.float32)
    else:
        past16 = jnp.pad(pool_past, ((0, 0), (1, 0), (0, 0)))
    tt = min(T, ROW_TILE)
    tm = min(B * T, ROW_TILE)
    x1, st = _pool_layer(x, past16, row(norm_mix[0]), pool_w[0].astype(bf), row(pool_scale[0]), pos0, tt)
    x2 = _mlp_layer(x1.reshape(B * T, D), row(norm_mlp[0]), w_up[0], w_down[0], tm, FF_CHUNK).reshape(B, T, D)
    pool_new = st[:, 1:][None]

    tab, tabt = _rope_tables(pos0 + jnp.arange(T, dtype=jnp.int32))
    kc, vc, kp, kw, kib, qt, qit, wit, vt = _proj_layer(
        x2, row(norm_mix[1]), w_rowmajor, w_transposed, tab, tabt, min(T, PROJ_TILE))
    k_new = kc.reshape(1, B, T, N_KV_HEADS, HEAD_DIM)
    v_new = vc.reshape(1, B, T, N_KV_HEADS, HEAD_DIM)
    ki_new = kw[:, :, :IDX_DIM][None]

    if attn_past is None:
        n_real = T
        kp_all, vt_all, kib_all = kp, vt, kib
    else:
        ck, cv, cki = attn_past
        P = ck.shape[1]
        n_real = P + T
        n_keys = _round_up(n_real, KEY_BLOCK)
        fill = lambda w: jnp.zeros((B, n_keys - n_real, w), bf)
        kp_all = jnp.concatenate([_pad_heads(ck.reshape(B, P, KV_W).astype(bf)), kp, fill(KV_PAD_W)], axis=1)
        kib_all = jnp.concatenate(
            [jnp.pad(cki.astype(bf), ((0, 0), (0, 0), (0, LANES - IDX_DIM))), kib, fill(LANES)], axis=1)
        vt_all = _value_blocks_t(
            jnp.concatenate([cv.reshape(B, P, KV_W).astype(bf), vc.astype(bf), fill(KV_W)], axis=1))
    tq = min(Q_TILE, _round_up(T, LANES))
    t_pad = _round_up(T, tq)
    padq = lambda a: jnp.pad(a, ((0, 0), (0, 0), (0, t_pad - T)))
    n_sel = min(TOPK_MAX, n_real // 4)
    o = _attn_layer(padq(qt), padq(qit), padq(wit), kp_all, vt_all, kib_all,
                    tq=tq, tq_real=min(T, tq), n_keys_real=n_real, pos0=pos0, n_sel=n_sel)
    o = o[:, :T].reshape(B * T, Q_W)

    y = _mlp_layer(x2.reshape(B * T, D), row(norm_mlp[1]), w_up[1], w_down[1], tm, FF_CHUNK,
                   attn=(o, w_o), final_g=row(norm_final)).reshape(B, T, D)
    return y, pool_new, k_new, v_new, ki_new


def kernel(x_prompt, x_sample, state_pool, cache_k, cache_v, cache_kidx, norm_mix, norm_mlp, norm_final,
           pool_w, pool_scale, attn_w_in, attn_w_o, mlp_w_up, mlp_w_down):
    bf = jnp.bfloat16
    shared = (norm_mix, norm_mlp, norm_final, pool_w, pool_scale, *_pack_w_in(attn_w_in[0]),
              attn_w_o[0].astype(bf), mlp_w_up.astype(bf), mlp_w_down.astype(bf))
    y_p, pool_p, k_p, v_p, ki_p = _trunk(x_prompt, 0, None, None, *shared)
    y_s, pool_s, k_s, v_s, ki_s = _trunk(
        x_sample, cache_k.shape[2], state_pool[0], (cache_k[0], cache_v[0], cache_kidx[0]), *shared)
    return (y_p, y_s, pool_p, pool_s, k_p, v_p, ki_p, k_s, v_s, ki_s)
```

```python
import functools
import math
import struct

import jax
import jax.numpy as jnp
from jax import lax
from jax.experimental import pallas as pl
from jax.experimental.pallas import tpu as pltpu

D_MODEL = 1024
CHUNK = 64
POOL_WINDOWS = (2, 4, 8, 16)
POOL_GROUP = D_MODEL // len(POOL_WINDOWS)
POOL_STATE = max(POOL_WINDOWS) - 1
N_HEADS = 16
HEAD_DIM = 64
N_KV_HEADS = 4
KV_GROUP = N_HEADS // N_KV_HEADS
N_IDX_HEADS = 8
IDX_DIM = 64
TOPK_MAX = 256
ROPE_THETA = 10000.0
D_FF = 4 * D_MODEL
EPS = 1e-6
NEG = -1e30
Q_W = N_HEADS * HEAD_DIM
KV_W = N_KV_HEADS * HEAD_DIM
QI_W = N_IDX_HEADS * IDX_DIM
LOG2E = math.log2(math.e)

SUBLANES = 8
PACKED_ROWS = 2 * SUBLANES
LANES = 128
KV_PAD_W = N_KV_HEADS * LANES
V_SLOT = HEAD_DIM + PACKED_ROWS
V_ROWS = N_KV_HEADS * V_SLOT
KEY_BLOCK = 256
Q_TILE = 256
ROW_TILE = 1024
PROJ_TILE = 1024
FF_CHUNK = 512

OFF_KC = 0
OFF_VC = OFF_KC + KV_W
OFF_KW = OFF_VC + KV_W
PROJ_ROWMAJOR_W = OFF_KW + LANES
ROW_Q = 0
ROW_QI = ROW_Q + Q_W
ROW_V = ROW_QI + QI_W
ROW_WI = ROW_V + KV_W
PROJ_TRANSPOSED_ROWS = ROW_WI + PACKED_ROWS

VMEM_LIMIT = 56 * 1024 * 1024

_NT = (((1,), (1,)), ((), ()))


def _f32_order_key(x):
    b = struct.unpack("<i", struct.pack("<f", x))[0]
    return b ^ ((b >> 31) & 0x7FFFFFFF)


def _signed16(v):
    return v - (1 << 16) if v >= (1 << 15) else v


NEG_KEY = _f32_order_key(NEG)
NEG_KEY_HI = NEG_KEY >> 16
NEG_KEY_LO = _signed16((NEG_KEY & 0xFFFF) ^ 0x8000)


def _rms(x, g):
    return x * lax.rsqrt(jnp.mean(x * x, axis=-1, keepdims=True) + EPS) * g


def _const_spec(shape):
    n = len(shape)
    return pl.BlockSpec(shape, lambda *_: (0,) * n, pipeline_mode=pl.Buffered(1))


def _tree(op, parts):
    parts = list(parts)
    while len(parts) > 1:
        parts = [op(parts[i], parts[i + 1]) if i + 1 < len(parts) else parts[i] for i in range(0, len(parts), 2)]
    return parts[0]


def _fold_rows(op, x, rows=SUBLANES):
    return _tree(op, [x[r:r + rows] for r in range(0, x.shape[0], rows)])


def _pool_kernel(x_ref, past_ref, g_ref, w_ref, scale_ref, x1_ref, st_ref, hp_ref, sa_ref, sb_ref, *, tt, pos0):
    t = pl.program_id(1)
    halo = POOL_STATE + 1
    lead = SUBLANES
    top = lead + halo

    @pl.when(t == 0)
    def _():
        zeros = jnp.zeros((lead, hp_ref.shape[1]), jnp.float32)
        hp_ref[0:lead, :] = zeros
        sa_ref[0:lead, :] = zeros
        sb_ref[0:lead, :] = zeros
        hp_ref[lead:top, :] = past_ref[0]

    x = x_ref[0]
    h = _rms(x, g_ref[...])
    hp_ref[top:top + tt, :] = h

    n = halo + tt
    g1, g2, g3 = POOL_GROUP, 2 * POOL_GROUP, 3 * POOL_GROUP
    sa_ref[lead:lead + n, :] = hp_ref[lead:lead + n, :] + hp_ref[lead - 1:lead - 1 + n, :]
    sb_ref[lead:lead + n, g1:] = sa_ref[lead:lead + n, g1:] + sa_ref[lead - 2:lead - 2 + n, g1:]
    sa_ref[lead:lead + n, g2:] = sb_ref[lead:lead + n, g2:] + sb_ref[lead - 4:lead - 4 + n, g2:]
    window_sums = (
        sa_ref[top:top + tt, 0:g1],
        sb_ref[top:top + tt, g1:g2],
        sa_ref[top:top + tt, g2:g3],
        sa_ref[top:top + tt, g3:] + sa_ref[top - 8:top - 8 + tt, g3:],
    )
    pos = pos0 + t * tt + lax.broadcasted_iota(jnp.int32, (tt, 1), 0)
    for g, win in enumerate(POOL_WINDOWS):
        c0, c1 = g * POOL_GROUP, (g + 1) * POOL_GROUP
        cnt = jnp.minimum(pos + 1, win).astype(jnp.float32)
        diff = window_sums[g] / cnt - h[:, c0:c1]
        y = jnp.dot(diff.astype(jnp.bfloat16), w_ref[g], preferred_element_type=jnp.float32)
        x1_ref[0, :, c0:c1] = x[:, c0:c1] + y * scale_ref[:, c0:c1]
    tail = hp_ref[lead + tt:top + tt, :]
    st_ref[0] = tail
    hp_ref[lead:top, :] = tail


def _pool_layer(x, past16, g, w, scale, pos0, tt):
    B, T, D = x.shape
    halo = POOL_STATE + 1
    plane = pltpu.VMEM((SUBLANES + halo + tt, D), jnp.float32)
    return pl.pallas_call(
        functools.partial(_pool_kernel, tt=tt, pos0=pos0),
        grid=(B, T // tt),
        in_specs=[
            pl.BlockSpec((1, tt, D), lambda b, t: (b, t, 0)),
            pl.BlockSpec((1, halo, D), lambda b, t: (b, 0, 0)),
            _const_spec((1, D)),
            _const_spec(w.shape),
            _const_spec((1, D)),
        ],
        out_specs=[
            pl.BlockSpec((1, tt, D), lambda b, t: (b, t, 0)),
            pl.BlockSpec((1, halo, D), lambda b, t: (b, 0, 0)),
        ],
        out_shape=[
            jax.ShapeDtypeStruct((B, T, D), jnp.float32),
            jax.ShapeDtypeStruct((B, halo, D), jnp.float32),
        ],
        scratch_shapes=[plane, plane, plane],
        compiler_params=pltpu.CompilerParams(
            dimension_semantics=("arbitrary", "arbitrary"), vmem_limit_bytes=VMEM_LIMIT),
        name="pool_mixer",
    )(x, past16, g, w, scale)


def _mlp_body(x, g_ref, up_ref, down_ref, fc):
    h = _rms(x, g_ref[...]).astype(jnp.bfloat16)
    acc = x
    for c in range(D_FF // fc):
        u = jnp.dot(h, up_ref[:, c * fc:(c + 1) * fc], preferred_element_type=jnp.float32)
        u = jnp.maximum(u, 0.0)
        a = (u * u).astype(jnp.bfloat16)
        acc = acc + jnp.dot(a, down_ref[c * fc:(c + 1) * fc, :], preferred_element_type=jnp.float32)
    return acc


def _mlp_kernel(*refs, fc, with_attn, with_final):
    refs = list(refs)
    x_ref = refs.pop(0)
    if with_attn:
        o_ref = refs.pop(0)
        wo_ref = refs.pop(0)
    g_ref = refs.pop(0)
    up_ref = refs.pop(0)
    down_ref = refs.pop(0)
    if with_final:
        gf_ref = refs.pop(0)
    out_ref = refs.pop(0)

    x = x_ref[...]
    if with_attn:
        x = x + jnp.dot(o_ref[...], wo_ref[...], preferred_element_type=jnp.float32)
    acc = _mlp_body(x, g_ref, up_ref, down_ref, fc)
    if with_final:
        acc = _rms(acc, gf_ref[...])
    out_ref[...] = acc


def _mlp_layer(x, g, up, down, tm, fc, attn=None, final_g=None):
    N, D = x.shape
    row = lambda i: (i, 0)
    args, specs = [x], [pl.BlockSpec((tm, D), row)]
    if attn is not None:
        o, wo = attn
        args += [o, wo]
        specs += [pl.BlockSpec((tm, Q_W), row), _const_spec(wo.shape)]
    args += [g, up, down]
    specs += [_const_spec((1, D)), _const_spec(up.shape), _const_spec(down.shape)]
    if final_g is not None:
        args.append(final_g)
        specs.append(_const_spec((1, D)))
    return pl.pallas_call(
        functools.partial(_mlp_kernel, fc=fc, with_attn=attn is not None, with_final=final_g is not None),
        grid=(N // tm,),
        in_specs=specs,
        out_specs=pl.BlockSpec((tm, D), row),
        out_shape=jax.ShapeDtypeStruct((N, D), jnp.float32),
        compiler_params=pltpu.CompilerParams(
            dimension_semantics=("arbitrary",), vmem_limit_bytes=VMEM_LIMIT),
        name="mlp_attn_out" if attn is not None else "mlp",
    )(*args)


def _rope_block(xb, tab_ref, special):
    o = LANES if special else 0
    c = tab_ref[:, o:o + LANES]
    s1 = tab_ref[:, 2 * LANES + o:3 * LANES + o]
    s2 = tab_ref[:, 4 * LANES + o:5 * LANES + o]
    return xb * c + pltpu.roll(xb, LANES - HEAD_DIM // 2, 1) * s1 + pltpu.roll(xb, HEAD_DIM // 2, 1) * s2


def _rope_rows(p, c, s):
    half = HEAD_DIM // 2
    r = p.reshape(p.shape[0] // HEAD_DIM, 2, half, p.shape[1])
    x1, x2 = r[:, 0], r[:, 1]
    out = jnp.stack([x1 * c - x2 * s, x2 * c + x1 * s], axis=1)
    return out.reshape(p.shape)


def _proj_kernel(x_ref, g_ref, w_ref, wt_ref, tab_ref, tabt_ref,
                 kc_ref, vc_ref, kp_ref, kw_ref, kib_ref, qt_ref, qit_ref, wit_ref, vt_ref):
    h = _rms(x_ref[0], g_ref[...]).astype(jnp.bfloat16)

    def proj(off, width):
        return jnp.dot(h, w_ref[:, off:off + width], preferred_element_type=jnp.float32)

    def rope(p, special=False):
        return jnp.concatenate(
            [_rope_block(p[:, j:j + LANES], tab_ref, special) for j in range(0, p.shape[1], LANES)], axis=-1)

    kc = rope(proj(OFF_KC, KV_W))
    kc_ref[0] = kc
    vc_ref[0] = proj(OFF_VC, KV_W)
    gap = jnp.zeros((kc.shape[0], LANES - HEAD_DIM), jnp.float32)
    kp = jnp.concatenate(
        [piece for g in range(N_KV_HEADS) for piece in (kc[:, g * HEAD_DIM:(g + 1) * HEAD_DIM], gap)], axis=1)
    kp_ref[0] = kp.astype(kp_ref.dtype)
    kw = rope(proj(OFF_KW, LANES), special=True)
    kw_ref[0] = kw
    kib_ref[0] = kw.astype(kib_ref.dtype)

    def proj_t(row, n):
        return lax.dot_general(wt_ref[row:row + n, :], h, _NT, preferred_element_type=jnp.float32)

    half = HEAD_DIM // 2
    c, s = tabt_ref[0:half, :], tabt_ref[half:HEAD_DIM, :]
    qt_ref[0] = _rope_rows(proj_t(ROW_Q, Q_W), c * LOG2E, s * LOG2E).astype(qt_ref.dtype)
    qit_ref[0] = _rope_rows(proj_t(ROW_QI, QI_W), c, s).astype(qit_ref.dtype)
    vw = proj_t(ROW_V, KV_W + PACKED_ROWS)
    wit_ref[0] = vw[KV_W:KV_W + N_IDX_HEADS, :] * (N_IDX_HEADS ** -0.5)
    n = vw.shape[1]
    tail = jnp.where(lax.broadcasted_iota(jnp.int32, (V_SLOT - HEAD_DIM, n), 0) == 0, 1.0, 0.0)
    vt = jnp.concatenate(
        [piece for g in range(N_KV_HEADS) for piece in (vw[g * HEAD_DIM:(g + 1) * HEAD_DIM], tail)], axis=0)
    kbw = vt_ref.shape[3]
    for c in range(vt_ref.shape[1]):
        vt_ref[0, c] = vt[:, c * kbw:(c + 1) * kbw].astype(vt_ref.dtype)


def _proj_layer(x, g, w_rowmajor, w_transposed, tab, tabt, tm):
    B, T, D = x.shape
    blk = lambda w: pl.BlockSpec((1, tm, w), lambda b, t: (b, t, 0))
    blk_t = lambda r: pl.BlockSpec((1, r, tm), lambda b, t: (b, 0, t))
    bf, f32 = jnp.bfloat16, jnp.float32
    sds = jax.ShapeDtypeStruct
    kbw = min(tm, KEY_BLOCK)
    return pl.pallas_call(
        _proj_kernel,
        grid=(B, T // tm),
        in_specs=[
            blk(D),
            _const_spec((1, D)),
            _const_spec(w_rowmajor.shape),
            _const_spec(w_transposed.shape),
            pl.BlockSpec((tm, tab.shape[1]), lambda b, t: (t, 0)),
            pl.BlockSpec((HEAD_DIM, tm), lambda b, t: (0, t)),
        ],
        out_specs=[blk(KV_W), blk(KV_W), blk(KV_PAD_W), blk(LANES), blk(LANES),
                   blk_t(Q_W), blk_t(QI_W), blk_t(N_IDX_HEADS),
                   pl.BlockSpec((1, tm // kbw, V_ROWS, kbw), lambda b, t: (b, t, 0, 0))],
        out_shape=[sds((B, T, KV_W), f32), sds((B, T, KV_W), f32), sds((B, T, KV_PAD_W), bf),
                   sds((B, T, LANES), f32), sds((B, T, LANES), bf),
                   sds((B, Q_W, T), bf), sds((B, QI_W, T), bf), sds((B, N_IDX_HEADS, T), f32),
                   sds((B, T // kbw, V_ROWS, kbw), bf)],
        compiler_params=pltpu.CompilerParams(
            dimension_semantics=("arbitrary", "arbitrary"), vmem_limit_bytes=VMEM_LIMIT),
        name="attn_in_proj",
    )(x, g, w_rowmajor, w_transposed, tab, tabt)


def _attn_kernel(qt_ref, qit_ref, wit_ref, kp_ref, vt_ref, kib_ref, o_ref,
                 key_ref, hi_ref, lo_ref, lo2_ref, f_ref, s0_ref, s1_ref, oacc_ref, *,
                 tq, tq_real, n_keys_real, pos0, n_sel):
    kb = KEY_BLOCK
    t = pl.program_id(1)
    lane_q = lax.broadcasted_iota(jnp.int32, (1, tq), 1)
    q_limit = ((pos0 + t * tq + lane_q) // CHUNK + 1) * CHUNK
    qvalid = lane_q < tq_real
    n_reach = jnp.minimum(((pos0 + (t + 1) * tq - 1) // CHUNK + 1) * CHUNK, n_keys_real)
    nkb = (n_reach + kb - 1) // kb
    n_unvisited = jnp.maximum(n_keys_real - nkb * kb, 0)
    key_row = lax.broadcasted_iota(jnp.int32, (kb, tq), 0)

    def block_start(j):
        return pl.multiple_of(j * kb, kb)

    def for_blocks(body, carry, tiers=(2, 1)):
        start = 0
        for n in tiers:
            def trip(i, c, n=n, start=start):
                for u in range(n):
                    c = body(start + n * i + u, c)
                return c
            trips = (nkb - start) // n
            carry = lax.fori_loop(0, trips, trip, carry)
            start = start + trips * n
        return carry

    def admissible(k0):
        return key_row < q_limit - k0

    wi = wit_ref[0]

    def score_block(j, carry):
        k0 = block_start(j)
        kx = kib_ref[0, pl.ds(k0, kb), 0:IDX_DIM]
        acc = jnp.zeros((kb, tq), jnp.float32)
        for hp in range(N_IDX_HEADS // 2):
            qi2 = jnp.concatenate(
                [qit_ref[0, (2 * hp + u) * IDX_DIM:(2 * hp + u + 1) * IDX_DIM, :] for u in range(2)], axis=1)
            lg = jnp.dot(kx, qi2, preferred_element_type=jnp.float32)
            for u in range(2):
                hh = 2 * hp + u
                acc = acc + jnp.maximum(lg[:, u * tq:(u + 1) * tq], 0.0) * wi[hh:hh + 1, :]
        sc = jnp.where(admissible(k0), acc, NEG)
        if n_keys_real % kb:
            sc = jnp.where(key_row < n_keys_real - k0, sc, -jnp.inf)
        bits = lax.bitcast_convert_type(sc, jnp.int32)
        key = bits ^ ((bits >> 31) & jnp.int32(0x7FFFFFFF))
        key_ref[pl.ds(k0, kb), :] = key
        hi_ref[pl.ds(k0, kb), :] = (key >> 16).astype(jnp.int16)
        lo_ref[pl.ds(k0, kb), :] = (key ^ 0x8000).astype(jnp.int16)
        return carry

    for_blocks(score_block, 0, tiers=(4, 2, 1))

    def count(pred):
        def body(j, acc):
            blk = key_ref[pl.ds(block_start(j), kb), :]
            return acc + _fold_rows(jnp.add, jnp.where(pred(blk, j), 1, 0).astype(jnp.int32))
        acc = for_blocks(body, jnp.zeros((SUBLANES, tq), jnp.int32))
        return jnp.sum(acc, axis=0, keepdims=True)

    def count16(ref, pred):
        def body(j, acc):
            hit = jnp.where(pred(ref[pl.ds(block_start(j), kb), :]), jnp.int16(1), jnp.int16(0))
            return acc + _fold_rows(jnp.add, hit, rows=PACKED_ROWS)
        acc = for_blocks(body, jnp.zeros((PACKED_ROWS, tq), jnp.int16))
        return jnp.sum(acc.astype(jnp.int32), axis=0, keepdims=True)

    def as_plane(v):
        tile = jnp.broadcast_to(v, (PACKED_ROWS, tq)).astype(jnp.int16)
        return jnp.concatenate([tile] * (kb // PACKED_ROWS), axis=0)

    def radix16(ref, n_wanted, n_all, unvisited_ge):
        def step(i, carry):
            thr, n_at = carry
            cand = thr + jnp.left_shift(jnp.int32(1), 15 - i)
            plane = as_plane(cand)
            n_ge = count16(ref, lambda blk: blk >= plane) + unvisited_ge(cand)
            take = n_ge >= n_wanted
            return jnp.where(take, cand, thr), jnp.where(take, n_ge, n_at)
        return lax.fori_loop(0, 16, step, (jnp.full((1, tq), -2 ** 15, jnp.int32), n_all))

    n_all = jnp.broadcast_to(nkb * kb + n_unvisited, (1, tq))
    thr_hi, n_ge_hi = radix16(hi_ref, n_sel, n_all, lambda cand: jnp.where(cand <= NEG_KEY_HI, n_unvisited, 0))
    hi_plane = as_plane(thr_hi)
    n_gt_hi = count16(hi_ref, lambda blk: blk > hi_plane) + jnp.where(thr_hi < NEG_KEY_HI, n_unvisited, 0)

    def park_block(j, carry):
        rows = pl.ds(block_start(j), kb)
        lo2_ref[rows, :] = jnp.where(hi_ref[rows, :] == hi_plane, lo_ref[rows, :], jnp.int16(-2 ** 15))
        return carry

    for_blocks(park_block, 0)
    neg_in_bucket = thr_hi == NEG_KEY_HI
    thr_lo, n_ge_lo = radix16(lo2_ref, n_sel - n_gt_hi, n_ge_hi - n_gt_hi,
                              lambda cand: jnp.where(neg_in_bucket & (cand <= NEG_KEY_LO), n_unvisited, 0))
    thr = thr_hi * 65536 + (thr_lo + 2 ** 15)
    lo_plane = as_plane(thr_lo)
    n_gt_lo = (count16(lo2_ref, lambda blk: blk > lo_plane)
               + jnp.where(neg_in_bucket & (thr_lo < NEG_KEY_LO), n_unvisited, 0))
    n_gt = n_gt_hi + n_gt_lo
    n_eq = n_ge_lo - n_gt_lo
    n_eq_adm = count(lambda blk, j: (blk == thr) & admissible(block_start(j)))
    need = n_sel - n_gt
    tie = (n_eq > need) & (n_eq_adm > 0) & qvalid
    any_tie = jnp.max(tie.astype(jnp.int32)) > 0

    def write_mask(exact_ties):
        def body(j, run):
            k0 = block_start(j)
            kblk = key_ref[pl.ds(k0, kb), :]
            if exact_ties:
                r = lax.broadcasted_iota(jnp.int32, (kb, kb), 0)
                c = lax.broadcasted_iota(jnp.int32, (kb, kb), 1)
                tri = jnp.where(r >= c, 1.0, 0.0).astype(jnp.bfloat16)
                eqb = kblk == thr
                eqf = jnp.where(eqb, 1.0, 0.0)
                rank = jnp.dot(tri, eqf.astype(jnp.bfloat16), preferred_element_type=jnp.float32) + run
                run = run + jnp.sum(eqf, axis=0, keepdims=True)
                sel = (kblk > thr) | (eqb & (rank <= need.astype(jnp.float32)))
            else:
                sel = kblk >= thr
            f_ref[pl.ds(k0, kb), :] = jnp.where(sel, jnp.where(admissible(k0), jnp.inf, NEG), -jnp.inf)
            return run

        for_blocks(body, jnp.zeros((1, tq), jnp.float32))

    @pl.when(any_tie)
    def _():
        write_mask(True)

    @pl.when(jnp.logical_not(any_tie))
    def _():
        write_mask(False)

    gw = KV_GROUP * tq
    s_refs = (s0_ref, s1_ref)

    def logits_block(g, j):
        k0 = block_start(j)
        q4 = jnp.concatenate(
            [qt_ref[0, (g * KV_GROUP + r) * HEAD_DIM:(g * KV_GROUP + r + 1) * HEAD_DIM, :] for r in range(KV_GROUP)],
            axis=1)
        kblk = kp_ref[0, pl.ds(k0, kb), g * LANES:g * LANES + HEAD_DIM]
        s = jnp.dot(kblk, q4, preferred_element_type=jnp.float32)
        f = f_ref[pl.ds(k0, kb), :]
        s = jnp.minimum(s, jnp.concatenate([f] * KV_GROUP, axis=1))
        s_refs[g % 2][pl.ds(k0, kb), :] = s
        return _fold_rows(jnp.maximum, s)

    def values_block(g, j, m):
        p = jnp.exp2(s_refs[g % 2][pl.ds(block_start(j), kb), :] - m).astype(jnp.bfloat16)
        oacc_ref[...] += jnp.dot(vt_ref[0, j, g * V_SLOT:(g + 1) * V_SLOT, :], p,
                                 preferred_element_type=jnp.float32)

    m_init = jnp.full((SUBLANES, gw), -jnp.inf, jnp.float32)
    m_acc = for_blocks(lambda j, a: jnp.maximum(a, logits_block(0, j)), m_init, tiers=(4, 2, 1))
    for g in range(N_KV_HEADS):
        m = jnp.max(m_acc, axis=0, keepdims=True)
        oacc_ref[...] = jnp.zeros(oacc_ref.shape, jnp.float32)

        def fused(j, a, g=g, m=m):
            if g + 1 < N_KV_HEADS:
                a = jnp.maximum(a, logits_block(g + 1, j))
            values_block(g, j, m)
            return a

        m_acc = for_blocks(fused, m_init, tiers=(4, 2, 1))
        oacc = oacc_ref[...]
        on = oacc[0:HEAD_DIM] / oacc[HEAD_DIM:HEAD_DIM + 1]
        for u in range(KV_GROUP // 2):
            pair = jnp.concatenate([on[:, (2 * u) * tq:(2 * u + 1) * tq], on[:, (2 * u + 1) * tq:(2 * u + 2) * tq]],
                                   axis=0)
            c0 = (g * KV_GROUP + 2 * u) * HEAD_DIM
            o_ref[0, :, c0:c0 + 2 * HEAD_DIM] = pair.T.astype(o_ref.dtype)


def _attn_layer(qt, qit, wit, kp, vt, kib, *, tq, tq_real, n_keys_real, pos0, n_sel):
    B, _, T = qt.shape
    n_keys = kp.shape[1]
    qblk = lambda r: pl.BlockSpec((1, r, tq), lambda b, t: (b, 0, t))
    kblk = lambda w: pl.BlockSpec((1, n_keys, w), lambda b, t: (b, 0, 0))
    return pl.pallas_call(
        functools.partial(_attn_kernel, tq=tq, tq_real=tq_real, n_keys_real=n_keys_real, pos0=pos0, n_sel=n_sel),
        grid=(B, T // tq),
        in_specs=[qblk(Q_W), qblk(QI_W), qblk(N_IDX_HEADS), kblk(KV_PAD_W),
                  pl.BlockSpec((1,) + vt.shape[1:], lambda b, t: (b, 0, 0, 0)), kblk(LANES)],
        out_specs=pl.BlockSpec((1, tq, Q_W), lambda b, t: (b, t, 0)),
        out_shape=jax.ShapeDtypeStruct((B, T, Q_W), jnp.bfloat16),
        scratch_shapes=[pltpu.VMEM((n_keys, tq), jnp.int32)] + [pltpu.VMEM((n_keys, tq), jnp.int16)] * 3 + [
                        pltpu.VMEM((n_keys, tq), jnp.float32),
                        pltpu.VMEM((n_keys, KV_GROUP * tq), jnp.float32),
                        pltpu.VMEM((n_keys, KV_GROUP * tq), jnp.float32),
                        pltpu.VMEM((V_SLOT, KV_GROUP * tq), jnp.float32)],
        compiler_params=pltpu.CompilerParams(
            dimension_semantics=("arbitrary", "arbitrary"), vmem_limit_bytes=VMEM_LIMIT),
        name="dsa_attention",
    )(qt, qit, wit, kp, vt, kib)


def _pad_heads(a):
    lead = a.shape[:-1]
    a = a.reshape(*lead, N_KV_HEADS, HEAD_DIM)
    a = jnp.pad(a, [(0, 0)] * len(lead) + [(0, 0), (0, LANES - HEAD_DIM)])
    return a.reshape(*lead, KV_PAD_W)


def _pack_w_in(w_in):
    o = 0
    wq = w_in[:, o:o + Q_W] * (HEAD_DIM ** -0.5); o += Q_W
    wk = w_in[:, o:o + KV_W]; o += KV_W
    wv = w_in[:, o:o + KV_W]; o += KV_W
    wqi = w_in[:, o:o + QI_W] * (IDX_DIM ** -0.5); o += QI_W
    wki = w_in[:, o:o + IDX_DIM]; o += IDX_DIM
    wwi = w_in[:, o:o + N_IDX_HEADS]
    wkw = jnp.pad(jnp.concatenate([wki, wwi], axis=1), ((0, 0), (0, LANES - IDX_DIM - N_IDX_HEADS)))
    rowmajor = jnp.concatenate([wk, wv, wkw], axis=1)
    wwi_rows = jnp.pad(wwi, ((0, 0), (0, PACKED_ROWS - N_IDX_HEADS)))
    transposed = jnp.concatenate([wq, wqi, wv, wwi_rows], axis=1).T
    return rowmajor.astype(jnp.bfloat16), transposed.astype(jnp.bfloat16)


def _rope_tables(pos):
    inv_freq = 1.0 / (ROPE_THETA ** (jnp.arange(0, HEAD_DIM, 2, dtype=jnp.float32) / HEAD_DIM))
    ang = pos.astype(jnp.float32)[:, None] * inv_freq[None, :]
    c, s = jnp.cos(ang), jnp.sin(ang)
    z = jnp.zeros_like(c)
    n = pos.shape[0]
    zpad = jnp.zeros((n, LANES - HEAD_DIM), jnp.float32)
    wi_scale = jnp.full((n, N_IDX_HEADS), N_IDX_HEADS ** -0.5, jnp.float32)
    c_kw = jnp.concatenate([c, c, wi_scale, jnp.zeros((n, LANES - HEAD_DIM - N_IDX_HEADS), jnp.float32)], axis=1)
    tab = jnp.concatenate([
        jnp.concatenate([c, c, c, c], axis=1), c_kw,
        jnp.concatenate([-s, z, -s, z], axis=1), jnp.concatenate([-s, z, zpad], axis=1),
        jnp.concatenate([z, s, z, s], axis=1), jnp.concatenate([z, s, zpad], axis=1),
    ], axis=1)
    tabt = jnp.concatenate([c, s], axis=1).T
    return tab, tabt


def _round_up(n, m):
    return (n + m - 1) // m * m


def _value_blocks_t(v):
    B, n, _ = v.shape
    nb = n // KEY_BLOCK
    vt = jnp.transpose(v.reshape(B, nb, KEY_BLOCK, N_KV_HEADS, HEAD_DIM), (0, 1, 3, 4, 2))
    ones = jnp.ones((B, nb, N_KV_HEADS, 1, KEY_BLOCK), v.dtype)
    zeros = jnp.zeros((B, nb, N_KV_HEADS, V_SLOT - HEAD_DIM - 1, KEY_BLOCK), v.dtype)
    return jnp.concatenate([vt, ones, zeros], axis=3).reshape(B, nb, V_ROWS, KEY_BLOCK)


def _trunk(x, pos0, pool_past, attn_past, norm_mix, norm_mlp, norm_final, pool_w, pool_scale,
           w_rowmajor, w_transposed, w_o, w_up, w_down):
    B, T, D = x.shape
    bf = jnp.bfloat16
    row = lambda v: v.reshape(1, D)

    if pool_past is None:
        past16 = jnp.zeros((B, POOL_STATE + 1, D), jnp.float32)
    else:
        past16 = jnp.pad(pool_past, ((0, 0), (1, 0), (0, 0)))
    tt = min(T, ROW_TILE)
    tm = min(B * T, ROW_TILE)
    x1, st = _pool_layer(x, past16, row(norm_mix[0]), pool_w[0].astype(bf), row(pool_scale[0]), pos0, tt)
    x2 = _mlp_layer(x1.reshape(B * T, D), row(norm_mlp[0]), w_up[0], w_down[0], tm, FF_CHUNK).reshape(B, T, D)
    pool_new = st[:, 1:][None]

    tab, tabt = _rope_tables(pos0 + jnp.arange(T, dtype=jnp.int32))
    kc, vc, kp, kw, kib, qt, qit, wit, vt = _proj_layer(
        x2, row(norm_mix[1]), w_rowmajor, w_transposed, tab, tabt, min(T, PROJ_TILE))
    k_new = kc.reshape(1, B, T, N_KV_HEADS, HEAD_DIM)
    v_new = vc.reshape(1, B, T, N_KV_HEADS, HEAD_DIM)
    ki_new = kw[:, :, :IDX_DIM][None]

    if attn_past is None:
        n_real = T
        kp_all, vt_all, kib_all = kp, vt, kib
    else:
        ck, cv, cki = attn_past
        P = ck.shape[1]
        n_real = P + T
        n_keys = _round_up(n_real, KEY_BLOCK)
        fill = lambda w: jnp.zeros((B, n_keys - n_real, w), bf)
        kp_all = jnp.concatenate([_pad_heads(ck.reshape(B, P, KV_W).astype(bf)), kp, fill(KV_PAD_W)], axis=1)
        kib_all = jnp.concatenate(
            [jnp.pad(cki.astype(bf), ((0, 0), (0, 0), (0, LANES - IDX_DIM))), kib, fill(LANES)], axis=1)
        vt_all = _value_blocks_t(
            jnp.concatenate([cv.reshape(B, P, KV_W).astype(bf), vc.astype(bf), fill(KV_W)], axis=1))
    tq = min(Q_TILE, _round_up(T, LANES))
    t_pad = _round_up(T, tq)
    padq = lambda a: jnp.pad(a, ((0, 0), (0, 0), (0, t_pad - T)))
    n_sel = min(TOPK_MAX, n_real // 4)
    o = _attn_layer(padq(qt), padq(qit), padq(wit), kp_all, vt_all, kib_all,
                    tq=tq, tq_real=min(T, tq), n_keys_real=n_real, pos0=pos0, n_sel=n_sel)
    o = o[:, :T].reshape(B * T, Q_W)

    y = _mlp_layer(x2.reshape(B * T, D), row(norm_mlp[1]), w_up[1], w_down[1], tm, FF_CHUNK,
                   attn=(o, w_o), final_g=row(norm_final)).reshape(B, T, D)
    return y, pool_new, k_new, v_new, ki_new


def kernel(x_prompt, x_sample, state_pool, cache_k, cache_v, cache_kidx, norm_mix, norm_mlp, norm_final,
           pool_w, pool_scale, attn_w_in, attn_w_o, mlp_w_up, mlp_w_down):
    bf = jnp.bfloat16
    shared = (norm_mix, norm_mlp, norm_final, pool_w, pool_scale, *_pack_w_in(attn_w_in[0]),
              attn_w_o[0].astype(bf), mlp_w_up.astype(bf), mlp_w_down.astype(bf))
    y_p, pool_p, k_p, v_p, ki_p = _trunk(x_prompt, 0, None, None, *shared)
    y_s, pool_s, k_s, v_s, ki_s = _trunk(
        x_sample, cache_k.shape[2], state_pool[0], (cache_k[0], cache_v[0], cache_kidx[0]), *shared)
    return (y_p, y_s, pool_p, pool_s, k_p, v_p, ki_p, k_s, v_s, ki_s)
```

```python
import functools
import math
import struct

import jax
import jax.numpy as jnp
from jax import lax
from jax.experimental import pallas as pl
from jax.experimental.pallas import tpu as pltpu

D_MODEL = 1024
CHUNK = 64
POOL_WINDOWS = (2, 4, 8, 16)
POOL_GROUP = D_MODEL // len(POOL_WINDOWS)
POOL_STATE = max(POOL_WINDOWS) - 1
N_HEADS = 16
HEAD_DIM = 64
N_KV_HEADS = 4
KV_GROUP = N_HEADS // N_KV_HEADS
N_IDX_HEADS = 8
IDX_DIM = 64
TOPK_MAX = 256
ROPE_THETA = 10000.0
D_FF = 4 * D_MODEL
EPS = 1e-6
NEG = -1e30
Q_W = N_HEADS * HEAD_DIM
KV_W = N_KV_HEADS * HEAD_DIM
QI_W = N_IDX_HEADS * IDX_DIM
LOG2E = math.log2(math.e)

SUBLANES = 8
PACKED_ROWS = 2 * SUBLANES
LANES = 128
KV_PAD_W = N_KV_HEADS * LANES
V_SLOT = HEAD_DIM + PACKED_ROWS
V_ROWS = N_KV_HEADS * V_SLOT
KEY_BLOCK = 256
Q_TILE = 256
ROW_TILE = 1024
PROJ_TILE = 1024
FF_CHUNK = 512

OFF_KC = 0
OFF_VC = OFF_KC + KV_W
OFF_KW = OFF_VC + KV_W
PROJ_ROWMAJOR_W = OFF_KW + LANES
ROW_Q = 0
ROW_QI = ROW_Q + Q_W
ROW_V = ROW_QI + QI_W
ROW_WI = ROW_V + KV_W
PROJ_TRANSPOSED_ROWS = ROW_WI + PACKED_ROWS

VMEM_LIMIT = 56 * 1024 * 1024

_NT = (((1,), (1,)), ((), ()))


def _f32_order_key(x):
    b = struct.unpack("<i", struct.pack("<f", x))[0]
    return b ^ ((b >> 31) & 0x7FFFFFFF)


def _signed16(v):
    return v - (1 << 16) if v >= (1 << 15) else v


NEG_KEY = _f32_order_key(NEG)
NEG_KEY_HI = NEG_KEY >> 16
NEG_KEY_LO = _signed16((NEG_KEY & 0xFFFF) ^ 0x8000)


def _rms(x, g):
    return x * lax.rsqrt(jnp.mean(x * x, axis=-1, keepdims=True) + EPS) * g


def _const_spec(shape):
    n = len(shape)
    return pl.BlockSpec(shape, lambda *_: (0,) * n, pipeline_mode=pl.Buffered(1))


def _tree(op, parts):
    parts = list(parts)
    while len(parts) > 1:
        parts = [op(parts[i], parts[i + 1]) if i + 1 < len(parts) else parts[i] for i in range(0, len(parts), 2)]
    return parts[0]


def _fold_rows(op, x, rows=SUBLANES):
    return _tree(op, [x[r:r + rows] for r in range(0, x.shape[0], rows)])


def _pool_kernel(x_ref, past_ref, g_ref, w_ref, scale_ref, x1_ref, st_ref, hp_ref, sa_ref, sb_ref, *, tt, pos0):
    t = pl.program_id(1)
    halo = POOL_STATE + 1
    lead = SUBLANES
    top = lead + halo

    @pl.when(t == 0)
    def _():
        zeros = jnp.zeros((lead, hp_ref.shape[1]), jnp.float32)
        hp_ref[0:lead, :] = zeros
        sa_ref[0:lead, :] = zeros
        sb_ref[0:lead, :] = zeros
        hp_ref[lead:top, :] = past_ref[0]

    x = x_ref[0]
    h = _rms(x, g_ref[...])
    hp_ref[top:top + tt, :] = h

    n = halo + tt
    g1, g2, g3 = POOL_GROUP, 2 * POOL_GROUP, 3 * POOL_GROUP
    sa_ref[lead:lead + n, :] = hp_ref[lead:lead + n, :] + hp_ref[lead - 1:lead - 1 + n, :]
    sb_ref[lead:lead + n, g1:] = sa_ref[lead:lead + n, g1:] + sa_ref[lead - 2:lead - 2 + n, g1:]
    sa_ref[lead:lead + n, g2:] = sb_ref[lead:lead + n, g2:] + sb_ref[lead - 4:lead - 4 + n, g2:]
    window_sums = (
        sa_ref[top:top + tt, 0:g1],
        sb_ref[top:top + tt, g1:g2],
        sa_ref[top:top + tt, g2:g3],
        sa_ref[top:top + tt, g3:] + sa_ref[top - 8:top - 8 + tt, g3:],
    )
    pos = pos0 + t * tt + lax.broadcasted_iota(jnp.int32, (tt, 1), 0)
    for g, win in enumerate(POOL_WINDOWS):
        c0, c1 = g * POOL_GROUP, (g + 1) * POOL_GROUP
        cnt = jnp.minimum(pos + 1, win).astype(jnp.float32)
        diff = window_sums[g] / cnt - h[:, c0:c1]
        y = jnp.dot(diff.astype(jnp.bfloat16), w_ref[g], preferred_element_type=jnp.float32)
        x1_ref[0, :, c0:c1] = x[:, c0:c1] + y * scale_ref[:, c0:c1]
    tail = hp_ref[lead + tt:top + tt, :]
    st_ref[0] = tail
    hp_ref[lead:top, :] = tail


def _pool_layer(x, past16, g, w, scale, pos0, tt):
    B, T, D = x.shape
    halo = POOL_STATE + 1
    plane = pltpu.VMEM((SUBLANES + halo + tt, D), jnp.float32)
    return pl.pallas_call(
        functools.partial(_pool_kernel, tt=tt, pos0=pos0),
        grid=(B, T // tt),
        in_specs=[
            pl.BlockSpec((1, tt, D), lambda b, t: (b, t, 0)),
            pl.BlockSpec((1, halo, D), lambda b, t: (b, 0, 0)),
            _const_spec((1, D)),
            _const_spec(w.shape),
            _const_spec((1, D)),
        ],
        out_specs=[
            pl.BlockSpec((1, tt, D), lambda b, t: (b, t, 0)),
            pl.BlockSpec((1, halo, D), lambda b, t: (b, 0, 0)),
        ],
        out_shape=[
            jax.ShapeDtypeStruct((B, T, D), jnp.float32),
            jax.ShapeDtypeStruct((B, halo, D), jnp.float32),
        ],
        scratch_shapes=[plane, plane, plane],
        compiler_params=pltpu.CompilerParams(
            dimension_semantics=("arbitrary", "arbitrary"), vmem_limit_bytes=VMEM_LIMIT),
        name="pool_mixer",
    )(x, past16, g, w, scale)


def _mlp_body(x, g_ref, up_ref, down_ref, fc):
    h = _rms(x, g_ref[...]).astype(jnp.bfloat16)
    acc = x
    for c in range(D_FF // fc):
        u = jnp.dot(h, up_ref[:, c * fc:(c + 1) * fc], preferred_element_type=jnp.float32)
        u = jnp.maximum(u, 0.0)
        a = (u * u).astype(jnp.bfloat16)
        acc = acc + jnp.dot(a, down_ref[c * fc:(c + 1) * fc, :], preferred_element_type=jnp.float32)
    return acc


def _mlp_kernel(*refs, fc, with_attn, with_final):
    refs = list(refs)
    x_ref = refs.pop(0)
    if with_attn:
        o_ref = refs.pop(0)
        wo_ref = refs.pop(0)
    g_ref = refs.pop(0)
    up_ref = refs.pop(0)
    down_ref = refs.pop(0)
    if with_final:
        gf_ref = refs.pop(0)
    out_ref = refs.pop(0)

    x = x_ref[...]
    if with_attn:
        x = x + jnp.dot(o_ref[...], wo_ref[...], preferred_element_type=jnp.float32)
    acc = _mlp_body(x, g_ref, up_ref, down_ref, fc)
    if with_final:
        acc = _rms(acc, gf_ref[...])
    out_ref[...] = acc


def _mlp_layer(x, g, up, down, tm, fc, attn=None, final_g=None):
    N, D = x.shape
    row = lambda i: (i, 0)
    args, specs = [x], [pl.BlockSpec((tm, D), row)]
    if attn is not None:
        o, wo = attn
        args += [o, wo]
        specs += [pl.BlockSpec((tm, Q_W), row), _const_spec(wo.shape)]
    args += [g, up, down]
    specs += [_const_spec((1, D)), _const_spec(up.shape), _const_spec(down.shape)]
    if final_g is not None:
        args.append(final_g)
        specs.append(_const_spec((1, D)))
    return pl.pallas_call(
        functools.partial(_mlp_kernel, fc=fc, with_attn=attn is not None, with_final=final_g is not None),
        grid=(N // tm,),
        in_specs=specs,
        out_specs=pl.BlockSpec((tm, D), row),
        out_shape=jax.ShapeDtypeStruct((N, D), jnp.float32),
        compiler_params=pltpu.CompilerParams(
            dimension_semantics=("arbitrary",), vmem_limit_bytes=VMEM_LIMIT),
        name="mlp_attn_out" if attn is not None else "mlp",
    )(*args)


def _rope_block(xb, tab_ref, special):
    o = LANES if special else 0
    c = tab_ref[:, o:o + LANES]
    s1 = tab_ref[:, 2 * LANES + o:3 * LANES + o]
    s2 = tab_ref[:, 4 * LANES + o:5 * LANES + o]
    return xb * c + pltpu.roll(xb, LANES - HEAD_DIM // 2, 1) * s1 + pltpu.roll(xb, HEAD_DIM // 2, 1) * s2


def _rope_rows(p, c, s):
    half = HEAD_DIM // 2
    r = p.reshape(p.shape[0] // HEAD_DIM, 2, half, p.shape[1])
    x1, x2 = r[:, 0], r[:, 1]
    out = jnp.stack([x1 * c - x2 * s, x2 * c + x1 * s], axis=1)
    return out.reshape(p.shape)


def _proj_kernel(x_ref, g_ref, w_ref, wt_ref, tab_ref, tabt_ref,
                 kc_ref, vc_ref, kp_ref, kw_ref, kib_ref, qt_ref, qit_ref, wit_ref, vt_ref):
    h = _rms(x_ref[0], g_ref[...]).astype(jnp.bfloat16)

    def proj(off, width):
        return jnp.dot(h, w_ref[:, off:off + width], preferred_element_type=jnp.float32)

    def rope(p, special=False):
        return jnp.concatenate(
            [_rope_block(p[:, j:j + LANES], tab_ref, special) for j in range(0, p.shape[1], LANES)], axis=-1)

    kc = rope(proj(OFF_KC, KV_W))
    kc_ref[0] = kc
    vc_ref[0] = proj(OFF_VC, KV_W)
    gap = jnp.zeros((kc.shape[0], LANES - HEAD_DIM), jnp.float32)
    kp = jnp.concatenate(
        [piece for g in range(N_KV_HEADS) for piece in (kc[:, g * HEAD_DIM:(g + 1) * HEAD_DIM], gap)], axis=1)
    kp_ref[0] = kp.astype(kp_ref.dtype)
    kw = rope(proj(OFF_KW, LANES), special=True)
    kw_ref[0] = kw
    kib_ref[0] = kw.astype(kib_ref.dtype)

    def proj_t(row, n):
        return lax.dot_general(wt_ref[row:row + n, :], h, _NT, preferred_element_type=jnp.float32)

    half = HEAD_DIM // 2
    c, s = tabt_ref[0:half, :], tabt_ref[half:HEAD_DIM, :]
    qt_ref[0] = _rope_rows(proj_t(ROW_Q, Q_W), c * LOG2E, s * LOG2E).astype(qt_ref.dtype)
    qit_ref[0] = _rope_rows(proj_t(ROW_QI, QI_W), c, s).astype(qit_ref.dtype)
    vw = proj_t(ROW_V, KV_W + PACKED_ROWS)
    wit_ref[0] = vw[KV_W:KV_W + N_IDX_HEADS, :] * (N_IDX_HEADS ** -0.5)
    n = vw.shape[1]
    tail = jnp.where(lax.broadcasted_iota(jnp.int32, (V_SLOT - HEAD_DIM, n), 0) == 0, 1.0, 0.0)
    vt = jnp.concatenate(
        [piece for g in range(N_KV_HEADS) for piece in (vw[g * HEAD_DIM:(g + 1) * HEAD_DIM], tail)], axis=0)
    kbw = vt_ref.shape[3]
    for c in range(vt_ref.shape[1]):
        vt_ref[0, c] = vt[:, c * kbw:(c + 1) * kbw].astype(vt_ref.dtype)


def _proj_layer(x, g, w_rowmajor, w_transposed, tab, tabt, tm):
    B, T, D = x.shape
    blk = lambda w: pl.BlockSpec((1, tm, w), lambda b, t: (b, t, 0))
    blk_t = lambda r: pl.BlockSpec((1, r, tm), lambda b, t: (b, 0, t))
    bf, f32 = jnp.bfloat16, jnp.float32
    sds = jax.ShapeDtypeStruct
    kbw = min(tm, KEY_BLOCK)
    return pl.pallas_call(
        _proj_kernel,
        grid=(B, T // tm),
        in_specs=[
            blk(D),
            _const_spec((1, D)),
            _const_spec(w_rowmajor.shape),
            _const_spec(w_transposed.shape),
            pl.BlockSpec((tm, tab.shape[1]), lambda b, t: (t, 0)),
            pl.BlockSpec((HEAD_DIM, tm), lambda b, t: (0, t)),
        ],
        out_specs=[blk(KV_W), blk(KV_W), blk(KV_PAD_W), blk(LANES), blk(LANES),
                   blk_t(Q_W), blk_t(QI_W), blk_t(N_IDX_HEADS),
                   pl.BlockSpec((1, tm // kbw, V_ROWS, kbw), lambda b, t: (b, t, 0, 0))],
        out_shape=[sds((B, T, KV_W), f32), sds((B, T, KV_W), f32), sds((B, T, KV_PAD_W), bf),
                   sds((B, T, LANES), f32), sds((B, T, LANES), bf),
                   sds((B, Q_W, T), bf), sds((B, QI_W, T), bf), sds((B, N_IDX_HEADS, T), f32),
                   sds((B, T // kbw, V_ROWS, kbw), bf)],
        compiler_params=pltpu.CompilerParams(
            dimension_semantics=("arbitrary", "arbitrary"), vmem_limit_bytes=VMEM_LIMIT),
        name="attn_in_proj",
    )(x, g, w_rowmajor, w_transposed, tab, tabt)


def _attn_kernel(qt_ref, qit_ref, wit_ref, kp_ref, vt_ref, kib_ref, o_ref,
                 key_ref, hi_ref, lo_ref, lo2_ref, f_ref, s0_ref, s1_ref, oacc_ref, *,
                 tq, q_copies, n_keys_real, pos0, n_sel):
    kb = KEY_BLOCK
    nq = tq // q_copies
    t = pl.program_id(1)
    lane_q = lax.broadcasted_iota(jnp.int32, (1, tq), 1) % nq
    q_limit = ((pos0 + t * nq + lane_q) // CHUNK + 1) * CHUNK
    n_reach = jnp.minimum(((pos0 + (t + 1) * nq - 1) // CHUNK + 1) * CHUNK, n_keys_real)
    nkb = (n_reach + kb - 1) // kb
    n_unvisited = jnp.maximum(n_keys_real - nkb * kb, 0)
    key_row = lax.broadcasted_iota(jnp.int32, (kb, tq), 0)

    def block_start(j):
        return pl.multiple_of(j * kb, kb)

    def for_blocks(body, carry, tiers=(2, 1)):
        start = 0
        for n in tiers:
            def trip(i, c, n=n, start=start):
                for u in range(n):
                    c = body(start + n * i + u, c)
                return c
            trips = (nkb - start) // n
            carry = lax.fori_loop(0, trips, trip, carry)
            start = start + trips * n
        return carry

    def admissible(k0):
        return key_row < q_limit - k0

    wi = wit_ref[0]

    def score_block(j, carry):
        k0 = block_start(j)
        kx = kib_ref[0, pl.ds(k0, kb), 0:IDX_DIM]
        acc = jnp.zeros((kb, tq), jnp.float32)
        for hp in range(N_IDX_HEADS // 2):
            qi2 = jnp.concatenate(
                [qit_ref[0, (2 * hp + u) * IDX_DIM:(2 * hp + u + 1) * IDX_DIM, :] for u in range(2)], axis=1)
            lg = jnp.dot(kx, qi2, preferred_element_type=jnp.float32)
            for u in range(2):
                hh = 2 * hp + u
                acc = acc + jnp.maximum(lg[:, u * tq:(u + 1) * tq], 0.0) * wi[hh:hh + 1, :]
        sc = jnp.where(admissible(k0), acc, NEG)
        if n_keys_real % kb:
            sc = jnp.where(key_row < n_keys_real - k0, sc, -jnp.inf)
        bits = lax.bitcast_convert_type(sc, jnp.int32)
        key = bits ^ ((bits >> 31) & jnp.int32(0x7FFFFFFF))
        key_ref[pl.ds(k0, kb), :] = key
        hi_ref[pl.ds(k0, kb), :] = (key >> 16).astype(jnp.int16)
        lo_ref[pl.ds(k0, kb), :] = (key ^ 0x8000).astype(jnp.int16)
        return carry

    for_blocks(score_block, 0, tiers=(4, 2, 1))

    def count(pred):
        def body(j, acc):
            blk = key_ref[pl.ds(block_start(j), kb), :]
            return acc + _fold_rows(jnp.add, jnp.where(pred(blk, j), 1, 0).astype(jnp.int32))
        acc = for_blocks(body, jnp.zeros((SUBLANES, tq), jnp.int32))
        return jnp.sum(acc, axis=0, keepdims=True)

    def count16(ref, pred):
        def body(j, acc):
            hit = jnp.where(pred(ref[pl.ds(block_start(j), kb), :]), jnp.int16(1), jnp.int16(0))
            return acc + _fold_rows(jnp.add, hit, rows=PACKED_ROWS)
        acc = for_blocks(body, jnp.zeros((PACKED_ROWS, tq), jnp.int16))
        return jnp.sum(acc.astype(jnp.int32), axis=0, keepdims=True)

    def as_plane(v):
        tile = jnp.broadcast_to(v, (PACKED_ROWS, tq)).astype(jnp.int16)
        return jnp.concatenate([tile] * (kb // PACKED_ROWS), axis=0)

    def radix16(ref, n_wanted, n_all, unvisited_ge):
        def step(i, carry):
            thr, n_at = carry
            cand = thr + jnp.left_shift(jnp.int32(1), 15 - i)
            plane = as_plane(cand)
            n_ge = count16(ref, lambda blk: blk >= plane) + unvisited_ge(cand)
            take = n_ge >= n_wanted
            return jnp.where(take, cand, thr), jnp.where(take, n_ge, n_at)
        return lax.fori_loop(0, 16, step, (jnp.full((1, tq), -2 ** 15, jnp.int32), n_all))

    n_all = jnp.broadcast_to(nkb * kb + n_unvisited, (1, tq))
    thr_hi, n_ge_hi = radix16(hi_ref, n_sel, n_all, lambda cand: jnp.where(cand <= NEG_KEY_HI, n_unvisited, 0))
    hi_plane = as_plane(thr_hi)
    n_gt_hi = count16(hi_ref, lambda blk: blk > hi_plane) + jnp.where(thr_hi < NEG_KEY_HI, n_unvisited, 0)

    def park_block(j, carry):
        rows = pl.ds(block_start(j), kb)
        lo2_ref[rows, :] = jnp.where(hi_ref[rows, :] == hi_plane, lo_ref[rows, :], jnp.int16(-2 ** 15))
        return carry

    for_blocks(park_block, 0)
    neg_in_bucket = thr_hi == NEG_KEY_HI
    thr_lo, n_ge_lo = radix16(lo2_ref, n_sel - n_gt_hi, n_ge_hi - n_gt_hi,
                              lambda cand: jnp.where(neg_in_bucket & (cand <= NEG_KEY_LO), n_unvisited, 0))
    thr = thr_hi * 65536 + (thr_lo + 2 ** 15)
    lo_plane = as_plane(thr_lo)
    n_gt_lo = (count16(lo2_ref, lambda blk: blk > lo_plane)
               + jnp.where(neg_in_bucket & (thr_lo < NEG_KEY_LO), n_unvisited, 0))
    n_gt = n_gt_hi + n_gt_lo
    n_eq = n_ge_lo - n_gt_lo
    n_eq_adm = count(lambda blk, j: (blk == thr) & admissible(block_start(j)))
    need = n_sel - n_gt
    tie = (n_eq > need) & (n_eq_adm > 0)
    any_tie = jnp.max(tie.astype(jnp.int32)) > 0

    def write_mask(exact_ties):
        def body(j, run):
            k0 = block_start(j)
            kblk = key_ref[pl.ds(k0, kb), :]
            if exact_ties:
                r = lax.broadcasted_iota(jnp.int32, (kb, kb), 0)
                c = lax.broadcasted_iota(jnp.int32, (kb, kb), 1)
                tri = jnp.where(r >= c, 1.0, 0.0).astype(jnp.bfloat16)
                eqb = kblk == thr
                eqf = jnp.where(eqb, 1.0, 0.0)
                rank = jnp.dot(tri, eqf.astype(jnp.bfloat16), preferred_element_type=jnp.float32) + run
                run = run + jnp.sum(eqf, axis=0, keepdims=True)
                sel = (kblk > thr) | (eqb & (rank <= need.astype(jnp.float32)))
            else:
                sel = kblk >= thr
            f_ref[pl.ds(k0, kb), :] = jnp.where(sel, jnp.where(admissible(k0), jnp.inf, NEG), -jnp.inf)
            return run

        for_blocks(body, jnp.zeros((1, tq), jnp.float32))

    @pl.when(any_tie)
    def _():
        write_mask(True)

    @pl.when(jnp.logical_not(any_tie))
    def _():
        write_mask(False)

    tiles = KV_GROUP // q_copies
    gw = tiles * tq
    s_refs = (s0_ref, s1_ref)

    def logits_block(g, j):
        k0 = block_start(j)
        q4 = jnp.concatenate(
            [qt_ref[0, (g * tiles + r) * HEAD_DIM:(g * tiles + r + 1) * HEAD_DIM, :] for r in range(tiles)],
            axis=1)
        kblk = kp_ref[0, pl.ds(k0, kb), g * LANES:g * LANES + HEAD_DIM]
        s = jnp.dot(kblk, q4, preferred_element_type=jnp.float32)
        f = f_ref[pl.ds(k0, kb), :]
        s = jnp.minimum(s, jnp.concatenate([f] * tiles, axis=1))
        s_refs[g % 2][pl.ds(k0, kb), :] = s
        return _fold_rows(jnp.maximum, s)

    def values_block(g, j, m):
        p = jnp.exp2(s_refs[g % 2][pl.ds(block_start(j), kb), :] - m).astype(jnp.bfloat16)
        oacc_ref[...] += jnp.dot(vt_ref[0, j, g * V_SLOT:(g + 1) * V_SLOT, :], p,
                                 preferred_element_type=jnp.float32)

    m_init = jnp.full((SUBLANES, gw), -jnp.inf, jnp.float32)
    m_acc = for_blocks(lambda j, a: jnp.maximum(a, logits_block(0, j)), m_init, tiers=(4, 2, 1))
    for g in range(N_KV_HEADS):
        m = jnp.max(m_acc, axis=0, keepdims=True)
        oacc_ref[...] = jnp.zeros(oacc_ref.shape, jnp.float32)

        def fused(j, a, g=g, m=m):
            if g + 1 < N_KV_HEADS:
                a = jnp.maximum(a, logits_block(g + 1, j))
            values_block(g, j, m)
            return a

        m_acc = for_blocks(fused, m_init, tiers=(4, 2, 1))
        oacc = oacc_ref[...]
        on = oacc[0:HEAD_DIM] / oacc[HEAD_DIM:HEAD_DIM + 1]
        if q_copies > 1:
            o_ref[0, 0, g] = on.astype(o_ref.dtype)
            continue
        for u in range(KV_GROUP // 2):
            pair = jnp.concatenate([on[:, (2 * u) * tq:(2 * u + 1) * tq], on[:, (2 * u + 1) * tq:(2 * u + 2) * tq]],
                                   axis=0)
            c0 = (g * KV_GROUP + 2 * u) * HEAD_DIM
            o_ref[0, :, c0:c0 + 2 * HEAD_DIM] = pair.T.astype(o_ref.dtype)


def _attn_layer(qt, qit, wit, kp, vt, kib, *, tq, q_copies, n_keys_real, pos0, n_sel):
    B, _, T = qt.shape
    n_keys = kp.shape[1]
    gw = KV_GROUP // q_copies * tq
    qblk = lambda r: pl.BlockSpec((1, r, tq), lambda b, t: (b, 0, t))
    kblk = lambda w: pl.BlockSpec((1, n_keys, w), lambda b, t: (b, 0, 0))
    if q_copies == 1:
        out_spec = pl.BlockSpec((1, tq, Q_W), lambda b, t: (b, t, 0))
        out_shape = jax.ShapeDtypeStruct((B, T, Q_W), jnp.bfloat16)
    else:
        out_spec = pl.BlockSpec((1, 1, N_KV_HEADS, HEAD_DIM, gw), lambda b, t: (b, t, 0, 0, 0))
        out_shape = jax.ShapeDtypeStruct((B, T // tq, N_KV_HEADS, HEAD_DIM, gw), jnp.bfloat16)
    return pl.pallas_call(
        functools.partial(_attn_kernel, tq=tq, q_copies=q_copies, n_keys_real=n_keys_real, pos0=pos0, n_sel=n_sel),
        grid=(B, T // tq),
        in_specs=[qblk(Q_W // q_copies), qblk(QI_W), qblk(N_IDX_HEADS), kblk(KV_PAD_W),
                  pl.BlockSpec((1,) + vt.shape[1:], lambda b, t: (b, 0, 0, 0)), kblk(LANES)],
        out_specs=out_spec,
        out_shape=out_shape,
        scratch_shapes=[pltpu.VMEM((n_keys, tq), jnp.int32)] + [pltpu.VMEM((n_keys, tq), jnp.int16)] * 3 + [
                        pltpu.VMEM((n_keys, tq), jnp.float32),
                        pltpu.VMEM((n_keys, gw), jnp.float32),
                        pltpu.VMEM((n_keys, gw), jnp.float32),
                        pltpu.VMEM((V_SLOT, gw), jnp.float32)],
        compiler_params=pltpu.CompilerParams(
            dimension_semantics=("arbitrary", "arbitrary"), vmem_limit_bytes=VMEM_LIMIT),
        name="dsa_attention",
    )(qt, qit, wit, kp, vt, kib)


def _pad_heads(a):
    lead = a.shape[:-1]
    a = a.reshape(*lead, N_KV_HEADS, HEAD_DIM)
    a = jnp.pad(a, [(0, 0)] * len(lead) + [(0, 0), (0, LANES - HEAD_DIM)])
    return a.reshape(*lead, KV_PAD_W)


def _pack_w_in(w_in):
    o = 0
    wq = w_in[:, o:o + Q_W] * (HEAD_DIM ** -0.5); o += Q_W
    wk = w_in[:, o:o + KV_W]; o += KV_W
    wv = w_in[:, o:o + KV_W]; o += KV_W
    wqi = w_in[:, o:o + QI_W] * (IDX_DIM ** -0.5); o += QI_W
    wki = w_in[:, o:o + IDX_DIM]; o += IDX_DIM
    wwi = w_in[:, o:o + N_IDX_HEADS]
    wkw = jnp.pad(jnp.concatenate([wki, wwi], axis=1), ((0, 0), (0, LANES - IDX_DIM - N_IDX_HEADS)))
    rowmajor = jnp.concatenate([wk, wv, wkw], axis=1)
    wwi_rows = jnp.pad(wwi, ((0, 0), (0, PACKED_ROWS - N_IDX_HEADS)))
    transposed = jnp.concatenate([wq, wqi, wv, wwi_rows], axis=1).T
    return rowmajor.astype(jnp.bfloat16), transposed.astype(jnp.bfloat16)


def _rope_tables(pos):
    inv_freq = 1.0 / (ROPE_THETA ** (jnp.arange(0, HEAD_DIM, 2, dtype=jnp.float32) / HEAD_DIM))
    ang = pos.astype(jnp.float32)[:, None] * inv_freq[None, :]
    c, s = jnp.cos(ang), jnp.sin(ang)
    z = jnp.zeros_like(c)
    n = pos.shape[0]
    zpad = jnp.zeros((n, LANES - HEAD_DIM), jnp.float32)
    wi_scale = jnp.full((n, N_IDX_HEADS), N_IDX_HEADS ** -0.5, jnp.float32)
    c_kw = jnp.concatenate([c, c, wi_scale, jnp.zeros((n, LANES - HEAD_DIM - N_IDX_HEADS), jnp.float32)], axis=1)
    tab = jnp.concatenate([
        jnp.concatenate([c, c, c, c], axis=1), c_kw,
        jnp.concatenate([-s, z, -s, z], axis=1), jnp.concatenate([-s, z, zpad], axis=1),
        jnp.concatenate([z, s, z, s], axis=1), jnp.concatenate([z, s, zpad], axis=1),
    ], axis=1)
    tabt = jnp.concatenate([c, s], axis=1).T
    return tab, tabt


def _round_up(n, m):
    return (n + m - 1) // m * m


def _value_blocks_t(v):
    B, n, _ = v.shape
    nb = n // KEY_BLOCK
    vt = jnp.transpose(v.reshape(B, nb, KEY_BLOCK, N_KV_HEADS, HEAD_DIM), (0, 1, 3, 4, 2))
    ones = jnp.ones((B, nb, N_KV_HEADS, 1, KEY_BLOCK), v.dtype)
    zeros = jnp.zeros((B, nb, N_KV_HEADS, V_SLOT - HEAD_DIM - 1, KEY_BLOCK), v.dtype)
    return jnp.concatenate([vt, ones, zeros], axis=3).reshape(B, nb, V_ROWS, KEY_BLOCK)


def _trunk(x, pos0, pool_past, attn_past, norm_mix, norm_mlp, norm_final, pool_w, pool_scale,
           w_rowmajor, w_transposed, w_o, w_up, w_down):
    B, T, D = x.shape
    bf = jnp.bfloat16
    row = lambda v: v.reshape(1, D)

    if pool_past is None:
        past16 = jnp.zeros((B, POOL_STATE + 1, D), jnp.float32)
    else:
        past16 = jnp.pad(pool_past, ((0, 0), (1, 0), (0, 0)))
    tt = min(T, ROW_TILE)
    tm = min(B * T, ROW_TILE)
    x1, st = _pool_layer(x, past16, row(norm_mix[0]), pool_w[0].astype(bf), row(pool_scale[0]), pos0, tt)
    x2 = _mlp_layer(x1.reshape(B * T, D), row(norm_mlp[0]), w_up[0], w_down[0], tm, FF_CHUNK).reshape(B, T, D)
    pool_new = st[:, 1:][None]

    tab, tabt = _rope_tables(pos0 + jnp.arange(T, dtype=jnp.int32))
    kc, vc, kp, kw, kib, qt, qit, wit, vt = _proj_layer(
        x2, row(norm_mix[1]), w_rowmajor, w_transposed, tab, tabt, min(T, PROJ_TILE))
    k_new = kc.reshape(1, B, T, N_KV_HEADS, HEAD_DIM)
    v_new = vc.reshape(1, B, T, N_KV_HEADS, HEAD_DIM)
    ki_new = kw[:, :, :IDX_DIM][None]

    if attn_past is None:
        n_real = T
        kp_all, vt_all, kib_all = kp, vt, kib
    else:
        ck, cv, cki = attn_past
        P = ck.shape[1]
        n_real = P + T
        n_keys = _round_up(n_real, KEY_BLOCK)
        fill = lambda w: jnp.zeros((B, n_keys - n_real, w), bf)
        kp_all = jnp.concatenate([_pad_heads(ck.reshape(B, P, KV_W).astype(bf)), kp, fill(KV_PAD_W)], axis=1)
        kib_all = jnp.concatenate(
            [jnp.pad(cki.astype(bf), ((0, 0), (0, 0), (0, LANES - IDX_DIM))), kib, fill(LANES)], axis=1)
        vt_all = _value_blocks_t(
            jnp.concatenate([cv.reshape(B, P, KV_W).astype(bf), vc.astype(bf), fill(KV_W)], axis=1))
    n_sel = min(TOPK_MAX, n_real // 4)
    attn_args = dict(n_keys_real=n_real, pos0=pos0, n_sel=n_sel)
    if T % Q_TILE == 0:
        o = _attn_layer(qt, qit, wit, kp_all, vt_all, kib_all, tq=Q_TILE, q_copies=1, **attn_args)
    else:
        qc = LANES // T
        assert T * qc == LANES and KV_GROUP % qc == 0, (T, qc)
        rep = lambda a: jnp.concatenate([a] * qc, axis=-1)
        qt_heads = jnp.transpose(qt.reshape(B, N_HEADS // qc, qc, HEAD_DIM, T), (0, 1, 3, 2, 4))
        o = _attn_layer(qt_heads.reshape(B, Q_W // qc, LANES), rep(qit), rep(wit), kp_all, vt_all, kib_all,
                        tq=LANES, q_copies=qc, **attn_args)
        o = o.reshape(B, N_KV_HEADS, HEAD_DIM, KV_GROUP // qc, qc, T)
        o = jnp.transpose(o, (0, 5, 1, 3, 4, 2))
    o = o.reshape(B * T, Q_W)

    y = _mlp_layer(x2.reshape(B * T, D), row(norm_mlp[1]), w_up[1], w_down[1], tm, FF_CHUNK,
                   attn=(o, w_o), final_g=row(norm_final)).reshape(B, T, D)
    return y, pool_new, k_new, v_new, ki_new


def kernel(x_prompt, x_sample, state_pool, cache_k, cache_v, cache_kidx, norm_mix, norm_mlp, norm_final,
           pool_w, pool_scale, attn_w_in, attn_w_o, mlp_w_up, mlp_w_down):
    bf = jnp.bfloat16
    shared = (norm_mix, norm_mlp, norm_final, pool_w, pool_scale, *_pack_w_in(attn_w_in[0]),
              attn_w_o[0].astype(bf), mlp_w_up.astype(bf), mlp_w_down.astype(bf))
    y_p, pool_p, k_p, v_p, ki_p = _trunk(x_prompt, 0, None, None, *shared)
    y_s, pool_s, k_s, v_s, ki_s = _trunk(
        x_sample, cache_k.shape[2], state_pool[0], (cache_k[0], cache_v[0], cache_kidx[0]), *shared)
    return (y_p, y_s, pool_p, pool_s, k_p, v_p, ki_p, k_s, v_s, ki_s)
```

```python
import functools
import math
import struct

import jax
import jax.numpy as jnp
from jax import lax
from jax.experimental import pallas as pl
from jax.experimental.pallas import tpu as pltpu

D_MODEL = 1024
CHUNK = 64
POOL_WINDOWS = (2, 4, 8, 16)
POOL_GROUP = D_MODEL // len(POOL_WINDOWS)
POOL_STATE = max(POOL_WINDOWS) - 1
N_HEADS = 16
HEAD_DIM = 64
N_KV_HEADS = 4
KV_GROUP = N_HEADS // N_KV_HEADS
N_IDX_HEADS = 8
IDX_DIM = 64
TOPK_MAX = 256
ROPE_THETA = 10000.0
D_FF = 4 * D_MODEL
EPS = 1e-6
NEG = -1e30
Q_W = N_HEADS * HEAD_DIM
KV_W = N_KV_HEADS * HEAD_DIM
QI_W = N_IDX_HEADS * IDX_DIM
LOG2E = math.log2(math.e)

SUBLANES = 8
PACKED_ROWS = 2 * SUBLANES
LANES = 128
KV_PAD_W = N_KV_HEADS * LANES
V_SLOT = HEAD_DIM + PACKED_ROWS
V_ROWS = N_KV_HEADS * V_SLOT
KEY_BLOCK = 256
Q_TILE = 256
ROW_TILE = 1024
PROJ_TILE = 1024
FF_CHUNK = 512

OFF_KC = 0
OFF_VC = OFF_KC + KV_W
OFF_KW = OFF_VC + KV_W
PROJ_ROWMAJOR_W = OFF_KW + LANES
ROW_Q = 0
ROW_QI = ROW_Q + Q_W
ROW_V = ROW_QI + QI_W
ROW_WI = ROW_V + KV_W
PROJ_TRANSPOSED_ROWS = ROW_WI + PACKED_ROWS

VMEM_LIMIT = 56 * 1024 * 1024

_NT = (((1,), (1,)), ((), ()))


def _f32_order_key(x):
    b = struct.unpack("<i", struct.pack("<f", x))[0]
    return b ^ ((b >> 31) & 0x7FFFFFFF)


def _signed16(v):
    return v - (1 << 16) if v >= (1 << 15) else v


NEG_KEY = _f32_order_key(NEG)
NEG_KEY_HI = NEG_KEY >> 16
NEG_KEY_LO = _signed16((NEG_KEY & 0xFFFF) ^ 0x8000)


def _rms(x, g):
    return x * lax.rsqrt(jnp.mean(x * x, axis=-1, keepdims=True) + EPS) * g


def _const_spec(shape):
    n = len(shape)
    return pl.BlockSpec(shape, lambda *_: (0,) * n, pipeline_mode=pl.Buffered(1))


def _tree(op, parts):
    parts = list(parts)
    while len(parts) > 1:
        parts = [op(parts[i], parts[i + 1]) if i + 1 < len(parts) else parts[i] for i in range(0, len(parts), 2)]
    return parts[0]


def _fold_rows(op, x, rows=SUBLANES):
    return _tree(op, [x[r:r + rows] for r in range(0, x.shape[0], rows)])


def _pool_kernel(x_ref, past_ref, g_ref, w_ref, scale_ref, x1_ref, st_ref, hp_ref, sa_ref, sb_ref, *, tt, pos0):
    t = pl.program_id(1)
    halo = POOL_STATE + 1
    lead = SUBLANES
    top = lead + halo

    @pl.when(t == 0)
    def _():
        zeros = jnp.zeros((lead, hp_ref.shape[1]), jnp.float32)
        hp_ref[0:lead, :] = zeros
        sa_ref[0:lead, :] = zeros
        sb_ref[0:lead, :] = zeros
        hp_ref[lead:top, :] = past_ref[0]

    x = x_ref[0]
    h = _rms(x, g_ref[...])
    hp_ref[top:top + tt, :] = h

    n = halo + tt
    g1, g2, g3 = POOL_GROUP, 2 * POOL_GROUP, 3 * POOL_GROUP
    sa_ref[lead:lead + n, :] = hp_ref[lead:lead + n, :] + hp_ref[lead - 1:lead - 1 + n, :]
    sb_ref[lead:lead + n, g1:] = sa_ref[lead:lead + n, g1:] + sa_ref[lead - 2:lead - 2 + n, g1:]
    sa_ref[lead:lead + n, g2:] = sb_ref[lead:lead + n, g2:] + sb_ref[lead - 4:lead - 4 + n, g2:]
    window_sums = (
        sa_ref[top:top + tt, 0:g1],
        sb_ref[top:top + tt, g1:g2],
        sa_ref[top:top + tt, g2:g3],
        sa_ref[top:top + tt, g3:] + sa_ref[top - 8:top - 8 + tt, g3:],
    )
    pos = pos0 + t * tt + lax.broadcasted_iota(jnp.int32, (tt, 1), 0)
    for g, win in enumerate(POOL_WINDOWS):
        c0, c1 = g * POOL_GROUP, (g + 1) * POOL_GROUP
        cnt = jnp.minimum(pos + 1, win).astype(jnp.float32)
        diff = window_sums[g] / cnt - h[:, c0:c1]
        y = jnp.dot(diff.astype(jnp.bfloat16), w_ref[g], preferred_element_type=jnp.float32)
        x1_ref[0, :, c0:c1] = x[:, c0:c1] + y * scale_ref[:, c0:c1]
    tail = hp_ref[lead + tt:top + tt, :]
    st_ref[0] = tail
    hp_ref[lead:top, :] = tail


def _pool_layer(x, past16, g, w, scale, pos0, tt):
    B, T, D = x.shape
    halo = POOL_STATE + 1
    plane = pltpu.VMEM((SUBLANES + halo + tt, D), jnp.float32)
    return pl.pallas_call(
        functools.partial(_pool_kernel, tt=tt, pos0=pos0),
        grid=(B, T // tt),
        in_specs=[
            pl.BlockSpec((1, tt, D), lambda b, t: (b, t, 0)),
            pl.BlockSpec((1, halo, D), lambda b, t: (b, 0, 0)),
            _const_spec((1, D)),
            _const_spec(w.shape),
            _const_spec((1, D)),
        ],
        out_specs=[
            pl.BlockSpec((1, tt, D), lambda b, t: (b, t, 0)),
            pl.BlockSpec((1, halo, D), lambda b, t: (b, 0, 0)),
        ],
        out_shape=[
            jax.ShapeDtypeStruct((B, T, D), jnp.float32),
            jax.ShapeDtypeStruct((B, halo, D), jnp.float32),
        ],
        scratch_shapes=[plane, plane, plane],
        compiler_params=pltpu.CompilerParams(
            dimension_semantics=("arbitrary", "arbitrary"), vmem_limit_bytes=VMEM_LIMIT),
        name="pool_mixer",
    )(x, past16, g, w, scale)


def _mlp_body(x, g_ref, up_ref, down_ref, fc):
    h = _rms(x, g_ref[...]).astype(jnp.bfloat16)
    acc = x
    for c in range(D_FF // fc):
        u = jnp.dot(h, up_ref[:, c * fc:(c + 1) * fc], preferred_element_type=jnp.float32)
        u = jnp.maximum(u, 0.0)
        a = (u * u).astype(jnp.bfloat16)
        acc = acc + jnp.dot(a, down_ref[c * fc:(c + 1) * fc, :], preferred_element_type=jnp.float32)
    return acc


def _mlp_kernel(*refs, fc, with_attn, with_final):
    refs = list(refs)
    x_ref = refs.pop(0)
    if with_attn:
        o_ref = refs.pop(0)
        wo_ref = refs.pop(0)
    g_ref = refs.pop(0)
    up_ref = refs.pop(0)
    down_ref = refs.pop(0)
    if with_final:
        gf_ref = refs.pop(0)
    out_ref = refs.pop(0)

    x = x_ref[...]
    if with_attn:
        x = x + jnp.dot(o_ref[...], wo_ref[...], preferred_element_type=jnp.float32)
    acc = _mlp_body(x, g_ref, up_ref, down_ref, fc)
    if with_final:
        acc = _rms(acc, gf_ref[...])
    out_ref[...] = acc


def _mlp_layer(x, g, up, down, tm, fc, attn=None, final_g=None):
    N, D = x.shape
    row = lambda i: (i, 0)
    args, specs = [x], [pl.BlockSpec((tm, D), row)]
    if attn is not None:
        o, wo = attn
        args += [o, wo]
        specs += [pl.BlockSpec((tm, Q_W), row), _const_spec(wo.shape)]
    args += [g, up, down]
    specs += [_const_spec((1, D)), _const_spec(up.shape), _const_spec(down.shape)]
    if final_g is not None:
        args.append(final_g)
        specs.append(_const_spec((1, D)))
    return pl.pallas_call(
        functools.partial(_mlp_kernel, fc=fc, with_attn=attn is not None, with_final=final_g is not None),
        grid=(N // tm,),
        in_specs=specs,
        out_specs=pl.BlockSpec((tm, D), row),
        out_shape=jax.ShapeDtypeStruct((N, D), jnp.float32),
        compiler_params=pltpu.CompilerParams(
            dimension_semantics=("arbitrary",), vmem_limit_bytes=VMEM_LIMIT),
        name="mlp_attn_out" if attn is not None else "mlp",
    )(*args)


def _rope_block(xb, tab_ref, special):
    o = LANES if special else 0
    c = tab_ref[:, o:o + LANES]
    s1 = tab_ref[:, 2 * LANES + o:3 * LANES + o]
    s2 = tab_ref[:, 4 * LANES + o:5 * LANES + o]
    return xb * c + pltpu.roll(xb, LANES - HEAD_DIM // 2, 1) * s1 + pltpu.roll(xb, HEAD_DIM // 2, 1) * s2


def _rope_rows(p, c, s):
    half = HEAD_DIM // 2
    r = p.reshape(p.shape[0] // HEAD_DIM, 2, half, p.shape[1])
    x1, x2 = r[:, 0], r[:, 1]
    out = jnp.stack([x1 * c - x2 * s, x2 * c + x1 * s], axis=1)
    return out.reshape(p.shape)


def _proj_kernel(x_ref, g_ref, w_ref, wt_ref, tab_ref, tabt_ref,
                 kc_ref, vc_ref, kp_ref, kw_ref, kib_ref, qt_ref, qit_ref, wit_ref, vt_ref):
    h = _rms(x_ref[0], g_ref[...]).astype(jnp.bfloat16)

    def proj(off, width):
        return jnp.dot(h, w_ref[:, off:off + width], preferred_element_type=jnp.float32)

    def rope(p, special=False):
        return jnp.concatenate(
            [_rope_block(p[:, j:j + LANES], tab_ref, special) for j in range(0, p.shape[1], LANES)], axis=-1)

    kc = rope(proj(OFF_KC, KV_W))
    kc_ref[0] = kc
    vc_ref[0] = proj(OFF_VC, KV_W)
    gap = jnp.zeros((kc.shape[0], LANES - HEAD_DIM), jnp.float32)
    kp = jnp.concatenate(
        [piece for g in range(N_KV_HEADS) for piece in (kc[:, g * HEAD_DIM:(g + 1) * HEAD_DIM], gap)], axis=1)
    kp_ref[0] = kp.astype(kp_ref.dtype)
    kw = rope(proj(OFF_KW, LANES), special=True)
    kw_ref[0] = kw
    kib_ref[0] = kw.astype(kib_ref.dtype)

    def proj_t(row, n):
        return lax.dot_general(wt_ref[row:row + n, :], h, _NT, preferred_element_type=jnp.float32)

    half = HEAD_DIM // 2
    c, s = tabt_ref[0:half, :], tabt_ref[half:HEAD_DIM, :]
    qt_ref[0] = _rope_rows(proj_t(ROW_Q, Q_W), c * LOG2E, s * LOG2E).astype(qt_ref.dtype)
    qit_ref[0] = _rope_rows(proj_t(ROW_QI, QI_W), c, s).astype(qit_ref.dtype)
    vw = proj_t(ROW_V, KV_W + PACKED_ROWS)
    wit_ref[0] = vw[KV_W:KV_W + N_IDX_HEADS, :] * (N_IDX_HEADS ** -0.5)
    n = vw.shape[1]
    tail = jnp.where(lax.broadcasted_iota(jnp.int32, (V_SLOT - HEAD_DIM, n), 0) == 0, 1.0, 0.0)
    vt = jnp.concatenate(
        [piece for g in range(N_KV_HEADS) for piece in (vw[g * HEAD_DIM:(g + 1) * HEAD_DIM], tail)], axis=0)
    kbw = vt_ref.shape[3]
    for c in range(vt_ref.shape[1]):
        vt_ref[0, c] = vt[:, c * kbw:(c + 1) * kbw].astype(vt_ref.dtype)


def _proj_layer(x, g, w_rowmajor, w_transposed, tab, tabt, tm):
    B, T, D = x.shape
    blk = lambda w: pl.BlockSpec((1, tm, w), lambda b, t: (b, t, 0))
    blk_t = lambda r: pl.BlockSpec((1, r, tm), lambda b, t: (b, 0, t))
    bf, f32 = jnp.bfloat16, jnp.float32
    sds = jax.ShapeDtypeStruct
    kbw = min(tm, KEY_BLOCK)
    return pl.pallas_call(
        _proj_kernel,
        grid=(B, T // tm),
        in_specs=[
            blk(D),
            _const_spec((1, D)),
            _const_spec(w_rowmajor.shape),
            _const_spec(w_transposed.shape),
            pl.BlockSpec((tm, tab.shape[1]), lambda b, t: (t, 0)),
            pl.BlockSpec((HEAD_DIM, tm), lambda b, t: (0, t)),
        ],
        out_specs=[blk(KV_W), blk(KV_W), blk(KV_PAD_W), blk(LANES), blk(LANES),
                   blk_t(Q_W), blk_t(QI_W), blk_t(N_IDX_HEADS),
                   pl.BlockSpec((1, tm // kbw, V_ROWS, kbw), lambda b, t: (b, t, 0, 0))],
        out_shape=[sds((B, T, KV_W), f32), sds((B, T, KV_W), f32), sds((B, T, KV_PAD_W), bf),
                   sds((B, T, LANES), f32), sds((B, T, LANES), bf),
                   sds((B, Q_W, T), bf), sds((B, QI_W, T), bf), sds((B, N_IDX_HEADS, T), f32),
                   sds((B, T // kbw, V_ROWS, kbw), bf)],
        compiler_params=pltpu.CompilerParams(
            dimension_semantics=("arbitrary", "arbitrary"), vmem_limit_bytes=VMEM_LIMIT),
        name="attn_in_proj",
    )(x, g, w_rowmajor, w_transposed, tab, tabt)


def _attn_kernel(qt_ref, qit_ref, wit_ref, kp_ref, vt_ref, kib_ref, o_ref,
                 key_ref, hi_ref, lo_ref, lo2_ref, f_ref, s0_ref, s1_ref, oacc_ref, *,
                 tq, tq_real, n_keys_real, pos0, n_sel):
    kb = KEY_BLOCK
    t = pl.program_id(1)
    lane_q = lax.broadcasted_iota(jnp.int32, (1, tq), 1)
    q_limit = ((pos0 + t * tq + lane_q) // CHUNK + 1) * CHUNK
    qvalid = lane_q < tq_real
    n_reach = jnp.minimum(((pos0 + (t + 1) * tq - 1) // CHUNK + 1) * CHUNK, n_keys_real)
    nkb = (n_reach + kb - 1) // kb
    n_unvisited = jnp.maximum(n_keys_real - nkb * kb, 0)
    key_row = lax.broadcasted_iota(jnp.int32, (kb, tq), 0)

    def block_start(j):
        return pl.multiple_of(j * kb, kb)

    def for_blocks(body, carry, tiers=(2, 1)):
        start = 0
        for n in tiers:
            def trip(i, c, n=n, start=start):
                for u in range(n):
                    c = body(start + n * i + u, c)
                return c
            trips = (nkb - start) // n
            carry = lax.fori_loop(0, trips, trip, carry)
            start = start + trips * n
        return carry

    def admissible(k0):
        return key_row < q_limit - k0

    wi = wit_ref[0]

    def score_block(j, carry):
        k0 = block_start(j)
        kx = kib_ref[0, pl.ds(k0, kb), 0:IDX_DIM]
        acc = jnp.zeros((kb, tq), jnp.float32)
        for hp in range(N_IDX_HEADS // 2):
            qi2 = jnp.concatenate(
                [qit_ref[0, (2 * hp + u) * IDX_DIM:(2 * hp + u + 1) * IDX_DIM, :] for u in range(2)], axis=1)
            lg = jnp.dot(kx, qi2, preferred_element_type=jnp.float32)
            for u in range(2):
                hh = 2 * hp + u
                acc = acc + jnp.maximum(lg[:, u * tq:(u + 1) * tq], 0.0) * wi[hh:hh + 1, :]
        sc = jnp.where(admissible(k0), acc, NEG)
        if n_keys_real % kb:
            sc = jnp.where(key_row < n_keys_real - k0, sc, -jnp.inf)
        bits = lax.bitcast_convert_type(sc, jnp.int32)
        key = bits ^ ((bits >> 31) & jnp.int32(0x7FFFFFFF))
        key_ref[pl.ds(k0, kb), :] = key
        hi_ref[pl.ds(k0, kb), :] = (key >> 16).astype(jnp.int16)
        lo_ref[pl.ds(k0, kb), :] = (key ^ 0x8000).astype(jnp.int16)
        return carry

    for_blocks(score_block, 0, tiers=(4, 2, 1))

    def count16(ref, pred):
        def body(j, acc):
            hit = jnp.where(pred(ref[pl.ds(block_start(j), kb), :]), jnp.int16(1), jnp.int16(0))
            return acc + _fold_rows(jnp.add, hit, rows=PACKED_ROWS)
        acc = for_blocks(body, jnp.zeros((PACKED_ROWS, tq), jnp.int16))
        return jnp.sum(acc.astype(jnp.int32), axis=0, keepdims=True)

    def as_plane(v):
        tile = jnp.broadcast_to(v, (PACKED_ROWS, tq)).astype(jnp.int16)
        return jnp.concatenate([tile] * (kb // PACKED_ROWS), axis=0)

    def radix16(ref, n_wanted, n_all, unvisited_ge):
        def step(i, carry):
            thr, n_at = carry
            cand = thr + jnp.left_shift(jnp.int32(1), 15 - i)
            plane = as_plane(cand)
            n_ge = count16(ref, lambda blk: blk >= plane) + unvisited_ge(cand)
            take = n_ge >= n_wanted
            return jnp.where(take, cand, thr), jnp.where(take, n_ge, n_at)
        return lax.fori_loop(0, 16, step, (jnp.full((1, tq), -2 ** 15, jnp.int32), n_all))

    n_all = jnp.broadcast_to(nkb * kb + n_unvisited, (1, tq))
    thr_hi, n_ge_hi = radix16(hi_ref, n_sel, n_all, lambda cand: jnp.where(cand <= NEG_KEY_HI, n_unvisited, 0))
    hi_plane = as_plane(thr_hi)
    n_gt_hi = count16(hi_ref, lambda blk: blk > hi_plane) + jnp.where(thr_hi < NEG_KEY_HI, n_unvisited, 0)

    def park_block(j, carry):
        rows = pl.ds(block_start(j), kb)
        lo2_ref[rows, :] = jnp.where(hi_ref[rows, :] == hi_plane, lo_ref[rows, :], jnp.int16(-2 ** 15))
        return carry

    for_blocks(park_block, 0)
    neg_in_bucket = thr_hi == NEG_KEY_HI
    thr_lo, n_ge_lo = radix16(lo2_ref, n_sel - n_gt_hi, n_ge_hi - n_gt_hi,
                              lambda cand: jnp.where(neg_in_bucket & (cand <= NEG_KEY_LO), n_unvisited, 0))
    thr = thr_hi * 65536 + (thr_lo + 2 ** 15)
    lo_plane = as_plane(thr_lo)
    n_gt_lo = (count16(lo2_ref, lambda blk: blk > lo_plane)
               + jnp.where(neg_in_bucket & (thr_lo < NEG_KEY_LO), n_unvisited, 0))
    n_gt = n_gt_hi + n_gt_lo
    n_eq = n_ge_lo - n_gt_lo
    need = n_sel - n_gt
    tie = (n_eq > need) & qvalid
    any_tie = jnp.max(tie.astype(jnp.int32)) > 0

    def write_mask(exact_ties):
        def body(j, run):
            k0 = block_start(j)
            kblk = key_ref[pl.ds(k0, kb), :]
            if exact_ties:
                r = lax.broadcasted_iota(jnp.int32, (kb, kb), 0)
                c = lax.broadcasted_iota(jnp.int32, (kb, kb), 1)
                tri = jnp.where(r >= c, 1.0, 0.0).astype(jnp.bfloat16)
                eqb = kblk == thr
                eqf = jnp.where(eqb, 1.0, 0.0)
                rank = jnp.dot(tri, eqf.astype(jnp.bfloat16), preferred_element_type=jnp.float32) + run
                run = run + jnp.sum(eqf, axis=0, keepdims=True)
                sel = (kblk > thr) | (eqb & (rank <= need.astype(jnp.float32)))
            else:
                sel = kblk >= thr
            f_ref[pl.ds(k0, kb), :] = jnp.where(sel, jnp.where(admissible(k0), jnp.inf, NEG), -jnp.inf)
            return run

        for_blocks(body, jnp.zeros((1, tq), jnp.float32))

    @pl.when(any_tie)
    def _():
        write_mask(True)

    @pl.when(jnp.logical_not(any_tie))
    def _():
        write_mask(False)

    gw = KV_GROUP * tq
    s_refs = (s0_ref, s1_ref)

    def logits_block(g, j):
        k0 = block_start(j)
        q4 = jnp.concatenate(
            [qt_ref[0, (g * KV_GROUP + r) * HEAD_DIM:(g * KV_GROUP + r + 1) * HEAD_DIM, :] for r in range(KV_GROUP)],
            axis=1)
        kblk = kp_ref[0, pl.ds(k0, kb), g * LANES:g * LANES + HEAD_DIM]
        s = jnp.dot(kblk, q4, preferred_element_type=jnp.float32)
        f = f_ref[pl.ds(k0, kb), :]
        s = jnp.minimum(s, jnp.concatenate([f] * KV_GROUP, axis=1))
        s_refs[g % 2][pl.ds(k0, kb), :] = s
        return _fold_rows(jnp.maximum, s)

    def values_block(g, j, m):
        p = jnp.exp2(s_refs[g % 2][pl.ds(block_start(j), kb), :] - m).astype(jnp.bfloat16)
        oacc_ref[...] += jnp.dot(vt_ref[0, j, g * V_SLOT:(g + 1) * V_SLOT, :], p,
                                 preferred_element_type=jnp.float32)

    m_init = jnp.full((SUBLANES, gw), -jnp.inf, jnp.float32)
    m_acc = for_blocks(lambda j, a: jnp.maximum(a, logits_block(0, j)), m_init, tiers=(4, 2, 1))
    for g in range(N_KV_HEADS):
        m = jnp.max(m_acc, axis=0, keepdims=True)
        oacc_ref[...] = jnp.zeros(oacc_ref.shape, jnp.float32)

        def fused(j, a, g=g, m=m):
            if g + 1 < N_KV_HEADS:
                a = jnp.maximum(a, logits_block(g + 1, j))
            values_block(g, j, m)
            return a

        m_acc = for_blocks(fused, m_init, tiers=(4, 2, 1))
        oacc = oacc_ref[...]
        on = oacc[0:HEAD_DIM] * (1.0 / oacc[HEAD_DIM:HEAD_DIM + 1])
        for u in range(KV_GROUP // 2):
            pair = jnp.concatenate([on[:, (2 * u) * tq:(2 * u + 1) * tq], on[:, (2 * u + 1) * tq:(2 * u + 2) * tq]],
                                   axis=0)
            c0 = (g * KV_GROUP + 2 * u) * HEAD_DIM
            o_ref[0, :, c0:c0 + 2 * HEAD_DIM] = pair.T.astype(o_ref.dtype)


def _attn_layer(qt, qit, wit, kp, vt, kib, *, tq, tq_real, n_keys_real, pos0, n_sel):
    B, _, T = qt.shape
    n_keys = kp.shape[1]
    qblk = lambda r: pl.BlockSpec((1, r, tq), lambda b, t: (b, 0, t))
    kblk = lambda w: pl.BlockSpec((1, n_keys, w), lambda b, t: (b, 0, 0))
    return pl.pallas_call(
        functools.partial(_attn_kernel, tq=tq, tq_real=tq_real, n_keys_real=n_keys_real, pos0=pos0, n_sel=n_sel),
        grid=(B, T // tq),
        in_specs=[qblk(Q_W), qblk(QI_W), qblk(N_IDX_HEADS), kblk(KV_PAD_W),
                  pl.BlockSpec((1,) + vt.shape[1:], lambda b, t: (b, 0, 0, 0)), kblk(LANES)],
        out_specs=pl.BlockSpec((1, tq, Q_W), lambda b, t: (b, t, 0)),
        out_shape=jax.ShapeDtypeStruct((B, T, Q_W), jnp.bfloat16),
        scratch_shapes=[pltpu.VMEM((n_keys, tq), jnp.int32)] + [pltpu.VMEM((n_keys, tq), jnp.int16)] * 3 + [
                        pltpu.VMEM((n_keys, tq), jnp.float32),
                        pltpu.VMEM((n_keys, KV_GROUP * tq), jnp.float32),
                        pltpu.VMEM((n_keys, KV_GROUP * tq), jnp.float32),
                        pltpu.VMEM((V_SLOT, KV_GROUP * tq), jnp.float32)],
        compiler_params=pltpu.CompilerParams(
            dimension_semantics=("arbitrary", "arbitrary"), vmem_limit_bytes=VMEM_LIMIT),
        name="dsa_attention",
    )(qt, qit, wit, kp, vt, kib)


def _pad_heads(a):
    lead = a.shape[:-1]
    a = a.reshape(*lead, N_KV_HEADS, HEAD_DIM)
    a = jnp.pad(a, [(0, 0)] * len(lead) + [(0, 0), (0, LANES - HEAD_DIM)])
    return a.reshape(*lead, KV_PAD_W)


def _pack_w_in(w_in):
    o = 0
    wq = w_in[:, o:o + Q_W] * (HEAD_DIM ** -0.5); o += Q_W
    wk = w_in[:, o:o + KV_W]; o += KV_W
    wv = w_in[:, o:o + KV_W]; o += KV_W
    wqi = w_in[:, o:o + QI_W] * (IDX_DIM ** -0.5); o += QI_W
    wki = w_in[:, o:o + IDX_DIM]; o += IDX_DIM
    wwi = w_in[:, o:o + N_IDX_HEADS]
    wkw = jnp.pad(jnp.concatenate([wki, wwi], axis=1), ((0, 0), (0, LANES - IDX_DIM - N_IDX_HEADS)))
    rowmajor = jnp.concatenate([wk, wv, wkw], axis=1)
    wwi_rows = jnp.pad(wwi, ((0, 0), (0, PACKED_ROWS - N_IDX_HEADS)))
    transposed = jnp.concatenate([wq, wqi, wv, wwi_rows], axis=1).T
    return rowmajor.astype(jnp.bfloat16), transposed.astype(jnp.bfloat16)


def _rope_tables(pos):
    inv_freq = 1.0 / (ROPE_THETA ** (jnp.arange(0, HEAD_DIM, 2, dtype=jnp.float32) / HEAD_DIM))
    ang = pos.astype(jnp.float32)[:, None] * inv_freq[None, :]
    c, s = jnp.cos(ang), jnp.sin(ang)
    z = jnp.zeros_like(c)
    n = pos.shape[0]
    zpad = jnp.zeros((n, LANES - HEAD_DIM), jnp.float32)
    wi_scale = jnp.full((n, N_IDX_HEADS), N_IDX_HEADS ** -0.5, jnp.float32)
    c_kw = jnp.concatenate([c, c, wi_scale, jnp.zeros((n, LANES - HEAD_DIM - N_IDX_HEADS), jnp.float32)], axis=1)
    tab = jnp.concatenate([
        jnp.concatenate([c, c, c, c], axis=1), c_kw,
        jnp.concatenate([-s, z, -s, z], axis=1), jnp.concatenate([-s, z, zpad], axis=1),
        jnp.concatenate([z, s, z, s], axis=1), jnp.concatenate([z, s, zpad], axis=1),
    ], axis=1)
    tabt = jnp.concatenate([c, s], axis=1).T
    return tab, tabt


def _round_up(n, m):
    return (n + m - 1) // m * m


def _value_blocks_t(v):
    B, n, _ = v.shape
    nb = n // KEY_BLOCK
    vt = jnp.transpose(v.reshape(B, nb, KEY_BLOCK, N_KV_HEADS, HEAD_DIM), (0, 1, 3, 4, 2))
    ones = jnp.ones((B, nb, N_KV_HEADS, 1, KEY_BLOCK), v.dtype)
    zeros = jnp.zeros((B, nb, N_KV_HEADS, V_SLOT - HEAD_DIM - 1, KEY_BLOCK), v.dtype)
    return jnp.concatenate([vt, ones, zeros], axis=3).reshape(B, nb, V_ROWS, KEY_BLOCK)


def _trunk(x, pos0, pool_past, attn_past, norm_mix, norm_mlp, norm_final, pool_w, pool_scale,
           w_rowmajor, w_transposed, w_o, w_up, w_down):
    B, T, D = x.shape
    bf = jnp.bfloat16
    row = lambda v: v.reshape(1, D)

    if pool_past is None:
        past16 = jnp.zeros((B, POOL_STATE + 1, D), jnp.float32)
    else:
        past16 = jnp.pad(pool_past, ((0, 0), (1, 0), (0, 0)))
    tt = min(T, ROW_TILE)
    tm = min(B * T, ROW_TILE)
    x1, st = _pool_layer(x, past16, row(norm_mix[0]), pool_w[0].astype(bf), row(pool_scale[0]), pos0, tt)
    x2 = _mlp_layer(x1.reshape(B * T, D), row(norm_mlp[0]), w_up[0], w_down[0], tm, FF_CHUNK).reshape(B, T, D)
    pool_new = st[:, 1:][None]

    tab, tabt = _rope_tables(pos0 + jnp.arange(T, dtype=jnp.int32))
    kc, vc, kp, kw, kib, qt, qit, wit, vt = _proj_layer(
        x2, row(norm_mix[1]), w_rowmajor, w_transposed, tab, tabt, min(T, PROJ_TILE))
    k_new = kc.reshape(1, B, T, N_KV_HEADS, HEAD_DIM)
    v_new = vc.reshape(1, B, T, N_KV_HEADS, HEAD_DIM)
    ki_new = kw[:, :, :IDX_DIM][None]

    if attn_past is None:
        n_real = T
        kp_all, vt_all, kib_all = kp, vt, kib
    else:
        ck, cv, cki = attn_past
        P = ck.shape[1]
        n_real = P + T
        n_keys = _round_up(n_real, KEY_BLOCK)
        fill = lambda w: jnp.zeros((B, n_keys - n_real, w), bf)
        kp_all = jnp.concatenate([_pad_heads(ck.reshape(B, P, KV_W).astype(bf)), kp, fill(KV_PAD_W)], axis=1)
        kib_all = jnp.concatenate(
            [jnp.pad(cki.astype(bf), ((0, 0), (0, 0), (0, LANES - IDX_DIM))), kib, fill(LANES)], axis=1)
        vt_all = _value_blocks_t(
            jnp.concatenate([cv.reshape(B, P, KV_W).astype(bf), vc.astype(bf), fill(KV_W)], axis=1))
    tq = min(Q_TILE, _round_up(T, LANES))
    t_pad = _round_up(T, tq)
    padq = lambda a: jnp.pad(a, ((0, 0), (0, 0), (0, t_pad - T)))
    n_sel = min(TOPK_MAX, n_real // 4)
    o = _attn_layer(padq(qt), padq(qit), padq(wit), kp_all, vt_all, kib_all,
                    tq=tq, tq_real=min(T, tq), n_keys_real=n_real, pos0=pos0, n_sel=n_sel)
    o = o[:, :T].reshape(B * T, Q_W)

    y = _mlp_layer(x2.reshape(B * T, D), row(norm_mlp[1]), w_up[1], w_down[1], tm, FF_CHUNK,
                   attn=(o, w_o), final_g=row(norm_final)).reshape(B, T, D)
    return y, pool_new, k_new, v_new, ki_new


def kernel(x_prompt, x_sample, state_pool, cache_k, cache_v, cache_kidx, norm_mix, norm_mlp, norm_final,
           pool_w, pool_scale, attn_w_in, attn_w_o, mlp_w_up, mlp_w_down):
    bf = jnp.bfloat16
    shared = (norm_mix, norm_mlp, norm_final, pool_w, pool_scale, *_pack_w_in(attn_w_in[0]),
              attn_w_o[0].astype(bf), mlp_w_up.astype(bf), mlp_w_down.astype(bf))
    y_p, pool_p, k_p, v_p, ki_p = _trunk(x_prompt, 0, None, None, *shared)
    y_s, pool_s, k_s, v_s, ki_s = _trunk(
        x_sample, cache_k.shape[2], state_pool[0], (cache_k[0], cache_v[0], cache_kidx[0]), *shared)
    return (y_p, y_s, pool_p, pool_s, k_p, v_p, ki_p, k_s, v_s, ki_s)
```

```python
import functools
import math
import struct

import jax
import jax.numpy as jnp
from jax import lax
from jax.experimental import pallas as pl
from jax.experimental.pallas import tpu as pltpu

D_MODEL = 1024
CHUNK = 64
POOL_WINDOWS = (2, 4, 8, 16)
POOL_GROUP = D_MODEL // len(POOL_WINDOWS)
POOL_STATE = max(POOL_WINDOWS) - 1
N_HEADS = 16
HEAD_DIM = 64
N_KV_HEADS = 4
KV_GROUP = N_HEADS // N_KV_HEADS
N_IDX_HEADS = 8
IDX_DIM = 64
TOPK_MAX = 256
ROPE_THETA = 10000.0
D_FF = 4 * D_MODEL
EPS = 1e-6
NEG = -1e30
Q_W = N_HEADS * HEAD_DIM
KV_W = N_KV_HEADS * HEAD_DIM
QI_W = N_IDX_HEADS * IDX_DIM
LOG2E = math.log2(math.e)

SUBLANES = 8
PACKED_ROWS = 2 * SUBLANES
LANES = 128
KV_PAD_W = N_KV_HEADS * LANES
V_SLOT = HEAD_DIM + PACKED_ROWS
V_ROWS = N_KV_HEADS * V_SLOT
KEY_BLOCK = 256
Q_TILE = 256
ROW_TILE = 1024
PROJ_TILE = 1024
FF_CHUNK = 512

OFF_KC = 0
OFF_VC = OFF_KC + KV_W
OFF_KW = OFF_VC + KV_W
PROJ_ROWMAJOR_W = OFF_KW + LANES
ROW_Q = 0
ROW_QI = ROW_Q + Q_W
ROW_V = ROW_QI + QI_W
ROW_WI = ROW_V + KV_W
PROJ_TRANSPOSED_ROWS = ROW_WI + PACKED_ROWS

VMEM_LIMIT = 56 * 1024 * 1024

_NT = (((1,), (1,)), ((), ()))


def _f32_order_key(x):
    b = struct.unpack("<i", struct.pack("<f", x))[0]
    return b ^ ((b >> 31) & 0x7FFFFFFF)


def _signed16(v):
    return v - (1 << 16) if v >= (1 << 15) else v


NEG_KEY = _f32_order_key(NEG)
NEG_KEY_HI = NEG_KEY >> 16
NEG_KEY_LO = _signed16((NEG_KEY & 0xFFFF) ^ 0x8000)


def _rms(x, g):
    return x * lax.rsqrt(jnp.mean(x * x, axis=-1, keepdims=True) + EPS) * g


def _const_spec(shape):
    n = len(shape)
    return pl.BlockSpec(shape, lambda *_: (0,) * n, pipeline_mode=pl.Buffered(1))


def _tree(op, parts):
    parts = list(parts)
    while len(parts) > 1:
        parts = [op(parts[i], parts[i + 1]) if i + 1 < len(parts) else parts[i] for i in range(0, len(parts), 2)]
    return parts[0]


def _fold_rows(op, x, rows=SUBLANES):
    return _tree(op, [x[r:r + rows] for r in range(0, x.shape[0], rows)])


def _pool_kernel(x_ref, past_ref, g_ref, w_ref, scale_ref, x1_ref, st_ref, hp_ref, sa_ref, sb_ref, *, tt, pos0):
    t = pl.program_id(1)
    halo = POOL_STATE + 1
    lead = SUBLANES
    top = lead + halo

    @pl.when(t == 0)
    def _():
        zeros = jnp.zeros((lead, hp_ref.shape[1]), jnp.float32)
        hp_ref[0:lead, :] = zeros
        sa_ref[0:lead, :] = zeros
        sb_ref[0:lead, :] = zeros
        hp_ref[lead:top, :] = past_ref[0]

    x = x_ref[0]
    h = _rms(x, g_ref[...])
    hp_ref[top:top + tt, :] = h

    n = halo + tt
    g1, g2, g3 = POOL_GROUP, 2 * POOL_GROUP, 3 * POOL_GROUP
    sa_ref[lead:lead + n, :] = hp_ref[lead:lead + n, :] + hp_ref[lead - 1:lead - 1 + n, :]
    sb_ref[lead:lead + n, g1:] = sa_ref[lead:lead + n, g1:] + sa_ref[lead - 2:lead - 2 + n, g1:]
    sa_ref[lead:lead + n, g2:] = sb_ref[lead:lead + n, g2:] + sb_ref[lead - 4:lead - 4 + n, g2:]
    window_sums = (
        sa_ref[top:top + tt, 0:g1],
        sb_ref[top:top + tt, g1:g2],
        sa_ref[top:top + tt, g2:g3],
        sa_ref[top:top + tt, g3:] + sa_ref[top - 8:top - 8 + tt, g3:],
    )
    pos = pos0 + t * tt + lax.broadcasted_iota(jnp.int32, (tt, 1), 0)
    for g, win in enumerate(POOL_WINDOWS):
        c0, c1 = g * POOL_GROUP, (g + 1) * POOL_GROUP
        cnt = jnp.minimum(pos + 1, win).astype(jnp.float32)
        diff = window_sums[g] / cnt - h[:, c0:c1]
        y = jnp.dot(diff.astype(jnp.bfloat16), w_ref[g], preferred_element_type=jnp.float32)
        x1_ref[0, :, c0:c1] = x[:, c0:c1] + y * scale_ref[:, c0:c1]
    tail = hp_ref[lead + tt:top + tt, :]
    st_ref[0] = tail
    hp_ref[lead:top, :] = tail


def _pool_layer(x, past16, g, w, scale, pos0, tt):
    B, T, D = x.shape
    halo = POOL_STATE + 1
    plane = pltpu.VMEM((SUBLANES + halo + tt, D), jnp.float32)
    return pl.pallas_call(
        functools.partial(_pool_kernel, tt=tt, pos0=pos0),
        grid=(B, T // tt),
        in_specs=[
            pl.BlockSpec((1, tt, D), lambda b, t: (b, t, 0)),
            pl.BlockSpec((1, halo, D), lambda b, t: (b, 0, 0)),
            _const_spec((1, D)),
            _const_spec(w.shape),
            _const_spec((1, D)),
        ],
        out_specs=[
            pl.BlockSpec((1, tt, D), lambda b, t: (b, t, 0)),
            pl.BlockSpec((1, halo, D), lambda b, t: (b, 0, 0)),
        ],
        out_shape=[
            jax.ShapeDtypeStruct((B, T, D), jnp.float32),
            jax.ShapeDtypeStruct((B, halo, D), jnp.float32),
        ],
        scratch_shapes=[plane, plane, plane],
        compiler_params=pltpu.CompilerParams(
            dimension_semantics=("arbitrary", "arbitrary"), vmem_limit_bytes=VMEM_LIMIT),
        name="pool_mixer",
    )(x, past16, g, w, scale)


def _mlp_body(x, g_ref, up_ref, down_ref, fc):
    h = _rms(x, g_ref[...]).astype(jnp.bfloat16)
    acc = x
    for c in range(D_FF // fc):
        u = jnp.dot(h, up_ref[:, c * fc:(c + 1) * fc], preferred_element_type=jnp.float32)
        u = jnp.maximum(u, 0.0)
        a = (u * u).astype(jnp.bfloat16)
        acc = acc + jnp.dot(a, down_ref[c * fc:(c + 1) * fc, :], preferred_element_type=jnp.float32)
    return acc


def _mlp_kernel(*refs, fc, with_attn, with_final):
    refs = list(refs)
    x_ref = refs.pop(0)
    if with_attn:
        o_ref = refs.pop(0)
        wo_ref = refs.pop(0)
    g_ref = refs.pop(0)
    up_ref = refs.pop(0)
    down_ref = refs.pop(0)
    if with_final:
        gf_ref = refs.pop(0)
    out_ref = refs.pop(0)

    x = x_ref[...]
    if with_attn:
        x = x + jnp.dot(o_ref[...], wo_ref[...], preferred_element_type=jnp.float32)
    acc = _mlp_body(x, g_ref, up_ref, down_ref, fc)
    if with_final:
        acc = _rms(acc, gf_ref[...])
    out_ref[...] = acc


def _mlp_layer(x, g, up, down, tm, fc, attn=None, final_g=None):
    N, D = x.shape
    row = lambda i: (i, 0)
    args, specs = [x], [pl.BlockSpec((tm, D), row)]
    if attn is not None:
        o, wo = attn
        args += [o, wo]
        specs += [pl.BlockSpec((tm, Q_W), row), _const_spec(wo.shape)]
    args += [g, up, down]
    specs += [_const_spec((1, D)), _const_spec(up.shape), _const_spec(down.shape)]
    if final_g is not None:
        args.append(final_g)
        specs.append(_const_spec((1, D)))
    return pl.pallas_call(
        functools.partial(_mlp_kernel, fc=fc, with_attn=attn is not None, with_final=final_g is not None),
        grid=(N // tm,),
        in_specs=specs,
        out_specs=pl.BlockSpec((tm, D), row),
        out_shape=jax.ShapeDtypeStruct((N, D), jnp.float32),
        compiler_params=pltpu.CompilerParams(
            dimension_semantics=("arbitrary",), vmem_limit_bytes=VMEM_LIMIT),
        name="mlp_attn_out" if attn is not None else "mlp",
    )(*args)


def _rope_block(xb, tab_ref, special):
    o = LANES if special else 0
    c = tab_ref[:, o:o + LANES]
    s1 = tab_ref[:, 2 * LANES + o:3 * LANES + o]
    s2 = tab_ref[:, 4 * LANES + o:5 * LANES + o]
    return xb * c + pltpu.roll(xb, LANES - HEAD_DIM // 2, 1) * s1 + pltpu.roll(xb, HEAD_DIM // 2, 1) * s2


def _rope_rows(p, c, s):
    half = HEAD_DIM // 2
    r = p.reshape(p.shape[0] // HEAD_DIM, 2, half, p.shape[1])
    x1, x2 = r[:, 0], r[:, 1]
    out = jnp.stack([x1 * c - x2 * s, x2 * c + x1 * s], axis=1)
    return out.reshape(p.shape)


def _proj_kernel(x_ref, g_ref, w_ref, wt_ref, tab_ref, tabt_ref,
                 kc_ref, vc_ref, kp_ref, kw_ref, kib_ref, qt_ref, qit_ref, wit_ref, vt_ref):
    h = _rms(x_ref[0], g_ref[...]).astype(jnp.bfloat16)

    def proj(off, width):
        return jnp.dot(h, w_ref[:, off:off + width], preferred_element_type=jnp.float32)

    def rope(p, special=False):
        return jnp.concatenate(
            [_rope_block(p[:, j:j + LANES], tab_ref, special) for j in range(0, p.shape[1], LANES)], axis=-1)

    kc = rope(proj(OFF_KC, KV_W))
    kc_ref[0] = kc
    vc_ref[0] = proj(OFF_VC, KV_W)
    gap = jnp.zeros((kc.shape[0], LANES - HEAD_DIM), jnp.float32)
    kp = jnp.concatenate(
        [piece for g in range(N_KV_HEADS) for piece in (kc[:, g * HEAD_DIM:(g + 1) * HEAD_DIM], gap)], axis=1)
    kp_ref[0] = kp.astype(kp_ref.dtype)
    kw = rope(proj(OFF_KW, LANES), special=True)
    kw_ref[0] = kw
    kib_ref[0] = kw.astype(kib_ref.dtype)

    def proj_t(row, n):
        return lax.dot_general(wt_ref[row:row + n, :], h, _NT, preferred_element_type=jnp.float32)

    half = HEAD_DIM // 2
    c, s = tabt_ref[0:half, :], tabt_ref[half:HEAD_DIM, :]
    qt_ref[0] = _rope_rows(proj_t(ROW_Q, Q_W), c * LOG2E, s * LOG2E).astype(qt_ref.dtype)
    qit_ref[0] = _rope_rows(proj_t(ROW_QI, QI_W), c, s).astype(qit_ref.dtype)
    vw = proj_t(ROW_V, KV_W + PACKED_ROWS)
    wit_ref[0] = vw[KV_W:KV_W + N_IDX_HEADS, :] * (N_IDX_HEADS ** -0.5)
    n = vw.shape[1]
    tail = jnp.where(lax.broadcasted_iota(jnp.int32, (V_SLOT - HEAD_DIM, n), 0) == 0, 1.0, 0.0)
    vt = jnp.concatenate(
        [piece for g in range(N_KV_HEADS) for piece in (vw[g * HEAD_DIM:(g + 1) * HEAD_DIM], tail)], axis=0)
    kbw = vt_ref.shape[3]
    for c in range(vt_ref.shape[1]):
        vt_ref[0, c] = vt[:, c * kbw:(c + 1) * kbw].astype(vt_ref.dtype)


def _proj_layer(x, g, w_rowmajor, w_transposed, tab, tabt, tm):
    B, T, D = x.shape
    blk = lambda w: pl.BlockSpec((1, tm, w), lambda b, t: (b, t, 0))
    blk_t = lambda r: pl.BlockSpec((1, r, tm), lambda b, t: (b, 0, t))
    bf, f32 = jnp.bfloat16, jnp.float32
    sds = jax.ShapeDtypeStruct
    kbw = min(tm, KEY_BLOCK)
    return pl.pallas_call(
        _proj_kernel,
        grid=(B, T // tm),
        in_specs=[
            blk(D),
            _const_spec((1, D)),
            _const_spec(w_rowmajor.shape),
            _const_spec(w_transposed.shape),
            pl.BlockSpec((tm, tab.shape[1]), lambda b, t: (t, 0)),
            pl.BlockSpec((HEAD_DIM, tm), lambda b, t: (0, t)),
        ],
        out_specs=[blk(KV_W), blk(KV_W), blk(KV_PAD_W), blk(LANES), blk(LANES),
                   blk_t(Q_W), blk_t(QI_W), blk_t(N_IDX_HEADS),
                   pl.BlockSpec((1, tm // kbw, V_ROWS, kbw), lambda b, t: (b, t, 0, 0))],
        out_shape=[sds((B, T, KV_W), f32), sds((B, T, KV_W), f32), sds((B, T, KV_PAD_W), bf),
                   sds((B, T, LANES), f32), sds((B, T, LANES), bf),
                   sds((B, Q_W, T), bf), sds((B, QI_W, T), bf), sds((B, N_IDX_HEADS, T), f32),
                   sds((B, T // kbw, V_ROWS, kbw), bf)],
        compiler_params=pltpu.CompilerParams(
            dimension_semantics=("arbitrary", "arbitrary"), vmem_limit_bytes=VMEM_LIMIT),
        name="attn_in_proj",
    )(x, g, w_rowmajor, w_transposed, tab, tabt)


def _attn_kernel(qt_ref, qit_ref, wit_ref, kp_ref, vt_ref, kib_ref, o_ref,
                 key_ref, hi_ref, lo_ref, lo2_ref, f_ref, s0_ref, s1_ref, oacc_ref, *,
                 tq, tq_real, n_keys_real, pos0, n_sel):
    kb = KEY_BLOCK
    t = pl.program_id(1)
    lane_q = lax.broadcasted_iota(jnp.int32, (1, tq), 1)
    q_limit = ((pos0 + t * tq + lane_q) // CHUNK + 1) * CHUNK
    qvalid = lane_q < tq_real
    n_reach = jnp.minimum(((pos0 + (t + 1) * tq - 1) // CHUNK + 1) * CHUNK, n_keys_real)
    nkb = (n_reach + kb - 1) // kb
    n_unvisited = jnp.maximum(n_keys_real - nkb * kb, 0)
    key_row = lax.broadcasted_iota(jnp.int32, (kb, tq), 0)

    def block_start(j):
        return pl.multiple_of(j * kb, kb)

    def for_blocks(body, carry, tiers=(2, 1)):
        start = 0
        for n in tiers:
            def trip(i, c, n=n, start=start):
                for u in range(n):
                    c = body(start + n * i + u, c)
                return c
            trips = (nkb - start) // n
            carry = lax.fori_loop(0, trips, trip, carry)
            start = start + trips * n
        return carry

    def admissible(k0):
        return key_row < q_limit - k0

    wi = wit_ref[0]

    def score_block(j, carry):
        k0 = block_start(j)
        kx = kib_ref[0, pl.ds(k0, kb), 0:IDX_DIM]
        acc = jnp.zeros((kb, tq), jnp.float32)
        for hp in range(N_IDX_HEADS // 2):
            qi2 = jnp.concatenate(
                [qit_ref[0, (2 * hp + u) * IDX_DIM:(2 * hp + u + 1) * IDX_DIM, :] for u in range(2)], axis=1)
            lg = jnp.dot(kx, qi2, preferred_element_type=jnp.float32)
            for u in range(2):
                hh = 2 * hp + u
                acc = acc + jnp.maximum(lg[:, u * tq:(u + 1) * tq], 0.0) * wi[hh:hh + 1, :]
        sc = jnp.where(admissible(k0), acc, NEG)
        if n_keys_real % kb:
            sc = jnp.where(key_row < n_keys_real - k0, sc, -jnp.inf)
        bits = lax.bitcast_convert_type(sc, jnp.int32)
        key = bits ^ ((bits >> 31) & jnp.int32(0x7FFFFFFF))
        key_ref[pl.ds(k0, kb), :] = key
        hi_ref[pl.ds(k0, kb), :] = (key >> 16).astype(jnp.int16)
        lo_ref[pl.ds(k0, kb), :] = (key ^ 0x8000).astype(jnp.int16)
        return carry

    for_blocks(score_block, 0, tiers=(4, 2, 1))

    def count16(ref, pred):
        def body(j, acc):
            hit = jnp.where(pred(ref[pl.ds(block_start(j), kb), :]), jnp.int16(1), jnp.int16(0))
            return acc + _fold_rows(jnp.add, hit, rows=PACKED_ROWS)
        acc = for_blocks(body, jnp.zeros((PACKED_ROWS, tq), jnp.int16))
        return jnp.sum(acc.astype(jnp.int32), axis=0, keepdims=True)

    def as_plane(v):
        tile = jnp.broadcast_to(v, (PACKED_ROWS, tq)).astype(jnp.int16)
        return jnp.concatenate([tile] * (kb // PACKED_ROWS), axis=0)

    def radix16(ref, n_wanted, n_all, unvisited_ge):
        def step(i, carry):
            thr, n_ge_thr, n_gt_thr = carry
            cand = thr + jnp.left_shift(jnp.int32(1), 15 - i)
            plane = as_plane(cand)
            n_ge = count16(ref, lambda blk: blk >= plane) + unvisited_ge(cand)
            take = n_ge >= n_wanted
            return jnp.where(take, cand, thr), jnp.where(take, n_ge, n_ge_thr), jnp.where(take, n_gt_thr, n_ge)
        start = (jnp.full((1, tq), -2 ** 15, jnp.int32), n_all, jnp.zeros((1, tq), jnp.int32))
        return lax.fori_loop(0, 16, step, start)

    n_all = jnp.broadcast_to(nkb * kb + n_unvisited, (1, tq))
    thr_hi, n_ge_hi, n_gt_hi = radix16(
        hi_ref, n_sel, n_all, lambda cand: jnp.where(cand <= NEG_KEY_HI, n_unvisited, 0))
    hi_plane = as_plane(thr_hi)

    def park_block(j, carry):
        rows = pl.ds(block_start(j), kb)
        lo2_ref[rows, :] = jnp.where(hi_ref[rows, :] == hi_plane, lo_ref[rows, :], jnp.int16(-2 ** 15))
        return carry

    for_blocks(park_block, 0)
    neg_in_bucket = thr_hi == NEG_KEY_HI
    thr_lo, n_ge_lo, n_gt_lo = radix16(
        lo2_ref, n_sel - n_gt_hi, n_ge_hi - n_gt_hi,
        lambda cand: jnp.where(neg_in_bucket & (cand <= NEG_KEY_LO), n_unvisited, 0))
    thr = thr_hi * 65536 + (thr_lo + 2 ** 15)
    n_gt = n_gt_hi + n_gt_lo
    n_eq = n_ge_lo - n_gt_lo
    need = n_sel - n_gt
    tie = (n_eq > need) & qvalid
    any_tie = jnp.max(tie.astype(jnp.int32)) > 0

    def mask_block(j, run, exact_ties):
        k0 = block_start(j)
        kblk = key_ref[pl.ds(k0, kb), :]
        if exact_ties:
            r = lax.broadcasted_iota(jnp.int32, (kb, kb), 0)
            c = lax.broadcasted_iota(jnp.int32, (kb, kb), 1)
            tri = jnp.where(r >= c, 1.0, 0.0).astype(jnp.bfloat16)
            eqb = kblk == thr
            eqf = jnp.where(eqb, 1.0, 0.0)
            rank = jnp.dot(tri, eqf.astype(jnp.bfloat16), preferred_element_type=jnp.float32) + run
            run = run + jnp.sum(eqf, axis=0, keepdims=True)
            sel = (kblk > thr) | (eqb & (rank <= need.astype(jnp.float32)))
        else:
            sel = kblk >= thr
        f = jnp.where(sel, jnp.where(admissible(k0), jnp.inf, NEG), -jnp.inf)
        f_ref[pl.ds(k0, kb), :] = f
        return f, run

    gw = KV_GROUP * tq
    s_refs = (s0_ref, s1_ref)

    def logits_block(g, j, f=None):
        k0 = block_start(j)
        q4 = jnp.concatenate(
            [qt_ref[0, (g * KV_GROUP + r) * HEAD_DIM:(g * KV_GROUP + r + 1) * HEAD_DIM, :] for r in range(KV_GROUP)],
            axis=1)
        kblk = kp_ref[0, pl.ds(k0, kb), g * LANES:g * LANES + HEAD_DIM]
        s = jnp.dot(kblk, q4, preferred_element_type=jnp.float32)
        if f is None:
            f = f_ref[pl.ds(k0, kb), :]
        s = jnp.minimum(s, jnp.concatenate([f] * KV_GROUP, axis=1))
        s_refs[g % 2][pl.ds(k0, kb), :] = s
        return _fold_rows(jnp.maximum, s)

    def values_block(g, j, m):
        p = jnp.exp2(s_refs[g % 2][pl.ds(block_start(j), kb), :] - m).astype(jnp.bfloat16)
        oacc_ref[...] += jnp.dot(vt_ref[0, j, g * V_SLOT:(g + 1) * V_SLOT, :], p,
                                 preferred_element_type=jnp.float32)

    m_init = jnp.full((SUBLANES, gw), -jnp.inf, jnp.float32)

    def first_pass(exact_ties):
        def body(j, carry):
            a, run = carry
            f, run = mask_block(j, run, exact_ties)
            return jnp.maximum(a, logits_block(0, j, f)), run
        return for_blocks(body, (m_init, jnp.zeros((1, tq), jnp.float32)), tiers=(4, 2, 1))[0]

    m_acc = lax.cond(any_tie, lambda: first_pass(True), lambda: first_pass(False))
    for g in range(N_KV_HEADS):
        m = jnp.max(m_acc, axis=0, keepdims=True)
        oacc_ref[...] = jnp.zeros(oacc_ref.shape, jnp.float32)

        def fused(j, a, g=g, m=m):
            if g + 1 < N_KV_HEADS:
                a = jnp.maximum(a, logits_block(g + 1, j))
            values_block(g, j, m)
            return a

        m_acc = for_blocks(fused, m_init, tiers=(4, 2, 1))
        oacc = oacc_ref[...]
        on = oacc[0:HEAD_DIM] * (1.0 / oacc[HEAD_DIM:HEAD_DIM + 1])
        for u in range(KV_GROUP // 2):
            pair = jnp.concatenate([on[:, (2 * u) * tq:(2 * u + 1) * tq], on[:, (2 * u + 1) * tq:(2 * u + 2) * tq]],
                                   axis=0)
            c0 = (g * KV_GROUP + 2 * u) * HEAD_DIM
            o_ref[0, :, c0:c0 + 2 * HEAD_DIM] = pair.T.astype(o_ref.dtype)


def _attn_layer(qt, qit, wit, kp, vt, kib, *, tq, tq_real, n_keys_real, pos0, n_sel):
    B, _, T = qt.shape
    n_keys = kp.shape[1]
    qblk = lambda r: pl.BlockSpec((1, r, tq), lambda b, t: (b, 0, t))
    kblk = lambda w: pl.BlockSpec((1, n_keys, w), lambda b, t: (b, 0, 0))
    return pl.pallas_call(
        functools.partial(_attn_kernel, tq=tq, tq_real=tq_real, n_keys_real=n_keys_real, pos0=pos0, n_sel=n_sel),
        grid=(B, T // tq),
        in_specs=[qblk(Q_W), qblk(QI_W), qblk(N_IDX_HEADS), kblk(KV_PAD_W),
                  pl.BlockSpec((1,) + vt.shape[1:], lambda b, t: (b, 0, 0, 0)), kblk(LANES)],
        out_specs=pl.BlockSpec((1, tq, Q_W), lambda b, t: (b, t, 0)),
        out_shape=jax.ShapeDtypeStruct((B, T, Q_W), jnp.bfloat16),
        scratch_shapes=[pltpu.VMEM((n_keys, tq), jnp.int32)] + [pltpu.VMEM((n_keys, tq), jnp.int16)] * 3 + [
                        pltpu.VMEM((n_keys, tq), jnp.float32),
                        pltpu.VMEM((n_keys, KV_GROUP * tq), jnp.float32),
                        pltpu.VMEM((n_keys, KV_GROUP * tq), jnp.float32),
                        pltpu.VMEM((V_SLOT, KV_GROUP * tq), jnp.float32)],
        compiler_params=pltpu.CompilerParams(
            dimension_semantics=("arbitrary", "arbitrary"), vmem_limit_bytes=VMEM_LIMIT),
        name="dsa_attention",
    )(qt, qit, wit, kp, vt, kib)


def _pad_heads(a):
    lead = a.shape[:-1]
    a = a.reshape(*lead, N_KV_HEADS, HEAD_DIM)
    a = jnp.pad(a, [(0, 0)] * len(lead) + [(0, 0), (0, LANES - HEAD_DIM)])
    return a.reshape(*lead, KV_PAD_W)


def _pack_w_in(w_in):
    o = 0
    wq = w_in[:, o:o + Q_W] * (HEAD_DIM ** -0.5); o += Q_W
    wk = w_in[:, o:o + KV_W]; o += KV_W
    wv = w_in[:, o:o + KV_W]; o += KV_W
    wqi = w_in[:, o:o + QI_W] * (IDX_DIM ** -0.5); o += QI_W
    wki = w_in[:, o:o + IDX_DIM]; o += IDX_DIM
    wwi = w_in[:, o:o + N_IDX_HEADS]
    wkw = jnp.pad(jnp.concatenate([wki, wwi], axis=1), ((0, 0), (0, LANES - IDX_DIM - N_IDX_HEADS)))
    rowmajor = jnp.concatenate([wk, wv, wkw], axis=1)
    wwi_rows = jnp.pad(wwi, ((0, 0), (0, PACKED_ROWS - N_IDX_HEADS)))
    transposed = jnp.concatenate([wq, wqi, wv, wwi_rows], axis=1).T
    return rowmajor.astype(jnp.bfloat16), transposed.astype(jnp.bfloat16)


def _rope_tables(pos):
    inv_freq = 1.0 / (ROPE_THETA ** (jnp.arange(0, HEAD_DIM, 2, dtype=jnp.float32) / HEAD_DIM))
    ang = pos.astype(jnp.float32)[:, None] * inv_freq[None, :]
    c, s = jnp.cos(ang), jnp.sin(ang)
    z = jnp.zeros_like(c)
    n = pos.shape[0]
    zpad = jnp.zeros((n, LANES - HEAD_DIM), jnp.float32)
    wi_scale = jnp.full((n, N_IDX_HEADS), N_IDX_HEADS ** -0.5, jnp.float32)
    c_kw = jnp.concatenate([c, c, wi_scale, jnp.zeros((n, LANES - HEAD_DIM - N_IDX_HEADS), jnp.float32)], axis=1)
    tab = jnp.concatenate([
        jnp.concatenate([c, c, c, c], axis=1), c_kw,
        jnp.concatenate([-s, z, -s, z], axis=1), jnp.concatenate([-s, z, zpad], axis=1),
        jnp.concatenate([z, s, z, s], axis=1), jnp.concatenate([z, s, zpad], axis=1),
    ], axis=1)
    tabt = jnp.concatenate([c, s], axis=1).T
    return tab, tabt


def _round_up(n, m):
    return (n + m - 1) // m * m


def _value_blocks_t(v):
    B, n, _ = v.shape
    nb = n // KEY_BLOCK
    vt = jnp.transpose(v.reshape(B, nb, KEY_BLOCK, N_KV_HEADS, HEAD_DIM), (0, 1, 3, 4, 2))
    ones = jnp.ones((B, nb, N_KV_HEADS, 1, KEY_BLOCK), v.dtype)
    zeros = jnp.zeros((B, nb, N_KV_HEADS, V_SLOT - HEAD_DIM - 1, KEY_BLOCK), v.dtype)
    return jnp.concatenate([vt, ones, zeros], axis=3).reshape(B, nb, V_ROWS, KEY_BLOCK)


def _trunk(x, pos0, pool_past, attn_past, norm_mix, norm_mlp, norm_final, pool_w, pool_scale,
           w_rowmajor, w_transposed, w_o, w_up, w_down):
    B, T, D = x.shape
    bf = jnp.bfloat16
    row = lambda v: v.reshape(1, D)

    if pool_past is None:
        past16 = jnp.zeros((B, POOL_STATE + 1, D), jnp.float32)
    else:
        past16 = jnp.pad(pool_past, ((0, 0), (1, 0), (0, 0)))
    tt = min(T, ROW_TILE)
    tm = min(B * T, ROW_TILE)
    x1, st = _pool_layer(x, past16, row(norm_mix[0]), pool_w[0].astype(bf), row(pool_scale[0]), pos0, tt)
    x2 = _mlp_layer(x1.reshape(B * T, D), row(norm_mlp[0]), w_up[0], w_down[0], tm, FF_CHUNK).reshape(B, T, D)
    pool_new = st[:, 1:][None]

    tab, tabt = _rope_tables(pos0 + jnp.arange(T, dtype=jnp.int32))
    kc, vc, kp, kw, kib, qt, qit, wit, vt = _proj_layer(
        x2, row(norm_mix[1]), w_rowmajor, w_transposed, tab, tabt, min(T, PROJ_TILE))
    k_new = kc.reshape(1, B, T, N_KV_HEADS, HEAD_DIM)
    v_new = vc.reshape(1, B, T, N_KV_HEADS, HEAD_DIM)
    ki_new = kw[:, :, :IDX_DIM][None]

    if attn_past is None:
        n_real = T
        kp_all, vt_all, kib_all = kp, vt, kib
    else:
        ck, cv, cki = attn_past
        P = ck.shape[1]
        n_real = P + T
        n_keys = _round_up(n_real, KEY_BLOCK)
        fill = lambda w: jnp.zeros((B, n_keys - n_real, w), bf)
        kp_all = jnp.concatenate([_pad_heads(ck.reshape(B, P, KV_W).astype(bf)), kp, fill(KV_PAD_W)], axis=1)
        kib_all = jnp.concatenate(
            [jnp.pad(cki.astype(bf), ((0, 0), (0, 0), (0, LANES - IDX_DIM))), kib, fill(LANES)], axis=1)
        vt_all = _value_blocks_t(
            jnp.concatenate([cv.reshape(B, P, KV_W).astype(bf), vc.astype(bf), fill(KV_W)], axis=1))
    tq = min(Q_TILE, _round_up(T, LANES))
    t_pad = _round_up(T, tq)
    padq = lambda a: jnp.pad(a, ((0, 0), (0, 0), (0, t_pad - T)))
    n_sel = min(TOPK_MAX, n_real // 4)
    o = _attn_layer(padq(qt), padq(qit), padq(wit), kp_all, vt_all, kib_all,
                    tq=tq, tq_real=min(T, tq), n_keys_real=n_real, pos0=pos0, n_sel=n_sel)
    o = o[:, :T].reshape(B * T, Q_W)

    y = _mlp_layer(x2.reshape(B * T, D), row(norm_mlp[1]), w_up[1], w_down[1], tm, FF_CHUNK,
                   attn=(o, w_o), final_g=row(norm_final)).reshape(B, T, D)
    return y, pool_new, k_new, v_new, ki_new


def kernel(x_prompt, x_sample, state_pool, cache_k, cache_v, cache_kidx, norm_mix, norm_mlp, norm_final,
           pool_w, pool_scale, attn_w_in, attn_w_o, mlp_w_up, mlp_w_down):
    bf = jnp.bfloat16
    shared = (norm_mix, norm_mlp, norm_final, pool_w, pool_scale, *_pack_w_in(attn_w_in[0]),
              attn_w_o[0].astype(bf), mlp_w_up.astype(bf), mlp_w_down.astype(bf))
    y_p, pool_p, k_p, v_p, ki_p = _trunk(x_prompt, 0, None, None, *shared)
    y_s, pool_s, k_s, v_s, ki_s = _trunk(
        x_sample, cache_k.shape[2], state_pool[0], (cache_k[0], cache_v[0], cache_kidx[0]), *shared)
    return (y_p, y_s, pool_p, pool_s, k_p, v_p, ki_p, k_s, v_s, ki_s)
```

```python
import functools
import math
import struct

import jax
import jax.numpy as jnp
from jax import lax
from jax.experimental import pallas as pl
from jax.experimental.pallas import tpu as pltpu

D_MODEL = 1024
CHUNK = 64
POOL_WINDOWS = (2, 4, 8, 16)
POOL_GROUP = D_MODEL // len(POOL_WINDOWS)
POOL_STATE = max(POOL_WINDOWS) - 1
N_HEADS = 16
HEAD_DIM = 64
N_KV_HEADS = 4
KV_GROUP = N_HEADS // N_KV_HEADS
N_IDX_HEADS = 8
IDX_DIM = 64
TOPK_MAX = 256
ROPE_THETA = 10000.0
D_FF = 4 * D_MODEL
EPS = 1e-6
NEG = -1e30
Q_W = N_HEADS * HEAD_DIM
KV_W = N_KV_HEADS * HEAD_DIM
QI_W = N_IDX_HEADS * IDX_DIM
LOG2E = math.log2(math.e)

SUBLANES = 8
PACKED_ROWS = 2 * SUBLANES
LANES = 128
KV_PAD_W = N_KV_HEADS * LANES
V_SLOT = HEAD_DIM + PACKED_ROWS
V_ROWS = N_KV_HEADS * V_SLOT
KEY_BLOCK = 256
Q_TILE = 256
ROW_TILE = 1024
PROJ_TILE = 1024
FF_CHUNK = 512

OFF_KC = 0
OFF_VC = OFF_KC + KV_W
OFF_KW = OFF_VC + KV_W
PROJ_ROWMAJOR_W = OFF_KW + LANES
ROW_Q = 0
ROW_QI = ROW_Q + Q_W
ROW_V = ROW_QI + QI_W
ROW_WI = ROW_V + KV_W
PROJ_TRANSPOSED_ROWS = ROW_WI + PACKED_ROWS

VMEM_LIMIT = 56 * 1024 * 1024

_NT = (((1,), (1,)), ((), ()))


def _f32_order_key(x):
    b = struct.unpack("<i", struct.pack("<f", x))[0]
    return b ^ ((b >> 31) & 0x7FFFFFFF)


def _signed16(v):
    return v - (1 << 16) if v >= (1 << 15) else v


NEG_KEY = _f32_order_key(NEG)
NEG_KEY_HI = NEG_KEY >> 16
NEG_KEY_LO = _signed16((NEG_KEY & 0xFFFF) ^ 0x8000)


def _rms(x, g):
    return x * lax.rsqrt(jnp.mean(x * x, axis=-1, keepdims=True) + EPS) * g


def _const_spec(shape):
    n = len(shape)
    return pl.BlockSpec(shape, lambda *_: (0,) * n, pipeline_mode=pl.Buffered(1))


def _tree(op, parts):
    parts = list(parts)
    while len(parts) > 1:
        parts = [op(parts[i], parts[i + 1]) if i + 1 < len(parts) else parts[i] for i in range(0, len(parts), 2)]
    return parts[0]


def _fold_rows(op, x, rows=SUBLANES):
    return _tree(op, [x[r:r + rows] for r in range(0, x.shape[0], rows)])


def _pool_kernel(x_ref, past_ref, g_ref, w_ref, scale_ref, x1_ref, st_ref, hp_ref, sa_ref, sb_ref, *, tt, pos0):
    t = pl.program_id(1)
    halo = POOL_STATE + 1
    lead = SUBLANES
    top = lead + halo

    @pl.when(t == 0)
    def _():
        zeros = jnp.zeros((lead, hp_ref.shape[1]), jnp.float32)
        hp_ref[0:lead, :] = zeros
        sa_ref[0:lead, :] = zeros
        sb_ref[0:lead, :] = zeros
        hp_ref[lead:top, :] = past_ref[0]

    x = x_ref[0]
    h = _rms(x, g_ref[...])
    hp_ref[top:top + tt, :] = h

    n = halo + tt
    g1, g2, g3 = POOL_GROUP, 2 * POOL_GROUP, 3 * POOL_GROUP
    sa_ref[lead:lead + n, :] = hp_ref[lead:lead + n, :] + hp_ref[lead - 1:lead - 1 + n, :]
    sb_ref[lead:lead + n, g1:] = sa_ref[lead:lead + n, g1:] + sa_ref[lead - 2:lead - 2 + n, g1:]
    sa_ref[lead:lead + n, g2:] = sb_ref[lead:lead + n, g2:] + sb_ref[lead - 4:lead - 4 + n, g2:]
    window_sums = (
        sa_ref[top:top + tt, 0:g1],
        sb_ref[top:top + tt, g1:g2],
        sa_ref[top:top + tt, g2:g3],
        sa_ref[top:top + tt, g3:] + sa_ref[top - 8:top - 8 + tt, g3:],
    )
    pos = pos0 + t * tt + lax.broadcasted_iota(jnp.int32, (tt, 1), 0)
    for g, win in enumerate(POOL_WINDOWS):
        c0, c1 = g * POOL_GROUP, (g + 1) * POOL_GROUP
        cnt = jnp.minimum(pos + 1, win).astype(jnp.float32)
        diff = window_sums[g] / cnt - h[:, c0:c1]
        y = jnp.dot(diff.astype(jnp.bfloat16), w_ref[g], preferred_element_type=jnp.float32)
        x1_ref[0, :, c0:c1] = x[:, c0:c1] + y * scale_ref[:, c0:c1]
    tail = hp_ref[lead + tt:top + tt, :]
    st_ref[0] = tail
    hp_ref[lead:top, :] = tail


def _pool_layer(x, past16, g, w, scale, pos0, tt):
    B, T, D = x.shape
    halo = POOL_STATE + 1
    plane = pltpu.VMEM((SUBLANES + halo + tt, D), jnp.float32)
    return pl.pallas_call(
        functools.partial(_pool_kernel, tt=tt, pos0=pos0),
        grid=(B, T // tt),
        in_specs=[
            pl.BlockSpec((1, tt, D), lambda b, t: (b, t, 0)),
            pl.BlockSpec((1, halo, D), lambda b, t: (b, 0, 0)),
            _const_spec((1, D)),
            _const_spec(w.shape),
            _const_spec((1, D)),
        ],
        out_specs=[
            pl.BlockSpec((1, tt, D), lambda b, t: (b, t, 0)),
            pl.BlockSpec((1, halo, D), lambda b, t: (b, 0, 0)),
        ],
        out_shape=[
            jax.ShapeDtypeStruct((B, T, D), jnp.float32),
            jax.ShapeDtypeStruct((B, halo, D), jnp.float32),
        ],
        scratch_shapes=[plane, plane, plane],
        compiler_params=pltpu.CompilerParams(
            dimension_semantics=("arbitrary", "arbitrary"), vmem_limit_bytes=VMEM_LIMIT),
        name="pool_mixer",
    )(x, past16, g, w, scale)


def _mlp_body(x, g_ref, up_ref, down_ref, fc):
    h = _rms(x, g_ref[...]).astype(jnp.bfloat16)
    acc = x
    for c in range(D_FF // fc):
        u = jnp.dot(h, up_ref[:, c * fc:(c + 1) * fc], preferred_element_type=jnp.float32)
        u = jnp.maximum(u, 0.0)
        a = (u * u).astype(jnp.bfloat16)
        acc = acc + jnp.dot(a, down_ref[c * fc:(c + 1) * fc, :], preferred_element_type=jnp.float32)
    return acc


def _mlp_kernel(*refs, fc, with_attn, with_final):
    refs = list(refs)
    x_ref = refs.pop(0)
    if with_attn:
        o_ref = refs.pop(0)
        wo_ref = refs.pop(0)
    g_ref = refs.pop(0)
    up_ref = refs.pop(0)
    down_ref = refs.pop(0)
    if with_final:
        gf_ref = refs.pop(0)
    out_ref = refs.pop(0)

    x = x_ref[...]
    if with_attn:
        x = x + jnp.dot(o_ref[...], wo_ref[...], preferred_element_type=jnp.float32)
    acc = _mlp_body(x, g_ref, up_ref, down_ref, fc)
    if with_final:
        acc = _rms(acc, gf_ref[...])
    out_ref[...] = acc


def _mlp_layer(x, g, up, down, tm, fc, attn=None, final_g=None):
    N, D = x.shape
    row = lambda i: (i, 0)
    args, specs = [x], [pl.BlockSpec((tm, D), row)]
    if attn is not None:
        o, wo = attn
        args += [o, wo]
        specs += [pl.BlockSpec((tm, Q_W), row), _const_spec(wo.shape)]
    args += [g, up, down]
    specs += [_const_spec((1, D)), _const_spec(up.shape), _const_spec(down.shape)]
    if final_g is not None:
        args.append(final_g)
        specs.append(_const_spec((1, D)))
    return pl.pallas_call(
        functools.partial(_mlp_kernel, fc=fc, with_attn=attn is not None, with_final=final_g is not None),
        grid=(N // tm,),
        in_specs=specs,
        out_specs=pl.BlockSpec((tm, D), row),
        out_shape=jax.ShapeDtypeStruct((N, D), jnp.float32),
        compiler_params=pltpu.CompilerParams(
            dimension_semantics=("arbitrary",), vmem_limit_bytes=VMEM_LIMIT),
        name="mlp_attn_out" if attn is not None else "mlp",
    )(*args)


def _rope_block(xb, tab_ref, special):
    o = LANES if special else 0
    c = tab_ref[:, o:o + LANES]
    s1 = tab_ref[:, 2 * LANES + o:3 * LANES + o]
    s2 = tab_ref[:, 4 * LANES + o:5 * LANES + o]
    return xb * c + pltpu.roll(xb, LANES - HEAD_DIM // 2, 1) * s1 + pltpu.roll(xb, HEAD_DIM // 2, 1) * s2


def _rope_rows(p, c, s):
    half = HEAD_DIM // 2
    r = p.reshape(p.shape[0] // HEAD_DIM, 2, half, p.shape[1])
    x1, x2 = r[:, 0], r[:, 1]
    out = jnp.stack([x1 * c - x2 * s, x2 * c + x1 * s], axis=1)
    return out.reshape(p.shape)


def _proj_kernel(x_ref, g_ref, w_ref, wt_ref, tab_ref, tabt_ref,
                 kc_ref, vc_ref, kp_ref, kw_ref, kib_ref, qt_ref, qit_ref, wit_ref, vt_ref):
    h = _rms(x_ref[0], g_ref[...]).astype(jnp.bfloat16)

    def proj(off, width):
        return jnp.dot(h, w_ref[:, off:off + width], preferred_element_type=jnp.float32)

    def rope(p, special=False):
        return jnp.concatenate(
            [_rope_block(p[:, j:j + LANES], tab_ref, special) for j in range(0, p.shape[1], LANES)], axis=-1)

    kc = rope(proj(OFF_KC, KV_W))
    kc_ref[0] = kc
    vc_ref[0] = proj(OFF_VC, KV_W)
    gap = jnp.zeros((kc.shape[0], LANES - HEAD_DIM), jnp.float32)
    kp = jnp.concatenate(
        [piece for g in range(N_KV_HEADS) for piece in (kc[:, g * HEAD_DIM:(g + 1) * HEAD_DIM], gap)], axis=1)
    kp_ref[0] = kp.astype(kp_ref.dtype)
    kw = rope(proj(OFF_KW, LANES), special=True)
    kw_ref[0] = kw
    kib_ref[0] = kw.astype(kib_ref.dtype)

    def proj_t(row, n):
        return lax.dot_general(wt_ref[row:row + n, :], h, _NT, preferred_element_type=jnp.float32)

    half = HEAD_DIM // 2
    c, s = tabt_ref[0:half, :], tabt_ref[half:HEAD_DIM, :]
    qt_ref[0] = _rope_rows(proj_t(ROW_Q, Q_W), c * LOG2E, s * LOG2E).astype(qt_ref.dtype)
    qit_ref[0] = _rope_rows(proj_t(ROW_QI, QI_W), c, s).astype(qit_ref.dtype)
    vw = proj_t(ROW_V, KV_W + PACKED_ROWS)
    wit_ref[0] = vw[KV_W:KV_W + N_IDX_HEADS, :] * (N_IDX_HEADS ** -0.5)
    n = vw.shape[1]
    tail = jnp.where(lax.broadcasted_iota(jnp.int32, (V_SLOT - HEAD_DIM, n), 0) == 0, 1.0, 0.0)
    vt = jnp.concatenate(
        [piece for g in range(N_KV_HEADS) for piece in (vw[g * HEAD_DIM:(g + 1) * HEAD_DIM], tail)], axis=0)
    kbw = vt_ref.shape[3]
    for c in range(vt_ref.shape[1]):
        vt_ref[0, c] = vt[:, c * kbw:(c + 1) * kbw].astype(vt_ref.dtype)


def _proj_layer(x, g, w_rowmajor, w_transposed, tab, tabt, tm):
    B, T, D = x.shape
    blk = lambda w: pl.BlockSpec((1, tm, w), lambda b, t: (b, t, 0))
    blk_t = lambda r: pl.BlockSpec((1, r, tm), lambda b, t: (b, 0, t))
    bf, f32 = jnp.bfloat16, jnp.float32
    sds = jax.ShapeDtypeStruct
    kbw = min(tm, KEY_BLOCK)
    return pl.pallas_call(
        _proj_kernel,
        grid=(B, T // tm),
        in_specs=[
            blk(D),
            _const_spec((1, D)),
            _const_spec(w_rowmajor.shape),
            _const_spec(w_transposed.shape),
            pl.BlockSpec((tm, tab.shape[1]), lambda b, t: (t, 0)),
            pl.BlockSpec((HEAD_DIM, tm), lambda b, t: (0, t)),
        ],
        out_specs=[blk(KV_W), blk(KV_W), blk(KV_PAD_W), blk(LANES), blk(LANES),
                   blk_t(Q_W), blk_t(QI_W), blk_t(N_IDX_HEADS),
                   pl.BlockSpec((1, tm // kbw, V_ROWS, kbw), lambda b, t: (b, t, 0, 0))],
        out_shape=[sds((B, T, KV_W), f32), sds((B, T, KV_W), f32), sds((B, T, KV_PAD_W), bf),
                   sds((B, T, LANES), f32), sds((B, T, LANES), bf),
                   sds((B, Q_W, T), bf), sds((B, QI_W, T), bf), sds((B, N_IDX_HEADS, T), f32),
                   sds((B, T // kbw, V_ROWS, kbw), bf)],
        compiler_params=pltpu.CompilerParams(
            dimension_semantics=("arbitrary", "arbitrary"), vmem_limit_bytes=VMEM_LIMIT),
        name="attn_in_proj",
    )(x, g, w_rowmajor, w_transposed, tab, tabt)


def _attn_kernel(qt_ref, qit_ref, wit_ref, kp_ref, vt_ref, kib_ref, o_ref,
                 key_ref, hi_ref, lo_ref, lo2_ref, f_ref, s0_ref, s1_ref, oacc_ref, *,
                 tq, tq_real, n_keys_real, pos0, n_sel):
    kb = KEY_BLOCK
    t = pl.program_id(1)
    lane_q = lax.broadcasted_iota(jnp.int32, (1, tq), 1)
    q_limit = ((pos0 + t * tq + lane_q) // CHUNK + 1) * CHUNK
    qvalid = lane_q < tq_real
    n_reach = jnp.minimum(((pos0 + (t + 1) * tq - 1) // CHUNK + 1) * CHUNK, n_keys_real)
    nkb = (n_reach + kb - 1) // kb
    n_unvisited = jnp.maximum(n_keys_real - nkb * kb, 0)
    key_row = lax.broadcasted_iota(jnp.int32, (kb, tq), 0)

    def block_start(j):
        return pl.multiple_of(j * kb, kb)

    def for_blocks(body, carry, tiers=(2, 1)):
        start = 0
        for n in tiers:
            def trip(i, c, n=n, start=start):
                for u in range(n):
                    c = body(start + n * i + u, c)
                return c
            trips = (nkb - start) // n
            carry = lax.fori_loop(0, trips, trip, carry)
            start = start + trips * n
        return carry

    def admissible(k0):
        return key_row < q_limit - k0

    wi = wit_ref[0]

    def score_block(j, carry):
        k0 = block_start(j)
        kx = kib_ref[0, pl.ds(k0, kb), 0:IDX_DIM]
        acc = jnp.zeros((kb, tq), jnp.float32)
        for hp in range(N_IDX_HEADS // 2):
            qi2 = jnp.concatenate(
                [qit_ref[0, (2 * hp + u) * IDX_DIM:(2 * hp + u + 1) * IDX_DIM, :] for u in range(2)], axis=1)
            lg = jnp.dot(kx, qi2, preferred_element_type=jnp.float32)
            for u in range(2):
                hh = 2 * hp + u
                acc = acc + jnp.maximum(lg[:, u * tq:(u + 1) * tq], 0.0) * wi[hh:hh + 1, :]
        sc = jnp.where(admissible(k0), acc, NEG)
        if n_keys_real % kb:
            sc = jnp.where(key_row < n_keys_real - k0, sc, -jnp.inf)
        bits = lax.bitcast_convert_type(sc, jnp.int32)
        key = bits ^ ((bits >> 31) & jnp.int32(0x7FFFFFFF))
        key_ref[pl.ds(k0, kb), :] = key
        hi = (key >> 16).astype(jnp.int16)
        hi_ref[pl.ds(k0, kb), :] = hi
        lo_ref[pl.ds(k0, kb), :] = (key ^ 0x8000).astype(jnp.int16)
        return carry + _fold_rows(jnp.add, jnp.where(hi >= 0, jnp.int16(1), jnp.int16(0)), rows=PACKED_ROWS)

    def total16(acc):
        return jnp.sum(acc.astype(jnp.int32), axis=0, keepdims=True)

    n_hi_nonneg = total16(for_blocks(score_block, jnp.zeros((PACKED_ROWS, tq), jnp.int16), tiers=(4, 2, 1)))

    def count16(ref, pred):
        def body(j, acc):
            hit = jnp.where(pred(ref[pl.ds(block_start(j), kb), :]), jnp.int16(1), jnp.int16(0))
            return acc + _fold_rows(jnp.add, hit, rows=PACKED_ROWS)
        return total16(for_blocks(body, jnp.zeros((PACKED_ROWS, tq), jnp.int16)))

    def as_plane(v):
        tile = jnp.broadcast_to(v, (PACKED_ROWS, tq)).astype(jnp.int16)
        return jnp.concatenate([tile] * (kb // PACKED_ROWS), axis=0)

    def radix16(ref, n_wanted, n_all, n_nonneg, unvisited_ge):
        def decide(cand, n_ge, carry):
            thr, n_ge_thr, n_gt_thr = carry
            take = n_ge >= n_wanted
            return jnp.where(take, cand, thr), jnp.where(take, n_ge, n_ge_thr), jnp.where(take, n_gt_thr, n_ge)

        def step(i, carry):
            cand = carry[0] + jnp.left_shift(jnp.int32(1), 15 - i)
            plane = as_plane(cand)
            return decide(cand, count16(ref, lambda blk: blk >= plane) + unvisited_ge(cand), carry)

        zero = jnp.zeros((1, tq), jnp.int32)
        start = decide(zero, n_nonneg + unvisited_ge(zero), (jnp.full((1, tq), -2 ** 15, jnp.int32), n_all, zero))
        return lax.fori_loop(1, 16, step, start)

    n_all = jnp.broadcast_to(nkb * kb + n_unvisited, (1, tq))
    thr_hi, n_ge_hi, n_gt_hi = radix16(
        hi_ref, n_sel, n_all, n_hi_nonneg, lambda cand: jnp.where(cand <= NEG_KEY_HI, n_unvisited, 0))
    hi_plane = as_plane(thr_hi)

    def park_block(j, acc):
        rows = pl.ds(block_start(j), kb)
        lo2 = jnp.where(hi_ref[rows, :] == hi_plane, lo_ref[rows, :], jnp.int16(-2 ** 15))
        lo2_ref[rows, :] = lo2
        return acc + _fold_rows(jnp.add, jnp.where(lo2 >= 0, jnp.int16(1), jnp.int16(0)), rows=PACKED_ROWS)

    n_lo_nonneg = total16(for_blocks(park_block, jnp.zeros((PACKED_ROWS, tq), jnp.int16)))
    neg_in_bucket = thr_hi == NEG_KEY_HI
    thr_lo, n_ge_lo, n_gt_lo = radix16(
        lo2_ref, n_sel - n_gt_hi, n_ge_hi - n_gt_hi, n_lo_nonneg,
        lambda cand: jnp.where(neg_in_bucket & (cand <= NEG_KEY_LO), n_unvisited, 0))
    thr = thr_hi * 65536 + (thr_lo + 2 ** 15)
    n_gt = n_gt_hi + n_gt_lo
    n_eq = n_ge_lo - n_gt_lo
    need = n_sel - n_gt
    tie = (n_eq > need) & qvalid
    any_tie = jnp.max(tie.astype(jnp.int32)) > 0

    def mask_block(j, run, exact_ties):
        k0 = block_start(j)
        kblk = key_ref[pl.ds(k0, kb), :]
        if exact_ties:
            r = lax.broadcasted_iota(jnp.int32, (kb, kb), 0)
            c = lax.broadcasted_iota(jnp.int32, (kb, kb), 1)
            tri = jnp.where(r >= c, 1.0, 0.0).astype(jnp.bfloat16)
            eqb = kblk == thr
            eqf = jnp.where(eqb, 1.0, 0.0)
            rank = jnp.dot(tri, eqf.astype(jnp.bfloat16), preferred_element_type=jnp.float32) + run
            run = run + jnp.sum(eqf, axis=0, keepdims=True)
            sel = (kblk > thr) | (eqb & (rank <= need.astype(jnp.float32)))
        else:
            sel = kblk >= thr
        f = jnp.where(sel, jnp.where(admissible(k0), jnp.inf, NEG), -jnp.inf)
        f_ref[pl.ds(k0, kb), :] = f
        return f, run

    gw = KV_GROUP * tq
    s_refs = (s0_ref, s1_ref)

    def logits_block(g, j, f=None):
        k0 = block_start(j)
        q4 = jnp.concatenate(
            [qt_ref[0, (g * KV_GROUP + r) * HEAD_DIM:(g * KV_GROUP + r + 1) * HEAD_DIM, :] for r in range(KV_GROUP)],
            axis=1)
        kblk = kp_ref[0, pl.ds(k0, kb), g * LANES:g * LANES + HEAD_DIM]
        s = jnp.dot(kblk, q4, preferred_element_type=jnp.float32)
        if f is None:
            f = f_ref[pl.ds(k0, kb), :]
        s = jnp.minimum(s, jnp.concatenate([f] * KV_GROUP, axis=1))
        s_refs[g % 2][pl.ds(k0, kb), :] = s
        return _fold_rows(jnp.maximum, s)

    def values_block(g, j, m):
        p = jnp.exp2(s_refs[g % 2][pl.ds(block_start(j), kb), :] - m).astype(jnp.bfloat16)
        oacc_ref[...] += jnp.dot(vt_ref[0, j, g * V_SLOT:(g + 1) * V_SLOT, :], p,
                                 preferred_element_type=jnp.float32)

    m_init = jnp.full((SUBLANES, gw), -jnp.inf, jnp.float32)

    def first_pass(exact_ties):
        def body(j, carry):
            a, run = carry
            f, run = mask_block(j, run, exact_ties)
            return jnp.maximum(a, logits_block(0, j, f)), run
        return for_blocks(body, (m_init, jnp.zeros((1, tq), jnp.float32)), tiers=(4, 2, 1))[0]

    m_acc = lax.cond(any_tie, lambda: first_pass(True), lambda: first_pass(False))
    for g in range(N_KV_HEADS):
        m = jnp.max(m_acc, axis=0, keepdims=True)
        oacc_ref[...] = jnp.zeros(oacc_ref.shape, jnp.float32)

        def fused(j, a, g=g, m=m):
            if g + 1 < N_KV_HEADS:
                a = jnp.maximum(a, logits_block(g + 1, j))
            values_block(g, j, m)
            return a

        m_acc = for_blocks(fused, m_init, tiers=(4, 2, 1))
        oacc = oacc_ref[...]
        on = oacc[0:HEAD_DIM] * (1.0 / oacc[HEAD_DIM:HEAD_DIM + 1])
        for u in range(KV_GROUP // 2):
            pair = jnp.concatenate([on[:, (2 * u) * tq:(2 * u + 1) * tq], on[:, (2 * u + 1) * tq:(2 * u + 2) * tq]],
                                   axis=0)
            c0 = (g * KV_GROUP + 2 * u) * HEAD_DIM
            o_ref[0, :, c0:c0 + 2 * HEAD_DIM] = pair.T.astype(o_ref.dtype)


def _attn_layer(qt, qit, wit, kp, vt, kib, *, tq, tq_real, n_keys_real, pos0, n_sel):
    B, _, T = qt.shape
    n_keys = kp.shape[1]
    qblk = lambda r: pl.BlockSpec((1, r, tq), lambda b, t: (b, 0, t))
    kblk = lambda w: pl.BlockSpec((1, n_keys, w), lambda b, t: (b, 0, 0))
    return pl.pallas_call(
        functools.partial(_attn_kernel, tq=tq, tq_real=tq_real, n_keys_real=n_keys_real, pos0=pos0, n_sel=n_sel),
        grid=(B, T // tq),
        in_specs=[qblk(Q_W), qblk(QI_W), qblk(N_IDX_HEADS), kblk(KV_PAD_W),
                  pl.BlockSpec((1,) + vt.shape[1:], lambda b, t: (b, 0, 0, 0)), kblk(LANES)],
        out_specs=pl.BlockSpec((1, tq, Q_W), lambda b, t: (b, t, 0)),
        out_shape=jax.ShapeDtypeStruct((B, T, Q_W), jnp.bfloat16),
        scratch_shapes=[pltpu.VMEM((n_keys, tq), jnp.int32)] + [pltpu.VMEM((n_keys, tq), jnp.int16)] * 3 + [
                        pltpu.VMEM((n_keys, tq), jnp.float32),
                        pltpu.VMEM((n_keys, KV_GROUP * tq), jnp.float32),
                        pltpu.VMEM((n_keys, KV_GROUP * tq), jnp.float32),
                        pltpu.VMEM((V_SLOT, KV_GROUP * tq), jnp.float32)],
        compiler_params=pltpu.CompilerParams(
            dimension_semantics=("arbitrary", "arbitrary"), vmem_limit_bytes=VMEM_LIMIT),
        name="dsa_attention",
    )(qt, qit, wit, kp, vt, kib)


def _pad_heads(a):
    lead = a.shape[:-1]
    a = a.reshape(*lead, N_KV_HEADS, HEAD_DIM)
    a = jnp.pad(a, [(0, 0)] * len(lead) + [(0, 0), (0, LANES - HEAD_DIM)])
    return a.reshape(*lead, KV_PAD_W)


def _pack_w_in(w_in):
    o = 0
    wq = w_in[:, o:o + Q_W] * (HEAD_DIM ** -0.5); o += Q_W
    wk = w_in[:, o:o + KV_W]; o += KV_W
    wv = w_in[:, o:o + KV_W]; o += KV_W
    wqi = w_in[:, o:o + QI_W] * (IDX_DIM ** -0.5); o += QI_W
    wki = w_in[:, o:o + IDX_DIM]; o += IDX_DIM
    wwi = w_in[:, o:o + N_IDX_HEADS]
    wkw = jnp.pad(jnp.concatenate([wki, wwi], axis=1), ((0, 0), (0, LANES - IDX_DIM - N_IDX_HEADS)))
    rowmajor = jnp.concatenate([wk, wv, wkw], axis=1)
    wwi_rows = jnp.pad(wwi, ((0, 0), (0, PACKED_ROWS - N_IDX_HEADS)))
    transposed = jnp.concatenate([wq, wqi, wv, wwi_rows], axis=1).T
    return rowmajor.astype(jnp.bfloat16), transposed.astype(jnp.bfloat16)


def _rope_tables(pos):
    inv_freq = 1.0 / (ROPE_THETA ** (jnp.arange(0, HEAD_DIM, 2, dtype=jnp.float32) / HEAD_DIM))
    ang = pos.astype(jnp.float32)[:, None] * inv_freq[None, :]
    c, s = jnp.cos(ang), jnp.sin(ang)
    z = jnp.zeros_like(c)
    n = pos.shape[0]
    zpad = jnp.zeros((n, LANES - HEAD_DIM), jnp.float32)
    wi_scale = jnp.full((n, N_IDX_HEADS), N_IDX_HEADS ** -0.5, jnp.float32)
    c_kw = jnp.concatenate([c, c, wi_scale, jnp.zeros((n, LANES - HEAD_DIM - N_IDX_HEADS), jnp.float32)], axis=1)
    tab = jnp.concatenate([
        jnp.concatenate([c, c, c, c], axis=1), c_kw,
        jnp.concatenate([-s, z, -s, z], axis=1), jnp.concatenate([-s, z, zpad], axis=1),
        jnp.concatenate([z, s, z, s], axis=1), jnp.concatenate([z, s, zpad], axis=1),
    ], axis=1)
    tabt = jnp.concatenate([c, s], axis=1).T
    return tab, tabt


def _round_up(n, m):
    return (n + m - 1) // m * m


def _value_blocks_t(v):
    B, n, _ = v.shape
    nb = n // KEY_BLOCK
    vt = jnp.transpose(v.reshape(B, nb, KEY_BLOCK, N_KV_HEADS, HEAD_DIM), (0, 1, 3, 4, 2))
    ones = jnp.ones((B, nb, N_KV_HEADS, 1, KEY_BLOCK), v.dtype)
    zeros = jnp.zeros((B, nb, N_KV_HEADS, V_SLOT - HEAD_DIM - 1, KEY_BLOCK), v.dtype)
    return jnp.concatenate([vt, ones, zeros], axis=3).reshape(B, nb, V_ROWS, KEY_BLOCK)


def _trunk(x, pos0, pool_past, attn_past, norm_mix, norm_mlp, norm_final, pool_w, pool_scale,
           w_rowmajor, w_transposed, w_o, w_up, w_down):
    B, T, D = x.shape
    bf = jnp.bfloat16
    row = lambda v: v.reshape(1, D)

    if pool_past is None:
        past16 = jnp.zeros((B, POOL_STATE + 1, D), jnp.float32)
    else:
        past16 = jnp.pad(pool_past, ((0, 0), (1, 0), (0, 0)))
    tt = min(T, ROW_TILE)
    tm = min(B * T, ROW_TILE)
    x1, st = _pool_layer(x, past16, row(norm_mix[0]), pool_w[0].astype(bf), row(pool_scale[0]), pos0, tt)
    x2 = _mlp_layer(x1.reshape(B * T, D), row(norm_mlp[0]), w_up[0], w_down[0], tm, FF_CHUNK).reshape(B, T, D)
    pool_new = st[:, 1:][None]

    tab, tabt = _rope_tables(pos0 + jnp.arange(T, dtype=jnp.int32))
    kc, vc, kp, kw, kib, qt, qit, wit, vt = _proj_layer(
        x2, row(norm_mix[1]), w_rowmajor, w_transposed, tab, tabt, min(T, PROJ_TILE))
    k_new = kc.reshape(1, B, T, N_KV_HEADS, HEAD_DIM)
    v_new = vc.reshape(1, B, T, N_KV_HEADS, HEAD_DIM)
    ki_new = kw[:, :, :IDX_DIM][None]

    if attn_past is None:
        n_real = T
        kp_all, vt_all, kib_all = kp, vt, kib
    else:
        ck, cv, cki = attn_past
        P = ck.shape[1]
        n_real = P + T
        n_keys = _round_up(n_real, KEY_BLOCK)
        fill = lambda w: jnp.zeros((B, n_keys - n_real, w), bf)
        kp_all = jnp.concatenate([_pad_heads(ck.reshape(B, P, KV_W).astype(bf)), kp, fill(KV_PAD_W)], axis=1)
        kib_all = jnp.concatenate(
            [jnp.pad(cki.astype(bf), ((0, 0), (0, 0), (0, LANES - IDX_DIM))), kib, fill(LANES)], axis=1)
        vt_all = _value_blocks_t(
            jnp.concatenate([cv.reshape(B, P, KV_W).astype(bf), vc.astype(bf), fill(KV_W)], axis=1))
    tq = min(Q_TILE, _round_up(T, LANES))
    t_pad = _round_up(T, tq)
    padq = lambda a: jnp.pad(a, ((0, 0), (0, 0), (0, t_pad - T)))
    n_sel = min(TOPK_MAX, n_real // 4)
    o = _attn_layer(padq(qt), padq(qit), padq(wit), kp_all, vt_all, kib_all,
                    tq=tq, tq_real=min(T, tq), n_keys_real=n_real, pos0=pos0, n_sel=n_sel)
    o = o[:, :T].reshape(B * T, Q_W)

    y = _mlp_layer(x2.reshape(B * T, D), row(norm_mlp[1]), w_up[1], w_down[1], tm, FF_CHUNK,
                   attn=(o, w_o), final_g=row(norm_final)).reshape(B, T, D)
    return y, pool_new, k_new, v_new, ki_new


def kernel(x_prompt, x_sample, state_pool, cache_k, cache_v, cache_kidx, norm_mix, norm_mlp, norm_final,
           pool_w, pool_scale, attn_w_in, attn_w_o, mlp_w_up, mlp_w_down):
    bf = jnp.bfloat16
    shared = (norm_mix, norm_mlp, norm_final, pool_w, pool_scale, *_pack_w_in(attn_w_in[0]),
              attn_w_o[0].astype(bf), mlp_w_up.astype(bf), mlp_w_down.astype(bf))
    y_p, pool_p, k_p, v_p, ki_p = _trunk(x_prompt, 0, None, None, *shared)
    y_s, pool_s, k_s, v_s, ki_s = _trunk(
        x_sample, cache_k.shape[2], state_pool[0], (cache_k[0], cache_v[0], cache_kidx[0]), *shared)
    return (y_p, y_s, pool_p, pool_s, k_p, v_p, ki_p, k_s, v_s, ki_s)
```

```python
import functools
import math
import struct

import jax
import jax.numpy as jnp
from jax import lax
from jax.experimental import pallas as pl
from jax.experimental.pallas import tpu as pltpu

D_MODEL = 1024
CHUNK = 64
POOL_WINDOWS = (2, 4, 8, 16)
POOL_GROUP = D_MODEL // len(POOL_WINDOWS)
POOL_STATE = max(POOL_WINDOWS) - 1
N_HEADS = 16
HEAD_DIM = 64
N_KV_HEADS = 4
KV_GROUP = N_HEADS // N_KV_HEADS
N_IDX_HEADS = 8
IDX_DIM = 64
TOPK_MAX = 256
ROPE_THETA = 10000.0
D_FF = 4 * D_MODEL
EPS = 1e-6
NEG = -1e30
Q_W = N_HEADS * HEAD_DIM
KV_W = N_KV_HEADS * HEAD_DIM
QI_W = N_IDX_HEADS * IDX_DIM
LOG2E = math.log2(math.e)

SUBLANES = 8
PACKED_ROWS = 2 * SUBLANES
LANES = 128
KV_PAD_W = N_KV_HEADS * LANES
V_SLOT = HEAD_DIM + PACKED_ROWS
V_ROWS = N_KV_HEADS * V_SLOT
KEY_BLOCK = 256
Q_TILE = 256
ROW_TILE = 1024
PROJ_TILE = 1024
FF_CHUNK = 512

OFF_KC = 0
OFF_VC = OFF_KC + KV_W
OFF_KW = OFF_VC + KV_W
PROJ_ROWMAJOR_W = OFF_KW + LANES
ROW_Q = 0
ROW_QI = ROW_Q + Q_W
ROW_V = ROW_QI + QI_W
ROW_WI = ROW_V + KV_W
PROJ_TRANSPOSED_ROWS = ROW_WI + PACKED_ROWS

VMEM_LIMIT = 56 * 1024 * 1024

_NT = (((1,), (1,)), ((), ()))


def _f32_order_key(x):
    b = struct.unpack("<i", struct.pack("<f", x))[0]
    return b ^ ((b >> 31) & 0x7FFFFFFF)


def _signed16(v):
    return v - (1 << 16) if v >= (1 << 15) else v


NEG_KEY = _f32_order_key(NEG)
NEG_KEY_HI = NEG_KEY >> 16
NEG_KEY_LO = _signed16((NEG_KEY & 0xFFFF) ^ 0x8000)


def _rms(x, g):
    return x * lax.rsqrt(jnp.mean(x * x, axis=-1, keepdims=True) + EPS) * g


def _const_spec(shape):
    n = len(shape)
    return pl.BlockSpec(shape, lambda *_: (0,) * n, pipeline_mode=pl.Buffered(1))


def _tree(op, parts):
    parts = list(parts)
    while len(parts) > 1:
        parts = [op(parts[i], parts[i + 1]) if i + 1 < len(parts) else parts[i] for i in range(0, len(parts), 2)]
    return parts[0]


def _fold_rows(op, x, rows=SUBLANES):
    return _tree(op, [x[r:r + rows] for r in range(0, x.shape[0], rows)])


def _pool_kernel(x_ref, past_ref, g_ref, w_ref, scale_ref, x1_ref, st_ref, hp_ref, sa_ref, sb_ref, *, tt, pos0):
    t = pl.program_id(1)
    halo = POOL_STATE + 1
    lead = SUBLANES
    top = lead + halo

    @pl.when(t == 0)
    def _():
        zeros = jnp.zeros((lead, hp_ref.shape[1]), jnp.float32)
        hp_ref[0:lead, :] = zeros
        sa_ref[0:lead, :] = zeros
        sb_ref[0:lead, :] = zeros
        hp_ref[lead:top, :] = past_ref[0]

    x = x_ref[0]
    h = _rms(x, g_ref[...])
    hp_ref[top:top + tt, :] = h

    n = halo + tt
    g1, g2, g3 = POOL_GROUP, 2 * POOL_GROUP, 3 * POOL_GROUP
    sa_ref[lead:lead + n, :] = hp_ref[lead:lead + n, :] + hp_ref[lead - 1:lead - 1 + n, :]
    sb_ref[lead:lead + n, g1:] = sa_ref[lead:lead + n, g1:] + sa_ref[lead - 2:lead - 2 + n, g1:]
    sa_ref[lead:lead + n, g2:] = sb_ref[lead:lead + n, g2:] + sb_ref[lead - 4:lead - 4 + n, g2:]
    window_sums = (
        sa_ref[top:top + tt, 0:g1],
        sb_ref[top:top + tt, g1:g2],
        sa_ref[top:top + tt, g2:g3],
        sa_ref[top:top + tt, g3:] + sa_ref[top - 8:top - 8 + tt, g3:],
    )
    pos = pos0 + t * tt + lax.broadcasted_iota(jnp.int32, (tt, 1), 0)
    for g, win in enumerate(POOL_WINDOWS):
        c0, c1 = g * POOL_GROUP, (g + 1) * POOL_GROUP
        cnt = jnp.minimum(pos + 1, win).astype(jnp.float32)
        diff = window_sums[g] / cnt - h[:, c0:c1]
        y = jnp.dot(diff.astype(jnp.bfloat16), w_ref[g], preferred_element_type=jnp.float32)
        x1_ref[0, :, c0:c1] = x[:, c0:c1] + y * scale_ref[:, c0:c1]
    tail = hp_ref[lead + tt:top + tt, :]
    st_ref[0] = tail
    hp_ref[lead:top, :] = tail


def _pool_layer(x, past16, g, w, scale, pos0, tt):
    B, T, D = x.shape
    assert T % tt == 0 and D == D_MODEL, (x.shape, tt)
    halo = POOL_STATE + 1
    plane = pltpu.VMEM((SUBLANES + halo + tt, D), jnp.float32)
    return pl.pallas_call(
        functools.partial(_pool_kernel, tt=tt, pos0=pos0),
        grid=(B, T // tt),
        in_specs=[
            pl.BlockSpec((1, tt, D), lambda b, t: (b, t, 0)),
            pl.BlockSpec((1, halo, D), lambda b, t: (b, 0, 0)),
            _const_spec((1, D)),
            _const_spec(w.shape),
            _const_spec((1, D)),
        ],
        out_specs=[
            pl.BlockSpec((1, tt, D), lambda b, t: (b, t, 0)),
            pl.BlockSpec((1, halo, D), lambda b, t: (b, 0, 0)),
        ],
        out_shape=[
            jax.ShapeDtypeStruct((B, T, D), jnp.float32),
            jax.ShapeDtypeStruct((B, halo, D), jnp.float32),
        ],
        scratch_shapes=[plane, plane, plane],
        compiler_params=pltpu.CompilerParams(
            dimension_semantics=("arbitrary", "arbitrary"), vmem_limit_bytes=VMEM_LIMIT),
        name="pool_mixer",
    )(x, past16, g, w, scale)


def _mlp_body(x, g_ref, up_ref, down_ref, fc):
    h = _rms(x, g_ref[...]).astype(jnp.bfloat16)
    acc = x
    for c in range(D_FF // fc):
        u = jnp.dot(h, up_ref[:, c * fc:(c + 1) * fc], preferred_element_type=jnp.float32)
        u = jnp.maximum(u, 0.0)
        a = (u * u).astype(jnp.bfloat16)
        acc = acc + jnp.dot(a, down_ref[c * fc:(c + 1) * fc, :], preferred_element_type=jnp.float32)
    return acc


def _mlp_kernel(*refs, fc, with_attn, with_final):
    refs = list(refs)
    x_ref = refs.pop(0)
    if with_attn:
        o_ref = refs.pop(0)
        wo_ref = refs.pop(0)
    g_ref = refs.pop(0)
    up_ref = refs.pop(0)
    down_ref = refs.pop(0)
    if with_final:
        gf_ref = refs.pop(0)
    out_ref = refs.pop(0)

    x = x_ref[...]
    if with_attn:
        x = x + jnp.dot(o_ref[...], wo_ref[...], preferred_element_type=jnp.float32)
    acc = _mlp_body(x, g_ref, up_ref, down_ref, fc)
    if with_final:
        acc = _rms(acc, gf_ref[...])
    out_ref[...] = acc


def _mlp_layer(x, g, up, down, tm, fc, attn=None, final_g=None):
    N, D = x.shape
    assert N % tm == 0 and D_FF % fc == 0, (x.shape, tm, fc)
    row = lambda i: (i, 0)
    args, specs = [x], [pl.BlockSpec((tm, D), row)]
    if attn is not None:
        o, wo = attn
        args += [o, wo]
        specs += [pl.BlockSpec((tm, Q_W), row), _const_spec(wo.shape)]
    args += [g, up, down]
    specs += [_const_spec((1, D)), _const_spec(up.shape), _const_spec(down.shape)]
    if final_g is not None:
        args.append(final_g)
        specs.append(_const_spec((1, D)))
    return pl.pallas_call(
        functools.partial(_mlp_kernel, fc=fc, with_attn=attn is not None, with_final=final_g is not None),
        grid=(N // tm,),
        in_specs=specs,
        out_specs=pl.BlockSpec((tm, D), row),
        out_shape=jax.ShapeDtypeStruct((N, D), jnp.float32),
        compiler_params=pltpu.CompilerParams(
            dimension_semantics=("arbitrary",), vmem_limit_bytes=VMEM_LIMIT),
        name="mlp_attn_out" if attn is not None else "mlp",
    )(*args)


def _rope_block(xb, tab_ref, special):
    o = LANES if special else 0
    c = tab_ref[:, o:o + LANES]
    s1 = tab_ref[:, 2 * LANES + o:3 * LANES + o]
    s2 = tab_ref[:, 4 * LANES + o:5 * LANES + o]
    return xb * c + pltpu.roll(xb, LANES - HEAD_DIM // 2, 1) * s1 + pltpu.roll(xb, HEAD_DIM // 2, 1) * s2


def _rope_rows(p, c, s):
    half = HEAD_DIM // 2
    r = p.reshape(p.shape[0] // HEAD_DIM, 2, half, p.shape[1])
    x1, x2 = r[:, 0], r[:, 1]
    out = jnp.stack([x1 * c - x2 * s, x2 * c + x1 * s], axis=1)
    return out.reshape(p.shape)


def _proj_kernel(x_ref, g_ref, w_ref, wt_ref, tab_ref, tabt_ref,
                 kc_ref, vc_ref, kp_ref, kw_ref, kib_ref, qt_ref, qit_ref, wit_ref, vt_ref):
    h = _rms(x_ref[0], g_ref[...]).astype(jnp.bfloat16)

    def proj(off, width):
        return jnp.dot(h, w_ref[:, off:off + width], preferred_element_type=jnp.float32)

    def rope(p, special=False):
        return jnp.concatenate(
            [_rope_block(p[:, j:j + LANES], tab_ref, special) for j in range(0, p.shape[1], LANES)], axis=-1)

    kc = rope(proj(OFF_KC, KV_W))
    kc_ref[0] = kc
    vc_ref[0] = proj(OFF_VC, KV_W)
    gap = jnp.zeros((kc.shape[0], LANES - HEAD_DIM), jnp.float32)
    kp = jnp.concatenate(
        [piece for g in range(N_KV_HEADS) for piece in (kc[:, g * HEAD_DIM:(g + 1) * HEAD_DIM], gap)], axis=1)
    kp_ref[0] = kp.astype(kp_ref.dtype)
    kw = rope(proj(OFF_KW, LANES), special=True)
    kw_ref[0] = kw
    kib_ref[0] = kw.astype(kib_ref.dtype)

    def proj_t(row, n):
        return lax.dot_general(wt_ref[row:row + n, :], h, _NT, preferred_element_type=jnp.float32)

    half = HEAD_DIM // 2
    c, s = tabt_ref[0:half, :], tabt_ref[half:HEAD_DIM, :]
    qt_ref[0] = _rope_rows(proj_t(ROW_Q, Q_W), c * LOG2E, s * LOG2E).astype(qt_ref.dtype)
    qit_ref[0] = _rope_rows(proj_t(ROW_QI, QI_W), c, s).astype(qit_ref.dtype)
    vw = proj_t(ROW_V, KV_W + PACKED_ROWS)
    wit_ref[0] = vw[KV_W:KV_W + N_IDX_HEADS, :] * (N_IDX_HEADS ** -0.5)
    n = vw.shape[1]
    tail = jnp.where(lax.broadcasted_iota(jnp.int32, (V_SLOT - HEAD_DIM, n), 0) == 0, 1.0, 0.0)
    vt = jnp.concatenate(
        [piece for g in range(N_KV_HEADS) for piece in (vw[g * HEAD_DIM:(g + 1) * HEAD_DIM], tail)], axis=0)
    kbw = vt_ref.shape[3]
    for c in range(vt_ref.shape[1]):
        vt_ref[0, c] = vt[:, c * kbw:(c + 1) * kbw].astype(vt_ref.dtype)


def _proj_layer(x, g, w_rowmajor, w_transposed, tab, tabt, tm):
    B, T, D = x.shape
    blk = lambda w: pl.BlockSpec((1, tm, w), lambda b, t: (b, t, 0))
    blk_t = lambda r: pl.BlockSpec((1, r, tm), lambda b, t: (b, 0, t))
    bf, f32 = jnp.bfloat16, jnp.float32
    sds = jax.ShapeDtypeStruct
    kbw = min(tm, KEY_BLOCK)
    assert T % tm == 0 and tm % kbw == 0, (x.shape, tm)
    return pl.pallas_call(
        _proj_kernel,
        grid=(B, T // tm),
        in_specs=[
            blk(D),
            _const_spec((1, D)),
            _const_spec(w_rowmajor.shape),
            _const_spec(w_transposed.shape),
            pl.BlockSpec((tm, tab.shape[1]), lambda b, t: (t, 0)),
            pl.BlockSpec((HEAD_DIM, tm), lambda b, t: (0, t)),
        ],
        out_specs=[blk(KV_W), blk(KV_W), blk(KV_PAD_W), blk(LANES), blk(LANES),
                   blk_t(Q_W), blk_t(QI_W), blk_t(N_IDX_HEADS),
                   pl.BlockSpec((1, tm // kbw, V_ROWS, kbw), lambda b, t: (b, t, 0, 0))],
        out_shape=[sds((B, T, KV_W), f32), sds((B, T, KV_W), f32), sds((B, T, KV_PAD_W), bf),
                   sds((B, T, LANES), f32), sds((B, T, LANES), bf),
                   sds((B, Q_W, T), bf), sds((B, QI_W, T), bf), sds((B, N_IDX_HEADS, T), f32),
                   sds((B, T // kbw, V_ROWS, kbw), bf)],
        compiler_params=pltpu.CompilerParams(
            dimension_semantics=("arbitrary", "arbitrary"), vmem_limit_bytes=VMEM_LIMIT),
        name="attn_in_proj",
    )(x, g, w_rowmajor, w_transposed, tab, tabt)


def _attn_kernel(qt_ref, qit_ref, wit_ref, kp_ref, vt_ref, kib_ref, o_ref,
                 key_ref, hi_ref, lo_ref, lo2_ref, f_ref, s0_ref, s1_ref, oacc_ref, *,
                 tq, tq_real, n_keys_real, pos0, n_sel):
    kb = KEY_BLOCK
    t = pl.program_id(1)
    lane_q = lax.broadcasted_iota(jnp.int32, (1, tq), 1)
    q_limit = ((pos0 + t * tq + lane_q) // CHUNK + 1) * CHUNK
    qvalid = lane_q < tq_real
    n_reach = jnp.minimum(((pos0 + (t + 1) * tq - 1) // CHUNK + 1) * CHUNK, n_keys_real)
    nkb = (n_reach + kb - 1) // kb
    n_unvisited = jnp.maximum(n_keys_real - nkb * kb, 0)
    key_row = lax.broadcasted_iota(jnp.int32, (kb, tq), 0)

    def block_start(j):
        return pl.multiple_of(j * kb, kb)

    def for_blocks(body, carry, tiers=(2, 1)):
        start = 0
        for n in tiers:
            def trip(i, c, n=n, start=start):
                for u in range(n):
                    c = body(start + n * i + u, c)
                return c
            trips = (nkb - start) // n
            carry = lax.fori_loop(0, trips, trip, carry)
            start = start + trips * n
        return carry

    def admissible(k0):
        return key_row < q_limit - k0

    wi = wit_ref[0]

    def score_block(j, carry):
        k0 = block_start(j)
        kx = kib_ref[0, pl.ds(k0, kb), 0:IDX_DIM]
        acc = jnp.zeros((kb, tq), jnp.float32)
        for hp in range(N_IDX_HEADS // 2):
            qi2 = jnp.concatenate(
                [qit_ref[0, (2 * hp + u) * IDX_DIM:(2 * hp + u + 1) * IDX_DIM, :] for u in range(2)], axis=1)
            lg = jnp.dot(kx, qi2, preferred_element_type=jnp.float32)
            for u in range(2):
                hh = 2 * hp + u
                acc = acc + jnp.maximum(lg[:, u * tq:(u + 1) * tq], 0.0) * wi[hh:hh + 1, :]
        sc = jnp.where(admissible(k0), acc, NEG)
        if n_keys_real % kb:
            sc = jnp.where(key_row < n_keys_real - k0, sc, -jnp.inf)
        bits = lax.bitcast_convert_type(sc, jnp.int32)
        key = bits ^ ((bits >> 31) & jnp.int32(0x7FFFFFFF))
        key_ref[pl.ds(k0, kb), :] = key
        hi_ref[pl.ds(k0, kb), :] = (key >> 16).astype(jnp.int16)
        lo_ref[pl.ds(k0, kb), :] = (key ^ 0x8000).astype(jnp.int16)
        return carry

    for_blocks(score_block, 0, tiers=(4, 2, 1))

    def count16(ref, pred):
        def body(j, acc):
            hit = jnp.where(pred(ref[pl.ds(block_start(j), kb), :]), jnp.int16(1), jnp.int16(0))
            return acc + _fold_rows(jnp.add, hit, rows=PACKED_ROWS)
        acc = for_blocks(body, jnp.zeros((PACKED_ROWS, tq), jnp.int16))
        return jnp.sum(acc.astype(jnp.int32), axis=0, keepdims=True)

    def as_plane(v):
        tile = jnp.broadcast_to(v, (PACKED_ROWS, tq)).astype(jnp.int16)
        return jnp.concatenate([tile] * (kb // PACKED_ROWS), axis=0)

    def radix16(ref, n_wanted, n_all, unvisited_ge):
        def step(i, carry):
            thr, n_ge_thr, n_gt_thr = carry
            cand = thr + jnp.left_shift(jnp.int32(1), 15 - i)
            plane = as_plane(cand)
            n_ge = count16(ref, lambda blk: blk >= plane) + unvisited_ge(cand)
            take = n_ge >= n_wanted
            return jnp.where(take, cand, thr), jnp.where(take, n_ge, n_ge_thr), jnp.where(take, n_gt_thr, n_ge)
        start = (jnp.full((1, tq), -2 ** 15, jnp.int32), n_all, jnp.zeros((1, tq), jnp.int32))
        return lax.fori_loop(0, 16, step, start)

    n_all = jnp.broadcast_to(nkb * kb + n_unvisited, (1, tq))
    thr_hi, n_ge_hi, n_gt_hi = radix16(
        hi_ref, n_sel, n_all, lambda cand: jnp.where(cand <= NEG_KEY_HI, n_unvisited, 0))
    hi_plane = as_plane(thr_hi)

    def park_block(j, carry):
        rows = pl.ds(block_start(j), kb)
        lo2_ref[rows, :] = jnp.where(hi_ref[rows, :] == hi_plane, lo_ref[rows, :], jnp.int16(-2 ** 15))
        return carry

    for_blocks(park_block, 0)
    neg_in_bucket = thr_hi == NEG_KEY_HI
    thr_lo, n_ge_lo, n_gt_lo = radix16(
        lo2_ref, n_sel - n_gt_hi, n_ge_hi - n_gt_hi,
        lambda cand: jnp.where(neg_in_bucket & (cand <= NEG_KEY_LO), n_unvisited, 0))
    thr = thr_hi * 65536 + (thr_lo + 2 ** 15)
    n_gt = n_gt_hi + n_gt_lo
    n_eq = n_ge_lo - n_gt_lo
    need = n_sel - n_gt
    tie = (n_eq > need) & qvalid
    any_tie = jnp.max(tie.astype(jnp.int32)) > 0

    def mask_block(j, run, exact_ties):
        k0 = block_start(j)
        kblk = key_ref[pl.ds(k0, kb), :]
        if exact_ties:
            r = lax.broadcasted_iota(jnp.int32, (kb, kb), 0)
            c = lax.broadcasted_iota(jnp.int32, (kb, kb), 1)
            tri = jnp.where(r >= c, 1.0, 0.0).astype(jnp.bfloat16)
            eqb = kblk == thr
            eqf = jnp.where(eqb, 1.0, 0.0)
            rank = jnp.dot(tri, eqf.astype(jnp.bfloat16), preferred_element_type=jnp.float32) + run
            run = run + jnp.sum(eqf, axis=0, keepdims=True)
            sel = (kblk > thr) | (eqb & (rank <= need.astype(jnp.float32)))
        else:
            sel = kblk >= thr
        f = jnp.where(sel, jnp.where(admissible(k0), jnp.inf, NEG), -jnp.inf)
        f_ref[pl.ds(k0, kb), :] = f
        return f, run

    gw = KV_GROUP * tq
    s_refs = (s0_ref, s1_ref)

    def logits_block(g, j, f=None):
        k0 = block_start(j)
        q4 = jnp.concatenate(
            [qt_ref[0, (g * KV_GROUP + r) * HEAD_DIM:(g * KV_GROUP + r + 1) * HEAD_DIM, :] for r in range(KV_GROUP)],
            axis=1)
        kblk = kp_ref[0, pl.ds(k0, kb), g * LANES:g * LANES + HEAD_DIM]
        s = jnp.dot(kblk, q4, preferred_element_type=jnp.float32)
        if f is None:
            f = f_ref[pl.ds(k0, kb), :]
        s = jnp.minimum(s, jnp.concatenate([f] * KV_GROUP, axis=1))
        s_refs[g % 2][pl.ds(k0, kb), :] = s
        return _fold_rows(jnp.maximum, s)

    def values_block(g, j, m):
        p = jnp.exp2(s_refs[g % 2][pl.ds(block_start(j), kb), :] - m).astype(jnp.bfloat16)
        oacc_ref[...] += jnp.dot(vt_ref[0, j, g * V_SLOT:(g + 1) * V_SLOT, :], p,
                                 preferred_element_type=jnp.float32)

    m_init = jnp.full((SUBLANES, gw), -jnp.inf, jnp.float32)

    def first_pass(exact_ties):
        def body(j, carry):
            a, run = carry
            f, run = mask_block(j, run, exact_ties)
            return jnp.maximum(a, logits_block(0, j, f)), run
        return for_blocks(body, (m_init, jnp.zeros((1, tq), jnp.float32)), tiers=(4, 2, 1))[0]

    m_acc = lax.cond(any_tie, lambda: first_pass(True), lambda: first_pass(False))
    for g in range(N_KV_HEADS):
        m = jnp.max(m_acc, axis=0, keepdims=True)
        oacc_ref[...] = jnp.zeros(oacc_ref.shape, jnp.float32)

        def fused(j, a, g=g, m=m):
            if g + 1 < N_KV_HEADS:
                a = jnp.maximum(a, logits_block(g + 1, j))
            values_block(g, j, m)
            return a

        m_acc = for_blocks(fused, m_init, tiers=(4, 2, 1))
        oacc = oacc_ref[...]
        on = oacc[0:HEAD_DIM] * (1.0 / oacc[HEAD_DIM:HEAD_DIM + 1])
        for u in range(KV_GROUP // 2):
            pair = jnp.concatenate([on[:, (2 * u) * tq:(2 * u + 1) * tq], on[:, (2 * u + 1) * tq:(2 * u + 2) * tq]],
                                   axis=0)
            c0 = (g * KV_GROUP + 2 * u) * HEAD_DIM
            o_ref[0, :, c0:c0 + 2 * HEAD_DIM] = pair.T.astype(o_ref.dtype)


def _attn_layer(qt, qit, wit, kp, vt, kib, *, tq, tq_real, n_keys_real, pos0, n_sel):
    B, _, T = qt.shape
    n_keys = kp.shape[1]
    assert T % tq == 0 and tq % LANES == 0 and n_keys % KEY_BLOCK == 0 and n_keys_real <= n_keys, (qt.shape, kp.shape)
    assert n_sel <= n_keys_real, (n_sel, n_keys_real)
    qblk = lambda r: pl.BlockSpec((1, r, tq), lambda b, t: (b, 0, t))
    kblk = lambda w: pl.BlockSpec((1, n_keys, w), lambda b, t: (b, 0, 0))
    return pl.pallas_call(
        functools.partial(_attn_kernel, tq=tq, tq_real=tq_real, n_keys_real=n_keys_real, pos0=pos0, n_sel=n_sel),
        grid=(B, T // tq),
        in_specs=[qblk(Q_W), qblk(QI_W), qblk(N_IDX_HEADS), kblk(KV_PAD_W),
                  pl.BlockSpec((1,) + vt.shape[1:], lambda b, t: (b, 0, 0, 0)), kblk(LANES)],
        out_specs=pl.BlockSpec((1, tq, Q_W), lambda b, t: (b, t, 0)),
        out_shape=jax.ShapeDtypeStruct((B, T, Q_W), jnp.bfloat16),
        scratch_shapes=[pltpu.VMEM((n_keys, tq), jnp.int32)] + [pltpu.VMEM((n_keys, tq), jnp.int16)] * 3 + [
                        pltpu.VMEM((n_keys, tq), jnp.float32),
                        pltpu.VMEM((n_keys, KV_GROUP * tq), jnp.float32),
                        pltpu.VMEM((n_keys, KV_GROUP * tq), jnp.float32),
                        pltpu.VMEM((V_SLOT, KV_GROUP * tq), jnp.float32)],
        compiler_params=pltpu.CompilerParams(
            dimension_semantics=("arbitrary", "arbitrary"), vmem_limit_bytes=VMEM_LIMIT),
        name="dsa_attention",
    )(qt, qit, wit, kp, vt, kib)


def _pad_heads(a):
    lead = a.shape[:-1]
    a = a.reshape(*lead, N_KV_HEADS, HEAD_DIM)
    a = jnp.pad(a, [(0, 0)] * len(lead) + [(0, 0), (0, LANES - HEAD_DIM)])
    return a.reshape(*lead, KV_PAD_W)


def _pack_w_in(w_in):
    o = 0
    wq = w_in[:, o:o + Q_W] * (HEAD_DIM ** -0.5); o += Q_W
    wk = w_in[:, o:o + KV_W]; o += KV_W
    wv = w_in[:, o:o + KV_W]; o += KV_W
    wqi = w_in[:, o:o + QI_W] * (IDX_DIM ** -0.5); o += QI_W
    wki = w_in[:, o:o + IDX_DIM]; o += IDX_DIM
    wwi = w_in[:, o:o + N_IDX_HEADS]
    wkw = jnp.pad(jnp.concatenate([wki, wwi], axis=1), ((0, 0), (0, LANES - IDX_DIM - N_IDX_HEADS)))
    rowmajor = jnp.concatenate([wk, wv, wkw], axis=1)
    wwi_rows = jnp.pad(wwi, ((0, 0), (0, PACKED_ROWS - N_IDX_HEADS)))
    transposed = jnp.concatenate([wq, wqi, wv, wwi_rows], axis=1).T
    assert rowmajor.shape[1] == PROJ_ROWMAJOR_W and transposed.shape[0] == PROJ_TRANSPOSED_ROWS
    return rowmajor.astype(jnp.bfloat16), transposed.astype(jnp.bfloat16)


def _rope_tables(pos):
    inv_freq = 1.0 / (ROPE_THETA ** (jnp.arange(0, HEAD_DIM, 2, dtype=jnp.float32) / HEAD_DIM))
    ang = pos.astype(jnp.float32)[:, None] * inv_freq[None, :]
    c, s = jnp.cos(ang), jnp.sin(ang)
    z = jnp.zeros_like(c)
    n = pos.shape[0]
    zpad = jnp.zeros((n, LANES - HEAD_DIM), jnp.float32)
    wi_scale = jnp.full((n, N_IDX_HEADS), N_IDX_HEADS ** -0.5, jnp.float32)
    c_kw = jnp.concatenate([c, c, wi_scale, jnp.zeros((n, LANES - HEAD_DIM - N_IDX_HEADS), jnp.float32)], axis=1)
    tab = jnp.concatenate([
        jnp.concatenate([c, c, c, c], axis=1), c_kw,
        jnp.concatenate([-s, z, -s, z], axis=1), jnp.concatenate([-s, z, zpad], axis=1),
        jnp.concatenate([z, s, z, s], axis=1), jnp.concatenate([z, s, zpad], axis=1),
    ], axis=1)
    tabt = jnp.concatenate([c, s], axis=1).T
    return tab, tabt


def _round_up(n, m):
    return (n + m - 1) // m * m


def _value_blocks_t(v):
    B, n, _ = v.shape
    nb = n // KEY_BLOCK
    vt = jnp.transpose(v.reshape(B, nb, KEY_BLOCK, N_KV_HEADS, HEAD_DIM), (0, 1, 3, 4, 2))
    ones = jnp.ones((B, nb, N_KV_HEADS, 1, KEY_BLOCK), v.dtype)
    zeros = jnp.zeros((B, nb, N_KV_HEADS, V_SLOT - HEAD_DIM - 1, KEY_BLOCK), v.dtype)
    return jnp.concatenate([vt, ones, zeros], axis=3).reshape(B, nb, V_ROWS, KEY_BLOCK)


def _trunk(x, pos0, pool_past, attn_past, norm_mix, norm_mlp, norm_final, pool_w, pool_scale,
           w_rowmajor, w_transposed, w_o, w_up, w_down):
    B, T, D = x.shape
    bf = jnp.bfloat16
    row = lambda v: v.reshape(1, D)

    if pool_past is None:
        past16 = jnp.zeros((B, POOL_STATE + 1, D), jnp.float32)
    else:
        past16 = jnp.pad(pool_past, ((0, 0), (1, 0), (0, 0)))
    tt = min(T, ROW_TILE)
    tm = min(B * T, ROW_TILE)
    x1, st = _pool_layer(x, past16, row(norm_mix[0]), pool_w[0].astype(bf), row(pool_scale[0]), pos0, tt)
    x2 = _mlp_layer(x1.reshape(B * T, D), row(norm_mlp[0]), w_up[0], w_down[0], tm, FF_CHUNK).reshape(B, T, D)
    pool_new = st[:, 1:][None]

    tab, tabt = _rope_tables(pos0 + jnp.arange(T, dtype=jnp.int32))
    kc, vc, kp, kw, kib, qt, qit, wit, vt = _proj_layer(
        x2, row(norm_mix[1]), w_rowmajor, w_transposed, tab, tabt, min(T, PROJ_TILE))
    k_new = kc.reshape(1, B, T, N_KV_HEADS, HEAD_DIM)
    v_new = vc.reshape(1, B, T, N_KV_HEADS, HEAD_DIM)
    ki_new = kw[:, :, :IDX_DIM][None]

    if attn_past is None:
        n_real = T
        kp_all, vt_all, kib_all = kp, vt, kib
    else:
        ck, cv, cki = attn_past
        P = ck.shape[1]
        n_real = P + T
        n_keys = _round_up(n_real, KEY_BLOCK)
        fill = lambda w: jnp.zeros((B, n_keys - n_real, w), bf)
        kp_all = jnp.concatenate([_pad_heads(ck.reshape(B, P, KV_W).astype(bf)), kp, fill(KV_PAD_W)], axis=1)
        kib_all = jnp.concatenate(
            [jnp.pad(cki.astype(bf), ((0, 0), (0, 0), (0, LANES - IDX_DIM))), kib, fill(LANES)], axis=1)
        vt_all = _value_blocks_t(
            jnp.concatenate([cv.reshape(B, P, KV_W).astype(bf), vc.astype(bf), fill(KV_W)], axis=1))
    tq = min(Q_TILE, _round_up(T, LANES))
    t_pad = _round_up(T, tq)
    padq = lambda a: jnp.pad(a, ((0, 0), (0, 0), (0, t_pad - T)))
    n_sel = min(TOPK_MAX, n_real // 4)
    o = _attn_layer(padq(qt), padq(qit), padq(wit), kp_all, vt_all, kib_all,
                    tq=tq, tq_real=min(T, tq), n_keys_real=n_real, pos0=pos0, n_sel=n_sel)
    o = o[:, :T].reshape(B * T, Q_W)

    y = _mlp_layer(x2.reshape(B * T, D), row(norm_mlp[1]), w_up[1], w_down[1], tm, FF_CHUNK,
                   attn=(o, w_o), final_g=row(norm_final)).reshape(B, T, D)
    return y, pool_new, k_new, v_new, ki_new


def kernel(x_prompt, x_sample, state_pool, cache_k, cache_v, cache_kidx, norm_mix, norm_mlp, norm_final,
           pool_w, pool_scale, attn_w_in, attn_w_o, mlp_w_up, mlp_w_down):
    bf = jnp.bfloat16
    shared = (norm_mix, norm_mlp, norm_final, pool_w, pool_scale, *_pack_w_in(attn_w_in[0]),
              attn_w_o[0].astype(bf), mlp_w_up.astype(bf), mlp_w_down.astype(bf))
    y_p, pool_p, k_p, v_p, ki_p = _trunk(x_prompt, 0, None, None, *shared)
    y_s, pool_s, k_s, v_s, ki_s = _trunk(
        x_sample, cache_k.shape[2], state_pool[0], (cache_k[0], cache_v[0], cache_kidx[0]), *shared)
    return (y_p, y_s, pool_p, pool_s, k_p, v_p, ki_p, k_s, v_s, ki_s)
```

```python
import functools
import math
import struct

import jax
import jax.numpy as jnp
from jax import lax
from jax.experimental import pallas as pl
from jax.experimental.pallas import tpu as pltpu

D_MODEL = 1024
CHUNK = 64
POOL_WINDOWS = (2, 4, 8, 16)
POOL_GROUP = D_MODEL // len(POOL_WINDOWS)
POOL_STATE = max(POOL_WINDOWS) - 1
N_HEADS = 16
HEAD_DIM = 64
N_KV_HEADS = 4
KV_GROUP = N_HEADS // N_KV_HEADS
N_IDX_HEADS = 8
IDX_DIM = 64
TOPK_MAX = 256
ROPE_THETA = 10000.0
D_FF = 4 * D_MODEL
EPS = 1e-6
NEG = -1e30
Q_W = N_HEADS * HEAD_DIM
KV_W = N_KV_HEADS * HEAD_DIM
QI_W = N_IDX_HEADS * IDX_DIM
LOG2E = math.log2(math.e)

SUBLANES = 8
PACKED_ROWS = 2 * SUBLANES
LANES = 128
KV_PAD_W = N_KV_HEADS * LANES
V_SLOT = HEAD_DIM + PACKED_ROWS
V_ROWS = N_KV_HEADS * V_SLOT
KEY_BLOCK = 256
Q_TILE = 256
ROW_TILE = 1024
PROJ_TILE = 1024
FF_CHUNK = 512

OFF_KC = 0
OFF_VC = OFF_KC + KV_W
OFF_KW = OFF_VC + KV_W
PROJ_ROWMAJOR_W = OFF_KW + LANES
ROW_Q = 0
ROW_QI = ROW_Q + Q_W
ROW_V = ROW_QI + QI_W
ROW_WI = ROW_V + KV_W
PROJ_TRANSPOSED_ROWS = ROW_WI + PACKED_ROWS

VMEM_LIMIT = 56 * 1024 * 1024

_NT = (((1,), (1,)), ((), ()))


def _f32_order_key(x):
    b = struct.unpack("<i", struct.pack("<f", x))[0]
    return b ^ ((b >> 31) & 0x7FFFFFFF)


def _signed16(v):
    return v - (1 << 16) if v >= (1 << 15) else v


NEG_KEY = _f32_order_key(NEG)
NEG_KEY_HI = NEG_KEY >> 16
NEG_KEY_LO = _signed16((NEG_KEY & 0xFFFF) ^ 0x8000)


def _rms(x, g):
    return x * lax.rsqrt(jnp.mean(x * x, axis=-1, keepdims=True) + EPS) * g


def _const_spec(shape):
    n = len(shape)
    return pl.BlockSpec(shape, lambda *_: (0,) * n, pipeline_mode=pl.Buffered(1))


def _tree(op, parts):
    parts = list(parts)
    while len(parts) > 1:
        parts = [op(parts[i], parts[i + 1]) if i + 1 < len(parts) else parts[i] for i in range(0, len(parts), 2)]
    return parts[0]


def _fold_rows(op, x, rows=SUBLANES):
    return _tree(op, [x[r:r + rows] for r in range(0, x.shape[0], rows)])


def _pool_kernel(x_ref, past_ref, g_ref, w_ref, scale_ref, x1_ref, st_ref, hp_ref, sa_ref, sb_ref, *, tt, pos0):
    t = pl.program_id(1)
    halo = POOL_STATE + 1
    lead = SUBLANES
    top = lead + halo

    @pl.when(t == 0)
    def _():
        zeros = jnp.zeros((lead, hp_ref.shape[1]), jnp.float32)
        hp_ref[0:lead, :] = zeros
        sa_ref[0:lead, :] = zeros
        sb_ref[0:lead, :] = zeros
        hp_ref[lead:top, :] = past_ref[0]

    x = x_ref[0]
    h = _rms(x, g_ref[...])
    hp_ref[top:top + tt, :] = h

    n = halo + tt
    g1, g2, g3 = POOL_GROUP, 2 * POOL_GROUP, 3 * POOL_GROUP
    sa_ref[lead:lead + n, :] = hp_ref[lead:lead + n, :] + hp_ref[lead - 1:lead - 1 + n, :]
    sb_ref[lead:lead + n, g1:] = sa_ref[lead:lead + n, g1:] + sa_ref[lead - 2:lead - 2 + n, g1:]
    sa_ref[lead:lead + n, g2:] = sb_ref[lead:lead + n, g2:] + sb_ref[lead - 4:lead - 4 + n, g2:]
    window_sums = (
        sa_ref[top:top + tt, 0:g1],
        sb_ref[top:top + tt, g1:g2],
        sa_ref[top:top + tt, g2:g3],
        sa_ref[top:top + tt, g3:] + sa_ref[top - 8:top - 8 + tt, g3:],
    )
    pos = pos0 + t * tt + lax.broadcasted_iota(jnp.int32, (tt, 1), 0)
    for g, win in enumerate(POOL_WINDOWS):
        c0, c1 = g * POOL_GROUP, (g + 1) * POOL_GROUP
        cnt = jnp.minimum(pos + 1, win).astype(jnp.float32)
        diff = window_sums[g] / cnt - h[:, c0:c1]
        y = jnp.dot(diff.astype(jnp.bfloat16), w_ref[g], preferred_element_type=jnp.float32)
        x1_ref[0, :, c0:c1] = x[:, c0:c1] + y * scale_ref[:, c0:c1]
    tail = hp_ref[lead + tt:top + tt, :]
    st_ref[0] = tail
    hp_ref[lead:top, :] = tail


def _pool_layer(x, past16, g, w, scale, pos0, tt):
    B, T, D = x.shape
    assert T % tt == 0 and D == D_MODEL, (x.shape, tt)
    halo = POOL_STATE + 1
    plane = pltpu.VMEM((SUBLANES + halo + tt, D), jnp.float32)
    return pl.pallas_call(
        functools.partial(_pool_kernel, tt=tt, pos0=pos0),
        grid=(B, T // tt),
        in_specs=[
            pl.BlockSpec((1, tt, D), lambda b, t: (b, t, 0)),
            pl.BlockSpec((1, halo, D), lambda b, t: (b, 0, 0)),
            _const_spec((1, D)),
            _const_spec(w.shape),
            _const_spec((1, D)),
        ],
        out_specs=[
            pl.BlockSpec((1, tt, D), lambda b, t: (b, t, 0)),
            pl.BlockSpec((1, halo, D), lambda b, t: (b, 0, 0)),
        ],
        out_shape=[
            jax.ShapeDtypeStruct((B, T, D), jnp.float32),
            jax.ShapeDtypeStruct((B, halo, D), jnp.float32),
        ],
        scratch_shapes=[plane, plane, plane],
        compiler_params=pltpu.CompilerParams(
            dimension_semantics=("arbitrary", "arbitrary"), vmem_limit_bytes=VMEM_LIMIT),
        name="pool_mixer",
    )(x, past16, g, w, scale)


def _mlp_body(x, g_ref, up_ref, down_ref, fc):
    h = _rms(x, g_ref[...]).astype(jnp.bfloat16)
    acc = x
    for c in range(D_FF // fc):
        u = jnp.dot(h, up_ref[:, c * fc:(c + 1) * fc], preferred_element_type=jnp.float32)
        u = jnp.maximum(u, 0.0)
        a = (u * u).astype(jnp.bfloat16)
        acc = acc + jnp.dot(a, down_ref[c * fc:(c + 1) * fc, :], preferred_element_type=jnp.float32)
    return acc


def _mlp_kernel(*refs, fc, with_attn, with_final):
    refs = list(refs)
    x_ref = refs.pop(0)
    if with_attn:
        o_ref = refs.pop(0)
        wo_ref = refs.pop(0)
    g_ref = refs.pop(0)
    up_ref = refs.pop(0)
    down_ref = refs.pop(0)
    if with_final:
        gf_ref = refs.pop(0)
    out_ref = refs.pop(0)

    x = x_ref[...]
    if with_attn:
        x = x + jnp.dot(o_ref[...], wo_ref[...], preferred_element_type=jnp.float32)
    acc = _mlp_body(x, g_ref, up_ref, down_ref, fc)
    if with_final:
        acc = _rms(acc, gf_ref[...])
    out_ref[...] = acc


def _mlp_layer(x, g, up, down, tm, fc, attn=None, final_g=None):
    N, D = x.shape
    assert N % tm == 0 and D_FF % fc == 0, (x.shape, tm, fc)
    row = lambda i: (i, 0)
    args, specs = [x], [pl.BlockSpec((tm, D), row)]
    if attn is not None:
        o, wo = attn
        args += [o, wo]
        specs += [pl.BlockSpec((tm, Q_W), row), _const_spec(wo.shape)]
    args += [g, up, down]
    specs += [_const_spec((1, D)), _const_spec(up.shape), _const_spec(down.shape)]
    if final_g is not None:
        args.append(final_g)
        specs.append(_const_spec((1, D)))
    return pl.pallas_call(
        functools.partial(_mlp_kernel, fc=fc, with_attn=attn is not None, with_final=final_g is not None),
        grid=(N // tm,),
        in_specs=specs,
        out_specs=pl.BlockSpec((tm, D), row),
        out_shape=jax.ShapeDtypeStruct((N, D), jnp.float32),
        compiler_params=pltpu.CompilerParams(
            dimension_semantics=("arbitrary",), vmem_limit_bytes=VMEM_LIMIT),
        name="mlp_attn_out" if attn is not None else "mlp",
    )(*args)


def _rope_block(xb, tab_ref, special):
    o = LANES if special else 0
    c = tab_ref[:, o:o + LANES]
    s1 = tab_ref[:, 2 * LANES + o:3 * LANES + o]
    s2 = tab_ref[:, 4 * LANES + o:5 * LANES + o]
    return xb * c + pltpu.roll(xb, LANES - HEAD_DIM // 2, 1) * s1 + pltpu.roll(xb, HEAD_DIM // 2, 1) * s2


def _rope_rows(p, c, s):
    half = HEAD_DIM // 2
    r = p.reshape(p.shape[0] // HEAD_DIM, 2, half, p.shape[1])
    x1, x2 = r[:, 0], r[:, 1]
    out = jnp.stack([x1 * c - x2 * s, x2 * c + x1 * s], axis=1)
    return out.reshape(p.shape)


def _proj_kernel(x_ref, g_ref, w_ref, wt_ref, tab_ref, tabt_ref,
                 kc_ref, vc_ref, kp_ref, kw_ref, kib_ref, qt_ref, qit_ref, wit_ref, vt_ref):
    h = _rms(x_ref[0], g_ref[...]).astype(jnp.bfloat16)

    def proj(off, width):
        return jnp.dot(h, w_ref[:, off:off + width], preferred_element_type=jnp.float32)

    def rope(p, special=False):
        return jnp.concatenate(
            [_rope_block(p[:, j:j + LANES], tab_ref, special) for j in range(0, p.shape[1], LANES)], axis=-1)

    kc = rope(proj(OFF_KC, KV_W))
    kc_ref[0] = kc
    vc_ref[0] = proj(OFF_VC, KV_W)
    gap = jnp.zeros((kc.shape[0], LANES - HEAD_DIM), jnp.float32)
    kp = jnp.concatenate(
        [piece for g in range(N_KV_HEADS) for piece in (kc[:, g * HEAD_DIM:(g + 1) * HEAD_DIM], gap)], axis=1)
    kp_ref[0] = kp.astype(kp_ref.dtype)
    kw = rope(proj(OFF_KW, LANES), special=True)
    kw_ref[0] = kw
    kib_ref[0] = kw.astype(kib_ref.dtype)

    def proj_t(row, n):
        return lax.dot_general(wt_ref[row:row + n, :], h, _NT, preferred_element_type=jnp.float32)

    half = HEAD_DIM // 2
    c, s = tabt_ref[0:half, :], tabt_ref[half:HEAD_DIM, :]
    qt_ref[0] = _rope_rows(proj_t(ROW_Q, Q_W), c * LOG2E, s * LOG2E).astype(qt_ref.dtype)
    qit_ref[0] = _rope_rows(proj_t(ROW_QI, QI_W), c, s).astype(qit_ref.dtype)
    vw = proj_t(ROW_V, KV_W + PACKED_ROWS)
    wit_ref[0] = vw[KV_W:KV_W + N_IDX_HEADS, :] * (N_IDX_HEADS ** -0.5)
    n = vw.shape[1]
    tail = jnp.where(lax.broadcasted_iota(jnp.int32, (V_SLOT - HEAD_DIM, n), 0) == 0, 1.0, 0.0)
    vt = jnp.concatenate(
        [piece for g in range(N_KV_HEADS) for piece in (vw[g * HEAD_DIM:(g + 1) * HEAD_DIM], tail)], axis=0)
    kbw = vt_ref.shape[3]
    for c in range(vt_ref.shape[1]):
        vt_ref[0, c] = vt[:, c * kbw:(c + 1) * kbw].astype(vt_ref.dtype)


def _proj_layer(x, g, w_rowmajor, w_transposed, tab, tabt, tm):
    B, T, D = x.shape
    blk = lambda w: pl.BlockSpec((1, tm, w), lambda b, t: (b, t, 0))
    blk_t = lambda r: pl.BlockSpec((1, r, tm), lambda b, t: (b, 0, t))
    bf, f32 = jnp.bfloat16, jnp.float32
    sds = jax.ShapeDtypeStruct
    kbw = min(tm, KEY_BLOCK)
    assert T % tm == 0 and tm % kbw == 0, (x.shape, tm)
    return pl.pallas_call(
        _proj_kernel,
        grid=(B, T // tm),
        in_specs=[
            blk(D),
            _const_spec((1, D)),
            _const_spec(w_rowmajor.shape),
            _const_spec(w_transposed.shape),
            pl.BlockSpec((tm, tab.shape[1]), lambda b, t: (t, 0)),
            pl.BlockSpec((HEAD_DIM, tm), lambda b, t: (0, t)),
        ],
        out_specs=[blk(KV_W), blk(KV_W), blk(KV_PAD_W), blk(LANES), blk(LANES),
                   blk_t(Q_W), blk_t(QI_W), blk_t(N_IDX_HEADS),
                   pl.BlockSpec((1, tm // kbw, V_ROWS, kbw), lambda b, t: (b, t, 0, 0))],
        out_shape=[sds((B, T, KV_W), f32), sds((B, T, KV_W), f32), sds((B, T, KV_PAD_W), bf),
                   sds((B, T, LANES), f32), sds((B, T, LANES), bf),
                   sds((B, Q_W, T), bf), sds((B, QI_W, T), bf), sds((B, N_IDX_HEADS, T), f32),
                   sds((B, T // kbw, V_ROWS, kbw), bf)],
        compiler_params=pltpu.CompilerParams(
            dimension_semantics=("arbitrary", "arbitrary"), vmem_limit_bytes=VMEM_LIMIT),
        name="attn_in_proj",
    )(x, g, w_rowmajor, w_transposed, tab, tabt)


def _attn_kernel(qt_ref, qit_ref, wit_ref, kp_ref, vt_ref, kib_ref, o_ref,
                 key_ref, hi_ref, lo_ref, lo2_ref, f_ref, s0_ref, s1_ref, oacc_ref, *,
                 tq, tq_real, n_keys_real, pos0, n_sel):
    kb = KEY_BLOCK
    t = pl.program_id(1)
    lane_q = lax.broadcasted_iota(jnp.int32, (1, tq), 1)
    q_limit = ((pos0 + t * tq + lane_q) // CHUNK + 1) * CHUNK
    qvalid = lane_q < tq_real
    n_reach = jnp.minimum(((pos0 + (t + 1) * tq - 1) // CHUNK + 1) * CHUNK, n_keys_real)
    nkb = (n_reach + kb - 1) // kb
    n_unvisited = jnp.maximum(n_keys_real - nkb * kb, 0)
    key_row = lax.broadcasted_iota(jnp.int32, (kb, tq), 0)

    def block_start(j):
        return pl.multiple_of(j * kb, kb)

    def for_blocks(body, carry, tiers=(2, 1)):
        start = 0
        for n in tiers:
            def trip(i, c, n=n, start=start):
                for u in range(n):
                    c = body(start + n * i + u, c)
                return c
            trips = (nkb - start) // n
            carry = lax.fori_loop(0, trips, trip, carry)
            start = start + trips * n
        return carry

    def admissible(k0):
        return key_row < q_limit - k0

    wi = wit_ref[0]

    def score_block(j, carry):
        k0 = block_start(j)
        kx = kib_ref[0, pl.ds(k0, kb), 0:IDX_DIM]
        acc = jnp.zeros((kb, tq), jnp.float32)
        for hp in range(N_IDX_HEADS // 2):
            qi2 = jnp.concatenate(
                [qit_ref[0, (2 * hp + u) * IDX_DIM:(2 * hp + u + 1) * IDX_DIM, :] for u in range(2)], axis=1)
            lg = jnp.dot(kx, qi2, preferred_element_type=jnp.float32)
            for u in range(2):
                hh = 2 * hp + u
                acc = acc + jnp.maximum(lg[:, u * tq:(u + 1) * tq], 0.0) * wi[hh:hh + 1, :]
        sc = jnp.where(admissible(k0), acc, NEG)
        if n_keys_real % kb:
            sc = jnp.where(key_row < n_keys_real - k0, sc, -jnp.inf)
        bits = lax.bitcast_convert_type(sc, jnp.int32)
        key = bits ^ ((bits >> 31) & jnp.int32(0x7FFFFFFF))
        key_ref[pl.ds(k0, kb), :] = key
        hi_ref[pl.ds(k0, kb), :] = (key >> 16).astype(jnp.int16)
        lo_ref[pl.ds(k0, kb), :] = (key ^ 0x8000).astype(jnp.int16)
        return carry

    for_blocks(score_block, 0, tiers=(4, 2, 1))

    def count16(ref, pred):
        def body(j, acc):
            hit = jnp.where(pred(ref[pl.ds(block_start(j), kb), :]), jnp.int16(1), jnp.int16(0))
            return acc + _fold_rows(jnp.add, hit, rows=PACKED_ROWS)
        acc = for_blocks(body, jnp.zeros((PACKED_ROWS, tq), jnp.int16), tiers=(4, 2, 1))
        return jnp.sum(acc.astype(jnp.int32), axis=0, keepdims=True)

    def as_plane(v):
        tile = jnp.broadcast_to(v, (PACKED_ROWS, tq)).astype(jnp.int16)
        return jnp.concatenate([tile] * (kb // PACKED_ROWS), axis=0)

    def radix16(ref, n_wanted, n_all, unvisited_ge):
        def step(i, carry):
            thr, n_ge_thr, n_gt_thr = carry
            cand = thr + jnp.left_shift(jnp.int32(1), 15 - i)
            plane = as_plane(cand)
            n_ge = count16(ref, lambda blk: blk >= plane) + unvisited_ge(cand)
            take = n_ge >= n_wanted
            return jnp.where(take, cand, thr), jnp.where(take, n_ge, n_ge_thr), jnp.where(take, n_gt_thr, n_ge)
        start = (jnp.full((1, tq), -2 ** 15, jnp.int32), n_all, jnp.zeros((1, tq), jnp.int32))
        return lax.fori_loop(0, 16, step, start)

    n_all = jnp.broadcast_to(nkb * kb + n_unvisited, (1, tq))
    thr_hi, n_ge_hi, n_gt_hi = radix16(
        hi_ref, n_sel, n_all, lambda cand: jnp.where(cand <= NEG_KEY_HI, n_unvisited, 0))
    hi_plane = as_plane(thr_hi)

    def park_block(j, carry):
        rows = pl.ds(block_start(j), kb)
        lo2_ref[rows, :] = jnp.where(hi_ref[rows, :] == hi_plane, lo_ref[rows, :], jnp.int16(-2 ** 15))
        return carry

    for_blocks(park_block, 0)
    neg_in_bucket = thr_hi == NEG_KEY_HI
    thr_lo, n_ge_lo, n_gt_lo = radix16(
        lo2_ref, n_sel - n_gt_hi, n_ge_hi - n_gt_hi,
        lambda cand: jnp.where(neg_in_bucket & (cand <= NEG_KEY_LO), n_unvisited, 0))
    thr = thr_hi * 65536 + (thr_lo + 2 ** 15)
    n_gt = n_gt_hi + n_gt_lo
    n_eq = n_ge_lo - n_gt_lo
    need = n_sel - n_gt
    tie = (n_eq > need) & qvalid
    any_tie = jnp.max(tie.astype(jnp.int32)) > 0

    def mask_block(j, run, exact_ties):
        k0 = block_start(j)
        kblk = key_ref[pl.ds(k0, kb), :]
        if exact_ties:
            r = lax.broadcasted_iota(jnp.int32, (kb, kb), 0)
            c = lax.broadcasted_iota(jnp.int32, (kb, kb), 1)
            tri = jnp.where(r >= c, 1.0, 0.0).astype(jnp.bfloat16)
            eqb = kblk == thr
            eqf = jnp.where(eqb, 1.0, 0.0)
            rank = jnp.dot(tri, eqf.astype(jnp.bfloat16), preferred_element_type=jnp.float32) + run
            run = run + jnp.sum(eqf, axis=0, keepdims=True)
            sel = (kblk > thr) | (eqb & (rank <= need.astype(jnp.float32)))
        else:
            sel = kblk >= thr
        f = jnp.where(sel, jnp.where(admissible(k0), jnp.inf, NEG), -jnp.inf)
        f_ref[pl.ds(k0, kb), :] = f
        return f, run

    gw = KV_GROUP * tq
    s_refs = (s0_ref, s1_ref)

    def logits_block(g, j, f=None):
        k0 = block_start(j)
        q4 = jnp.concatenate(
            [qt_ref[0, (g * KV_GROUP + r) * HEAD_DIM:(g * KV_GROUP + r + 1) * HEAD_DIM, :] for r in range(KV_GROUP)],
            axis=1)
        kblk = kp_ref[0, pl.ds(k0, kb), g * LANES:g * LANES + HEAD_DIM]
        s = jnp.dot(kblk, q4, preferred_element_type=jnp.float32)
        if f is None:
            f = f_ref[pl.ds(k0, kb), :]
        s = jnp.minimum(s, jnp.concatenate([f] * KV_GROUP, axis=1))
        s_refs[g % 2][pl.ds(k0, kb), :] = s
        return _fold_rows(jnp.maximum, s)

    def values_block(g, j, m):
        p = jnp.exp2(s_refs[g % 2][pl.ds(block_start(j), kb), :] - m).astype(jnp.bfloat16)
        oacc_ref[...] += jnp.dot(vt_ref[0, j, g * V_SLOT:(g + 1) * V_SLOT, :], p,
                                 preferred_element_type=jnp.float32)

    m_init = jnp.full((SUBLANES, gw), -jnp.inf, jnp.float32)

    def first_pass(exact_ties):
        def body(j, carry):
            a, run = carry
            f, run = mask_block(j, run, exact_ties)
            return jnp.maximum(a, logits_block(0, j, f)), run
        return for_blocks(body, (m_init, jnp.zeros((1, tq), jnp.float32)), tiers=(4, 2, 1))[0]

    m_acc = lax.cond(any_tie, lambda: first_pass(True), lambda: first_pass(False))
    for g in range(N_KV_HEADS):
        m = jnp.max(m_acc, axis=0, keepdims=True)
        oacc_ref[...] = jnp.zeros(oacc_ref.shape, jnp.float32)

        def fused(j, a, g=g, m=m):
            if g + 1 < N_KV_HEADS:
                a = jnp.maximum(a, logits_block(g + 1, j))
            values_block(g, j, m)
            return a

        m_acc = for_blocks(fused, m_init, tiers=(4, 2, 1))
        oacc = oacc_ref[...]
        on = oacc[0:HEAD_DIM] * (1.0 / oacc[HEAD_DIM:HEAD_DIM + 1])
        for u in range(KV_GROUP // 2):
            pair = jnp.concatenate([on[:, (2 * u) * tq:(2 * u + 1) * tq], on[:, (2 * u + 1) * tq:(2 * u + 2) * tq]],
                                   axis=0)
            c0 = (g * KV_GROUP + 2 * u) * HEAD_DIM
            o_ref[0, :, c0:c0 + 2 * HEAD_DIM] = pair.T.astype(o_ref.dtype)


def _attn_layer(qt, qit, wit, kp, vt, kib, *, tq, tq_real, n_keys_real, pos0, n_sel):
    B, _, T = qt.shape
    n_keys = kp.shape[1]
    assert T % tq == 0 and tq % LANES == 0 and n_keys % KEY_BLOCK == 0 and n_keys_real <= n_keys, (qt.shape, kp.shape)
    assert n_sel <= n_keys_real, (n_sel, n_keys_real)
    qblk = lambda r: pl.BlockSpec((1, r, tq), lambda b, t: (b, 0, t))
    kblk = lambda w: pl.BlockSpec((1, n_keys, w), lambda b, t: (b, 0, 0))
    return pl.pallas_call(
        functools.partial(_attn_kernel, tq=tq, tq_real=tq_real, n_keys_real=n_keys_real, pos0=pos0, n_sel=n_sel),
        grid=(B, T // tq),
        in_specs=[qblk(Q_W), qblk(QI_W), qblk(N_IDX_HEADS), kblk(KV_PAD_W),
                  pl.BlockSpec((1,) + vt.shape[1:], lambda b, t: (b, 0, 0, 0)), kblk(LANES)],
        out_specs=pl.BlockSpec((1, tq, Q_W), lambda b, t: (b, t, 0)),
        out_shape=jax.ShapeDtypeStruct((B, T, Q_W), jnp.bfloat16),
        scratch_shapes=[pltpu.VMEM((n_keys, tq), jnp.int32)] + [pltpu.VMEM((n_keys, tq), jnp.int16)] * 3 + [
                        pltpu.VMEM((n_keys, tq), jnp.float32),
                        pltpu.VMEM((n_keys, KV_GROUP * tq), jnp.float32),
                        pltpu.VMEM((n_keys, KV_GROUP * tq), jnp.float32),
                        pltpu.VMEM((V_SLOT, KV_GROUP * tq), jnp.float32)],
        compiler_params=pltpu.CompilerParams(
            dimension_semantics=("arbitrary", "arbitrary"), vmem_limit_bytes=VMEM_LIMIT),
        name="dsa_attention",
    )(qt, qit, wit, kp, vt, kib)


def _pad_heads(a):
    lead = a.shape[:-1]
    a = a.reshape(*lead, N_KV_HEADS, HEAD_DIM)
    a = jnp.pad(a, [(0, 0)] * len(lead) + [(0, 0), (0, LANES - HEAD_DIM)])
    return a.reshape(*lead, KV_PAD_W)


def _pack_w_in(w_in):
    o = 0
    wq = w_in[:, o:o + Q_W] * (HEAD_DIM ** -0.5); o += Q_W
    wk = w_in[:, o:o + KV_W]; o += KV_W
    wv = w_in[:, o:o + KV_W]; o += KV_W
    wqi = w_in[:, o:o + QI_W] * (IDX_DIM ** -0.5); o += QI_W
    wki = w_in[:, o:o + IDX_DIM]; o += IDX_DIM
    wwi = w_in[:, o:o + N_IDX_HEADS]
    wkw = jnp.pad(jnp.concatenate([wki, wwi], axis=1), ((0, 0), (0, LANES - IDX_DIM - N_IDX_HEADS)))
    rowmajor = jnp.concatenate([wk, wv, wkw], axis=1)
    wwi_rows = jnp.pad(wwi, ((0, 0), (0, PACKED_ROWS - N_IDX_HEADS)))
    transposed = jnp.concatenate([wq, wqi, wv, wwi_rows], axis=1).T
    assert rowmajor.shape[1] == PROJ_ROWMAJOR_W and transposed.shape[0] == PROJ_TRANSPOSED_ROWS
    return rowmajor.astype(jnp.bfloat16), transposed.astype(jnp.bfloat16)


def _rope_tables(pos):
    inv_freq = 1.0 / (ROPE_THETA ** (jnp.arange(0, HEAD_DIM, 2, dtype=jnp.float32) / HEAD_DIM))
    ang = pos.astype(jnp.float32)[:, None] * inv_freq[None, :]
    c, s = jnp.cos(ang), jnp.sin(ang)
    z = jnp.zeros_like(c)
    n = pos.shape[0]
    zpad = jnp.zeros((n, LANES - HEAD_DIM), jnp.float32)
    wi_scale = jnp.full((n, N_IDX_HEADS), N_IDX_HEADS ** -0.5, jnp.float32)
    c_kw = jnp.concatenate([c, c, wi_scale, jnp.zeros((n, LANES - HEAD_DIM - N_IDX_HEADS), jnp.float32)], axis=1)
    tab = jnp.concatenate([
        jnp.concatenate([c, c, c, c], axis=1), c_kw,
        jnp.concatenate([-s, z, -s, z], axis=1), jnp.concatenate([-s, z, zpad], axis=1),
        jnp.concatenate([z, s, z, s], axis=1), jnp.concatenate([z, s, zpad], axis=1),
    ], axis=1)
    tabt = jnp.concatenate([c, s], axis=1).T
    return tab, tabt


def _round_up(n, m):
    return (n + m - 1) // m * m


def _value_blocks_t(v):
    B, n, _ = v.shape
    nb = n // KEY_BLOCK
    vt = jnp.transpose(v.reshape(B, nb, KEY_BLOCK, N_KV_HEADS, HEAD_DIM), (0, 1, 3, 4, 2))
    ones = jnp.ones((B, nb, N_KV_HEADS, 1, KEY_BLOCK), v.dtype)
    zeros = jnp.zeros((B, nb, N_KV_HEADS, V_SLOT - HEAD_DIM - 1, KEY_BLOCK), v.dtype)
    return jnp.concatenate([vt, ones, zeros], axis=3).reshape(B, nb, V_ROWS, KEY_BLOCK)


def _trunk(x, pos0, pool_past, attn_past, norm_mix, norm_mlp, norm_final, pool_w, pool_scale,
           w_rowmajor, w_transposed, w_o, w_up, w_down):
    B, T, D = x.shape
    bf = jnp.bfloat16
    row = lambda v: v.reshape(1, D)

    if pool_past is None:
        past16 = jnp.zeros((B, POOL_STATE + 1, D), jnp.float32)
    else:
        past16 = jnp.pad(pool_past, ((0, 0), (1, 0), (0, 0)))
    tt = min(T, ROW_TILE)
    tm = min(B * T, ROW_TILE)
    x1, st = _pool_layer(x, past16, row(norm_mix[0]), pool_w[0].astype(bf), row(pool_scale[0]), pos0, tt)
    x2 = _mlp_layer(x1.reshape(B * T, D), row(norm_mlp[0]), w_up[0], w_down[0], tm, FF_CHUNK).reshape(B, T, D)
    pool_new = st[:, 1:][None]

    tab, tabt = _rope_tables(pos0 + jnp.arange(T, dtype=jnp.int32))
    kc, vc, kp, kw, kib, qt, qit, wit, vt = _proj_layer(
        x2, row(norm_mix[1]), w_rowmajor, w_transposed, tab, tabt, min(T, PROJ_TILE))
    k_new = kc.reshape(1, B, T, N_KV_HEADS, HEAD_DIM)
    v_new = vc.reshape(1, B, T, N_KV_HEADS, HEAD_DIM)
    ki_new = kw[:, :, :IDX_DIM][None]

    if attn_past is None:
        n_real = T
        kp_all, vt_all, kib_all = kp, vt, kib
    else:
        ck, cv, cki = attn_past
        P = ck.shape[1]
        n_real = P + T
        n_keys = _round_up(n_real, KEY_BLOCK)
        fill = lambda w: jnp.zeros((B, n_keys - n_real, w), bf)
        kp_all = jnp.concatenate([_pad_heads(ck.reshape(B, P, KV_W).astype(bf)), kp, fill(KV_PAD_W)], axis=1)
        kib_all = jnp.concatenate(
            [jnp.pad(cki.astype(bf), ((0, 0), (0, 0), (0, LANES - IDX_DIM))), kib, fill(LANES)], axis=1)
        vt_all = _value_blocks_t(
            jnp.concatenate([cv.reshape(B, P, KV_W).astype(bf), vc.astype(bf), fill(KV_W)], axis=1))
    tq = min(Q_TILE, _round_up(T, LANES))
    t_pad = _round_up(T, tq)
    padq = lambda a: jnp.pad(a, ((0, 0), (0, 0), (0, t_pad - T)))
    n_sel = min(TOPK_MAX, n_real // 4)
    o = _attn_layer(padq(qt), padq(qit), padq(wit), kp_all, vt_all, kib_all,
                    tq=tq, tq_real=min(T, tq), n_keys_real=n_real, pos0=pos0, n_sel=n_sel)
    o = o[:, :T].reshape(B * T, Q_W)

    y = _mlp_layer(x2.reshape(B * T, D), row(norm_mlp[1]), w_up[1], w_down[1], tm, FF_CHUNK,
                   attn=(o, w_o), final_g=row(norm_final)).reshape(B, T, D)
    return y, pool_new, k_new, v_new, ki_new


def kernel(x_prompt, x_sample, state_pool, cache_k, cache_v, cache_kidx, norm_mix, norm_mlp, norm_final,
           pool_w, pool_scale, attn_w_in, attn_w_o, mlp_w_up, mlp_w_down):
    bf = jnp.bfloat16
    shared = (norm_mix, norm_mlp, norm_final, pool_w, pool_scale, *_pack_w_in(attn_w_in[0]),
              attn_w_o[0].astype(bf), mlp_w_up.astype(bf), mlp_w_down.astype(bf))
    y_p, pool_p, k_p, v_p, ki_p = _trunk(x_prompt, 0, None, None, *shared)
    y_s, pool_s, k_s, v_s, ki_s = _trunk(
        x_sample, cache_k.shape[2], state_pool[0], (cache_k[0], cache_v[0], cache_kidx[0]), *shared)
    return (y_p, y_s, pool_p, pool_s, k_p, v_p, ki_p, k_s, v_s, ki_s)
```

```python
import functools
import math
import struct

import jax
import jax.numpy as jnp
from jax import lax
from jax.experimental import pallas as pl
from jax.experimental.pallas import tpu as pltpu

D_MODEL = 1024
CHUNK = 64
POOL_WINDOWS = (2, 4, 8, 16)
POOL_GROUP = D_MODEL // len(POOL_WINDOWS)
POOL_STATE = max(POOL_WINDOWS) - 1
N_HEADS = 16
HEAD_DIM = 64
N_KV_HEADS = 4
KV_GROUP = N_HEADS // N_KV_HEADS
N_IDX_HEADS = 8
IDX_DIM = 64
TOPK_MAX = 256
ROPE_THETA = 10000.0
D_FF = 4 * D_MODEL
EPS = 1e-6
NEG = -1e30
Q_W = N_HEADS * HEAD_DIM
KV_W = N_KV_HEADS * HEAD_DIM
QI_W = N_IDX_HEADS * IDX_DIM
LOG2E = math.log2(math.e)

SUBLANES = 8
PACKED_ROWS = 2 * SUBLANES
LANES = 128
KV_PAD_W = N_KV_HEADS * LANES
V_SLOT = HEAD_DIM + PACKED_ROWS
V_ROWS = N_KV_HEADS * V_SLOT
KEY_BLOCK = 256
Q_TILE = 256
BLOCK_TIERS = (4, 2, 1)
ROW_TILE = 1024
PROJ_TILE = 1024
FF_CHUNK = 512

OFF_KC = 0
OFF_VC = OFF_KC + KV_W
OFF_KW = OFF_VC + KV_W
PROJ_ROWMAJOR_W = OFF_KW + LANES
ROW_Q = 0
ROW_QI = ROW_Q + Q_W
ROW_V = ROW_QI + QI_W
ROW_WI = ROW_V + KV_W
PROJ_TRANSPOSED_ROWS = ROW_WI + PACKED_ROWS

VMEM_LIMIT = 56 * 1024 * 1024

_NT = (((1,), (1,)), ((), ()))


def _f32_order_key(x):
    b = struct.unpack("<i", struct.pack("<f", x))[0]
    return b ^ ((b >> 31) & 0x7FFFFFFF)


def _signed16(v):
    return v - (1 << 16) if v >= (1 << 15) else v


NEG_KEY = _f32_order_key(NEG)
NEG_KEY_HI = NEG_KEY >> 16
NEG_KEY_LO = _signed16((NEG_KEY & 0xFFFF) ^ 0x8000)


def _rms(x, g):
    return x * lax.rsqrt(jnp.mean(x * x, axis=-1, keepdims=True) + EPS) * g


def _const_spec(shape):
    n = len(shape)
    return pl.BlockSpec(shape, lambda *_: (0,) * n, pipeline_mode=pl.Buffered(1))


def _tree(op, parts):
    parts = list(parts)
    while len(parts) > 1:
        parts = [op(parts[i], parts[i + 1]) if i + 1 < len(parts) else parts[i] for i in range(0, len(parts), 2)]
    return parts[0]


def _fold_rows(op, x, rows=SUBLANES):
    return _tree(op, [x[r:r + rows] for r in range(0, x.shape[0], rows)])


def _pool_kernel(x_ref, past_ref, g_ref, w_ref, scale_ref, x1_ref, st_ref, hp_ref, sa_ref, sb_ref, *, tt, pos0):
    t = pl.program_id(1)
    halo = POOL_STATE + 1
    lead = SUBLANES
    top = lead + halo

    @pl.when(t == 0)
    def _():
        zeros = jnp.zeros((lead, hp_ref.shape[1]), jnp.float32)
        hp_ref[0:lead, :] = zeros
        sa_ref[0:lead, :] = zeros
        sb_ref[0:lead, :] = zeros
        hp_ref[lead:top, :] = past_ref[0]

    x = x_ref[0]
    h = _rms(x, g_ref[...])
    hp_ref[top:top + tt, :] = h

    n = halo + tt
    g1, g2, g3 = POOL_GROUP, 2 * POOL_GROUP, 3 * POOL_GROUP
    sa_ref[lead:lead + n, :] = hp_ref[lead:lead + n, :] + hp_ref[lead - 1:lead - 1 + n, :]
    sb_ref[lead:lead + n, g1:] = sa_ref[lead:lead + n, g1:] + sa_ref[lead - 2:lead - 2 + n, g1:]
    sa_ref[lead:lead + n, g2:] = sb_ref[lead:lead + n, g2:] + sb_ref[lead - 4:lead - 4 + n, g2:]
    window_sums = (
        sa_ref[top:top + tt, 0:g1],
        sb_ref[top:top + tt, g1:g2],
        sa_ref[top:top + tt, g2:g3],
        sa_ref[top:top + tt, g3:] + sa_ref[top - 8:top - 8 + tt, g3:],
    )
    pos = pos0 + t * tt + lax.broadcasted_iota(jnp.int32, (tt, 1), 0)
    for g, win in enumerate(POOL_WINDOWS):
        c0, c1 = g * POOL_GROUP, (g + 1) * POOL_GROUP
        cnt = jnp.minimum(pos + 1, win).astype(jnp.float32)
        diff = window_sums[g] / cnt - h[:, c0:c1]
        y = jnp.dot(diff.astype(jnp.bfloat16), w_ref[g], preferred_element_type=jnp.float32)
        x1_ref[0, :, c0:c1] = x[:, c0:c1] + y * scale_ref[:, c0:c1]
    tail = hp_ref[lead + tt:top + tt, :]
    st_ref[0] = tail
    hp_ref[lead:top, :] = tail


def _pool_layer(x, past16, g, w, scale, pos0, tt):
    B, T, D = x.shape
    assert T % tt == 0 and D == D_MODEL, (x.shape, tt)
    halo = POOL_STATE + 1
    plane = pltpu.VMEM((SUBLANES + halo + tt, D), jnp.float32)
    return pl.pallas_call(
        functools.partial(_pool_kernel, tt=tt, pos0=pos0),
        grid=(B, T // tt),
        in_specs=[
            pl.BlockSpec((1, tt, D), lambda b, t: (b, t, 0)),
            pl.BlockSpec((1, halo, D), lambda b, t: (b, 0, 0)),
            _const_spec((1, D)),
            _const_spec(w.shape),
            _const_spec((1, D)),
        ],
        out_specs=[
            pl.BlockSpec((1, tt, D), lambda b, t: (b, t, 0)),
            pl.BlockSpec((1, halo, D), lambda b, t: (b, 0, 0)),
        ],
        out_shape=[
            jax.ShapeDtypeStruct((B, T, D), jnp.float32),
            jax.ShapeDtypeStruct((B, halo, D), jnp.float32),
        ],
        scratch_shapes=[plane, plane, plane],
        compiler_params=pltpu.CompilerParams(
            dimension_semantics=("arbitrary", "arbitrary"), vmem_limit_bytes=VMEM_LIMIT),
        name="pool_mixer",
    )(x, past16, g, w, scale)


def _mlp_body(x, g_ref, up_ref, down_ref, fc):
    h = _rms(x, g_ref[...]).astype(jnp.bfloat16)
    acc = x
    for c in range(D_FF // fc):
        u = jnp.dot(h, up_ref[:, c * fc:(c + 1) * fc], preferred_element_type=jnp.float32)
        u = jnp.maximum(u, 0.0)
        a = (u * u).astype(jnp.bfloat16)
        acc = acc + jnp.dot(a, down_ref[c * fc:(c + 1) * fc, :], preferred_element_type=jnp.float32)
    return acc


def _mlp_kernel(*refs, fc, with_attn, with_final):
    refs = list(refs)
    x_ref = refs.pop(0)
    if with_attn:
        o_ref = refs.pop(0)
        wo_ref = refs.pop(0)
    g_ref = refs.pop(0)
    up_ref = refs.pop(0)
    down_ref = refs.pop(0)
    if with_final:
        gf_ref = refs.pop(0)
    out_ref = refs.pop(0)

    x = x_ref[...]
    if with_attn:
        x = x + jnp.dot(o_ref[...], wo_ref[...], preferred_element_type=jnp.float32)
    acc = _mlp_body(x, g_ref, up_ref, down_ref, fc)
    if with_final:
        acc = _rms(acc, gf_ref[...])
    out_ref[...] = acc


def _mlp_layer(x, g, up, down, tm, fc, attn=None, final_g=None):
    N, D = x.shape
    assert N % tm == 0 and D_FF % fc == 0, (x.shape, tm, fc)
    row = lambda i: (i, 0)
    args, specs = [x], [pl.BlockSpec((tm, D), row)]
    if attn is not None:
        o, wo = attn
        args += [o, wo]
        specs += [pl.BlockSpec((tm, Q_W), row), _const_spec(wo.shape)]
    args += [g, up, down]
    specs += [_const_spec((1, D)), _const_spec(up.shape), _const_spec(down.shape)]
    if final_g is not None:
        args.append(final_g)
        specs.append(_const_spec((1, D)))
    return pl.pallas_call(
        functools.partial(_mlp_kernel, fc=fc, with_attn=attn is not None, with_final=final_g is not None),
        grid=(N // tm,),
        in_specs=specs,
        out_specs=pl.BlockSpec((tm, D), row),
        out_shape=jax.ShapeDtypeStruct((N, D), jnp.float32),
        compiler_params=pltpu.CompilerParams(
            dimension_semantics=("arbitrary",), vmem_limit_bytes=VMEM_LIMIT),
        name="mlp_attn_out" if attn is not None else "mlp",
    )(*args)


def _rope_block(xb, tab_ref, special):
    o = LANES if special else 0
    c = tab_ref[:, o:o + LANES]
    s1 = tab_ref[:, 2 * LANES + o:3 * LANES + o]
    s2 = tab_ref[:, 4 * LANES + o:5 * LANES + o]
    return xb * c + pltpu.roll(xb, LANES - HEAD_DIM // 2, 1) * s1 + pltpu.roll(xb, HEAD_DIM // 2, 1) * s2


def _rope_rows(p, c, s):
    half = HEAD_DIM // 2
    r = p.reshape(p.shape[0] // HEAD_DIM, 2, half, p.shape[1])
    x1, x2 = r[:, 0], r[:, 1]
    out = jnp.stack([x1 * c - x2 * s, x2 * c + x1 * s], axis=1)
    return out.reshape(p.shape)


def _proj_kernel(x_ref, g_ref, w_ref, wt_ref, tab_ref, tabt_ref,
                 kc_ref, vc_ref, kp_ref, kw_ref, kib_ref, qt_ref, qit_ref, wit_ref, vt_ref):
    h = _rms(x_ref[0], g_ref[...]).astype(jnp.bfloat16)

    def proj(off, width):
        return jnp.dot(h, w_ref[:, off:off + width], preferred_element_type=jnp.float32)

    def rope(p, special=False):
        return jnp.concatenate(
            [_rope_block(p[:, j:j + LANES], tab_ref, special) for j in range(0, p.shape[1], LANES)], axis=-1)

    kc = rope(proj(OFF_KC, KV_W))
    kc_ref[0] = kc
    vc_ref[0] = proj(OFF_VC, KV_W)
    gap = jnp.zeros((kc.shape[0], LANES - HEAD_DIM), jnp.float32)
    kp = jnp.concatenate(
        [piece for g in range(N_KV_HEADS) for piece in (kc[:, g * HEAD_DIM:(g + 1) * HEAD_DIM], gap)], axis=1)
    kp_ref[0] = kp.astype(kp_ref.dtype)
    kw = rope(proj(OFF_KW, LANES), special=True)
    kw_ref[0] = kw
    kib_ref[0] = kw.astype(kib_ref.dtype)

    def proj_t(row, n):
        return lax.dot_general(wt_ref[row:row + n, :], h, _NT, preferred_element_type=jnp.float32)

    half = HEAD_DIM // 2
    c, s = tabt_ref[0:half, :], tabt_ref[half:HEAD_DIM, :]
    qt_ref[0] = _rope_rows(proj_t(ROW_Q, Q_W), c * LOG2E, s * LOG2E).astype(qt_ref.dtype)
    qit_ref[0] = _rope_rows(proj_t(ROW_QI, QI_W), c, s).astype(qit_ref.dtype)
    vw = proj_t(ROW_V, KV_W + PACKED_ROWS)
    wit_ref[0] = vw[KV_W:KV_W + N_IDX_HEADS, :] * (N_IDX_HEADS ** -0.5)
    n = vw.shape[1]
    tail = jnp.where(lax.broadcasted_iota(jnp.int32, (V_SLOT - HEAD_DIM, n), 0) == 0, 1.0, 0.0)
    vt = jnp.concatenate(
        [piece for g in range(N_KV_HEADS) for piece in (vw[g * HEAD_DIM:(g + 1) * HEAD_DIM], tail)], axis=0)
    kbw = vt_ref.shape[3]
    for c in range(vt_ref.shape[1]):
        vt_ref[0, c] = vt[:, c * kbw:(c + 1) * kbw].astype(vt_ref.dtype)


def _proj_layer(x, g, w_rowmajor, w_transposed, tab, tabt, tm):
    B, T, D = x.shape
    blk = lambda w: pl.BlockSpec((1, tm, w), lambda b, t: (b, t, 0))
    blk_t = lambda r: pl.BlockSpec((1, r, tm), lambda b, t: (b, 0, t))
    bf, f32 = jnp.bfloat16, jnp.float32
    sds = jax.ShapeDtypeStruct
    kbw = min(tm, KEY_BLOCK)
    assert T % tm == 0 and tm % kbw == 0, (x.shape, tm)
    return pl.pallas_call(
        _proj_kernel,
        grid=(B, T // tm),
        in_specs=[
            blk(D),
            _const_spec((1, D)),
            _const_spec(w_rowmajor.shape),
            _const_spec(w_transposed.shape),
            pl.BlockSpec((tm, tab.shape[1]), lambda b, t: (t, 0)),
            pl.BlockSpec((HEAD_DIM, tm), lambda b, t: (0, t)),
        ],
        out_specs=[blk(KV_W), blk(KV_W), blk(KV_PAD_W), blk(LANES), blk(LANES),
                   blk_t(Q_W), blk_t(QI_W), blk_t(N_IDX_HEADS),
                   pl.BlockSpec((1, tm // kbw, V_ROWS, kbw), lambda b, t: (b, t, 0, 0))],
        out_shape=[sds((B, T, KV_W), f32), sds((B, T, KV_W), f32), sds((B, T, KV_PAD_W), bf),
                   sds((B, T, LANES), f32), sds((B, T, LANES), bf),
                   sds((B, Q_W, T), bf), sds((B, QI_W, T), bf), sds((B, N_IDX_HEADS, T), f32),
                   sds((B, T // kbw, V_ROWS, kbw), bf)],
        compiler_params=pltpu.CompilerParams(
            dimension_semantics=("arbitrary", "arbitrary"), vmem_limit_bytes=VMEM_LIMIT),
        name="attn_in_proj",
    )(x, g, w_rowmajor, w_transposed, tab, tabt)


def _attn_kernel(qt_ref, qit_ref, wit_ref, kp_ref, vt_ref, kib_ref, o_ref,
                 key_ref, hi_ref, lo_ref, lo2_ref, f_ref, s0_ref, s1_ref, oacc_ref, *,
                 tq, tq_real, n_keys_real, pos0, n_sel):
    kb = KEY_BLOCK
    t = pl.program_id(1)
    lane_q = lax.broadcasted_iota(jnp.int32, (1, tq), 1)
    q_limit = ((pos0 + t * tq + lane_q) // CHUNK + 1) * CHUNK
    qvalid = lane_q < tq_real
    n_reach = jnp.minimum(((pos0 + (t + 1) * tq - 1) // CHUNK + 1) * CHUNK, n_keys_real)
    nkb = (n_reach + kb - 1) // kb
    n_unvisited = jnp.maximum(n_keys_real - nkb * kb, 0)
    key_row = lax.broadcasted_iota(jnp.int32, (kb, tq), 0)

    def block_start(j):
        return pl.multiple_of(j * kb, kb)

    def for_blocks(body, carry):
        start = 0
        for n in BLOCK_TIERS:
            def trip(i, c, n=n, start=start):
                for u in range(n):
                    c = body(start + n * i + u, c)
                return c
            trips = (nkb - start) // n
            carry = lax.fori_loop(0, trips, trip, carry)
            start = start + trips * n
        return carry

    def admissible(k0):
        return key_row < q_limit - k0

    wi = wit_ref[0]

    def score_block(j, carry):
        k0 = block_start(j)
        kx = kib_ref[0, pl.ds(k0, kb), 0:IDX_DIM]
        acc = jnp.zeros((kb, tq), jnp.float32)
        for hp in range(N_IDX_HEADS // 2):
            qi2 = jnp.concatenate(
                [qit_ref[0, (2 * hp + u) * IDX_DIM:(2 * hp + u + 1) * IDX_DIM, :] for u in range(2)], axis=1)
            lg = jnp.dot(kx, qi2, preferred_element_type=jnp.float32)
            for u in range(2):
                hh = 2 * hp + u
                acc = acc + jnp.maximum(lg[:, u * tq:(u + 1) * tq], 0.0) * wi[hh:hh + 1, :]
        sc = jnp.where(admissible(k0), acc, NEG)
        if n_keys_real % kb:
            sc = jnp.where(key_row < n_keys_real - k0, sc, -jnp.inf)
        bits = lax.bitcast_convert_type(sc, jnp.int32)
        key = bits ^ ((bits >> 31) & jnp.int32(0x7FFFFFFF))
        key_ref[pl.ds(k0, kb), :] = key
        hi_ref[pl.ds(k0, kb), :] = (key >> 16).astype(jnp.int16)
        lo_ref[pl.ds(k0, kb), :] = (key ^ 0x8000).astype(jnp.int16)
        return carry

    for_blocks(score_block, 0)

    def count16(ref, pred):
        def body(j, acc):
            hit = jnp.where(pred(ref[pl.ds(block_start(j), kb), :]), jnp.int16(1), jnp.int16(0))
            return acc + _fold_rows(jnp.add, hit, rows=PACKED_ROWS)
        acc = for_blocks(body, jnp.zeros((PACKED_ROWS, tq), jnp.int16))
        return jnp.sum(acc.astype(jnp.int32), axis=0, keepdims=True)

    def as_plane(v):
        tile = jnp.broadcast_to(v, (PACKED_ROWS, tq)).astype(jnp.int16)
        return jnp.concatenate([tile] * (kb // PACKED_ROWS), axis=0)

    def radix16(ref, n_wanted, n_all, unvisited_ge):
        def step(i, carry):
            thr, n_ge_thr, n_gt_thr = carry
            cand = thr + jnp.left_shift(jnp.int32(1), 15 - i)
            plane = as_plane(cand)
            n_ge = count16(ref, lambda blk: blk >= plane) + unvisited_ge(cand)
            take = n_ge >= n_wanted
            return jnp.where(take, cand, thr), jnp.where(take, n_ge, n_ge_thr), jnp.where(take, n_gt_thr, n_ge)
        start = (jnp.full((1, tq), -2 ** 15, jnp.int32), n_all, jnp.zeros((1, tq), jnp.int32))
        return lax.fori_loop(0, 16, step, start)

    n_all = jnp.broadcast_to(nkb * kb + n_unvisited, (1, tq))
    thr_hi, n_ge_hi, n_gt_hi = radix16(
        hi_ref, n_sel, n_all, lambda cand: jnp.where(cand <= NEG_KEY_HI, n_unvisited, 0))
    hi_plane = as_plane(thr_hi)

    def park_block(j, carry):
        rows = pl.ds(block_start(j), kb)
        lo2_ref[rows, :] = jnp.where(hi_ref[rows, :] == hi_plane, lo_ref[rows, :], jnp.int16(-2 ** 15))
        return carry

    for_blocks(park_block, 0)
    neg_in_bucket = thr_hi == NEG_KEY_HI
    thr_lo, n_ge_lo, n_gt_lo = radix16(
        lo2_ref, n_sel - n_gt_hi, n_ge_hi - n_gt_hi,
        lambda cand: jnp.where(neg_in_bucket & (cand <= NEG_KEY_LO), n_unvisited, 0))
    thr = thr_hi * 65536 + (thr_lo + 2 ** 15)
    n_gt = n_gt_hi + n_gt_lo
    n_eq = n_ge_lo - n_gt_lo
    need = n_sel - n_gt
    tie = (n_eq > need) & qvalid
    any_tie = jnp.max(tie.astype(jnp.int32)) > 0

    def mask_block(j, run, exact_ties):
        k0 = block_start(j)
        kblk = key_ref[pl.ds(k0, kb), :]
        if exact_ties:
            r = lax.broadcasted_iota(jnp.int32, (kb, kb), 0)
            c = lax.broadcasted_iota(jnp.int32, (kb, kb), 1)
            tri = jnp.where(r >= c, 1.0, 0.0).astype(jnp.bfloat16)
            eqb = kblk == thr
            eqf = jnp.where(eqb, 1.0, 0.0)
            rank = jnp.dot(tri, eqf.astype(jnp.bfloat16), preferred_element_type=jnp.float32) + run
            run = run + jnp.sum(eqf, axis=0, keepdims=True)
            sel = (kblk > thr) | (eqb & (rank <= need.astype(jnp.float32)))
        else:
            sel = kblk >= thr
        f = jnp.where(sel, jnp.where(admissible(k0), jnp.inf, NEG), -jnp.inf)
        f_ref[pl.ds(k0, kb), :] = f
        return f, run

    gw = KV_GROUP * tq
    s_refs = (s0_ref, s1_ref)

    def logits_block(g, j, f=None):
        k0 = block_start(j)
        q4 = jnp.concatenate(
            [qt_ref[0, (g * KV_GROUP + r) * HEAD_DIM:(g * KV_GROUP + r + 1) * HEAD_DIM, :] for r in range(KV_GROUP)],
            axis=1)
        kblk = kp_ref[0, pl.ds(k0, kb), g * LANES:g * LANES + HEAD_DIM]
        s = jnp.dot(kblk, q4, preferred_element_type=jnp.float32)
        if f is None:
            f = f_ref[pl.ds(k0, kb), :]
        s = jnp.minimum(s, jnp.concatenate([f] * KV_GROUP, axis=1))
        s_refs[g % 2][pl.ds(k0, kb), :] = s
        return _fold_rows(jnp.maximum, s)

    def values_block(g, j, m):
        p = jnp.exp2(s_refs[g % 2][pl.ds(block_start(j), kb), :] - m).astype(jnp.bfloat16)
        oacc_ref[...] += jnp.dot(vt_ref[0, j, g * V_SLOT:(g + 1) * V_SLOT, :], p,
                                 preferred_element_type=jnp.float32)

    m_init = jnp.full((SUBLANES, gw), -jnp.inf, jnp.float32)

    def first_pass(exact_ties):
        def body(j, carry):
            a, run = carry
            f, run = mask_block(j, run, exact_ties)
            return jnp.maximum(a, logits_block(0, j, f)), run
        return for_blocks(body, (m_init, jnp.zeros((1, tq), jnp.float32)))[0]

    m_acc = lax.cond(any_tie, lambda: first_pass(True), lambda: first_pass(False))
    for g in range(N_KV_HEADS):
        m = jnp.max(m_acc, axis=0, keepdims=True)
        oacc_ref[...] = jnp.zeros(oacc_ref.shape, jnp.float32)

        def fused(j, a, g=g, m=m):
            if g + 1 < N_KV_HEADS:
                a = jnp.maximum(a, logits_block(g + 1, j))
            values_block(g, j, m)
            return a

        m_acc = for_blocks(fused, m_init)
        oacc = oacc_ref[...]
        on = oacc[0:HEAD_DIM] * (1.0 / oacc[HEAD_DIM:HEAD_DIM + 1])
        for u in range(KV_GROUP // 2):
            pair = jnp.concatenate([on[:, (2 * u) * tq:(2 * u + 1) * tq], on[:, (2 * u + 1) * tq:(2 * u + 2) * tq]],
                                   axis=0)
            c0 = (g * KV_GROUP + 2 * u) * HEAD_DIM
            o_ref[0, :, c0:c0 + 2 * HEAD_DIM] = pair.T.astype(o_ref.dtype)


def _attn_layer(qt, qit, wit, kp, vt, kib, *, tq, tq_real, n_keys_real, pos0, n_sel):
    B, _, T = qt.shape
    n_keys = kp.shape[1]
    assert T % tq == 0 and tq % LANES == 0 and n_keys % KEY_BLOCK == 0 and n_keys_real <= n_keys, (qt.shape, kp.shape)
    assert n_sel <= n_keys_real, (n_sel, n_keys_real)
    qblk = lambda r: pl.BlockSpec((1, r, tq), lambda b, t: (b, 0, t))
    kblk = lambda w: pl.BlockSpec((1, n_keys, w), lambda b, t: (b, 0, 0))
    return pl.pallas_call(
        functools.partial(_attn_kernel, tq=tq, tq_real=tq_real, n_keys_real=n_keys_real, pos0=pos0, n_sel=n_sel),
        grid=(B, T // tq),
        in_specs=[qblk(Q_W), qblk(QI_W), qblk(N_IDX_HEADS), kblk(KV_PAD_W),
                  pl.BlockSpec((1,) + vt.shape[1:], lambda b, t: (b, 0, 0, 0)), kblk(LANES)],
        out_specs=pl.BlockSpec((1, tq, Q_W), lambda b, t: (b, t, 0)),
        out_shape=jax.ShapeDtypeStruct((B, T, Q_W), jnp.bfloat16),
        scratch_shapes=[pltpu.VMEM((n_keys, tq), jnp.int32)] + [pltpu.VMEM((n_keys, tq), jnp.int16)] * 3 + [
                        pltpu.VMEM((n_keys, tq), jnp.float32),
                        pltpu.VMEM((n_keys, KV_GROUP * tq), jnp.float32),
                        pltpu.VMEM((n_keys, KV_GROUP * tq), jnp.float32),
                        pltpu.VMEM((V_SLOT, KV_GROUP * tq), jnp.float32)],
        compiler_params=pltpu.CompilerParams(
            dimension_semantics=("arbitrary", "arbitrary"), vmem_limit_bytes=VMEM_LIMIT),
        name="dsa_attention",
    )(qt, qit, wit, kp, vt, kib)


def _pad_heads(a):
    lead = a.shape[:-1]
    a = a.reshape(*lead, N_KV_HEADS, HEAD_DIM)
    a = jnp.pad(a, [(0, 0)] * len(lead) + [(0, 0), (0, LANES - HEAD_DIM)])
    return a.reshape(*lead, KV_PAD_W)


def _pack_w_in(w_in):
    o = 0
    wq = w_in[:, o:o + Q_W] * (HEAD_DIM ** -0.5); o += Q_W
    wk = w_in[:, o:o + KV_W]; o += KV_W
    wv = w_in[:, o:o + KV_W]; o += KV_W
    wqi = w_in[:, o:o + QI_W] * (IDX_DIM ** -0.5); o += QI_W
    wki = w_in[:, o:o + IDX_DIM]; o += IDX_DIM
    wwi = w_in[:, o:o + N_IDX_HEADS]
    wkw = jnp.pad(jnp.concatenate([wki, wwi], axis=1), ((0, 0), (0, LANES - IDX_DIM - N_IDX_HEADS)))
    rowmajor = jnp.concatenate([wk, wv, wkw], axis=1)
    wwi_rows = jnp.pad(wwi, ((0, 0), (0, PACKED_ROWS - N_IDX_HEADS)))
    transposed = jnp.concatenate([wq, wqi, wv, wwi_rows], axis=1).T
    assert rowmajor.shape[1] == PROJ_ROWMAJOR_W and transposed.shape[0] == PROJ_TRANSPOSED_ROWS
    return rowmajor.astype(jnp.bfloat16), transposed.astype(jnp.bfloat16)


def _rope_tables(pos):
    inv_freq = 1.0 / (ROPE_THETA ** (jnp.arange(0, HEAD_DIM, 2, dtype=jnp.float32) / HEAD_DIM))
    ang = pos.astype(jnp.float32)[:, None] * inv_freq[None, :]
    c, s = jnp.cos(ang), jnp.sin(ang)
    z = jnp.zeros_like(c)
    n = pos.shape[0]
    zpad = jnp.zeros((n, LANES - HEAD_DIM), jnp.float32)
    wi_scale = jnp.full((n, N_IDX_HEADS), N_IDX_HEADS ** -0.5, jnp.float32)
    c_kw = jnp.concatenate([c, c, wi_scale, jnp.zeros((n, LANES - HEAD_DIM - N_IDX_HEADS), jnp.float32)], axis=1)
    tab = jnp.concatenate([
        jnp.concatenate([c, c, c, c], axis=1), c_kw,
        jnp.concatenate([-s, z, -s, z], axis=1), jnp.concatenate([-s, z, zpad], axis=1),
        jnp.concatenate([z, s, z, s], axis=1), jnp.concatenate([z, s, zpad], axis=1),
    ], axis=1)
    tabt = jnp.concatenate([c, s], axis=1).T
    return tab, tabt


def _round_up(n, m):
    return (n + m - 1) // m * m


def _value_blocks_t(v):
    B, n, _ = v.shape
    nb = n // KEY_BLOCK
    vt = jnp.transpose(v.reshape(B, nb, KEY_BLOCK, N_KV_HEADS, HEAD_DIM), (0, 1, 3, 4, 2))
    ones = jnp.ones((B, nb, N_KV_HEADS, 1, KEY_BLOCK), v.dtype)
    zeros = jnp.zeros((B, nb, N_KV_HEADS, V_SLOT - HEAD_DIM - 1, KEY_BLOCK), v.dtype)
    return jnp.concatenate([vt, ones, zeros], axis=3).reshape(B, nb, V_ROWS, KEY_BLOCK)


def _trunk(x, pos0, pool_past, attn_past, norm_mix, norm_mlp, norm_final, pool_w, pool_scale,
           w_rowmajor, w_transposed, w_o, w_up, w_down):
    B, T, D = x.shape
    bf = jnp.bfloat16
    row = lambda v: v.reshape(1, D)

    if pool_past is None:
        past16 = jnp.zeros((B, POOL_STATE + 1, D), jnp.float32)
    else:
        past16 = jnp.pad(pool_past, ((0, 0), (1, 0), (0, 0)))
    tt = min(T, ROW_TILE)
    tm = min(B * T, ROW_TILE)
    x1, st = _pool_layer(x, past16, row(norm_mix[0]), pool_w[0].astype(bf), row(pool_scale[0]), pos0, tt)
    x2 = _mlp_layer(x1.reshape(B * T, D), row(norm_mlp[0]), w_up[0], w_down[0], tm, FF_CHUNK).reshape(B, T, D)
    pool_new = st[:, 1:][None]

    tab, tabt = _rope_tables(pos0 + jnp.arange(T, dtype=jnp.int32))
    kc, vc, kp, kw, kib, qt, qit, wit, vt = _proj_layer(
        x2, row(norm_mix[1]), w_rowmajor, w_transposed, tab, tabt, min(T, PROJ_TILE))
    k_new = kc.reshape(1, B, T, N_KV_HEADS, HEAD_DIM)
    v_new = vc.reshape(1, B, T, N_KV_HEADS, HEAD_DIM)
    ki_new = kw[:, :, :IDX_DIM][None]

    if attn_past is None:
        n_real = T
        kp_all, vt_all, kib_all = kp, vt, kib
    else:
        ck, cv, cki = attn_past
        P = ck.shape[1]
        n_real = P + T
        n_keys = _round_up(n_real, KEY_BLOCK)
        fill = lambda w: jnp.zeros((B, n_keys - n_real, w), bf)
        kp_all = jnp.concatenate([_pad_heads(ck.reshape(B, P, KV_W).astype(bf)), kp, fill(KV_PAD_W)], axis=1)
        kib_all = jnp.concatenate(
            [jnp.pad(cki.astype(bf), ((0, 0), (0, 0), (0, LANES - IDX_DIM))), kib, fill(LANES)], axis=1)
        vt_all = _value_blocks_t(
            jnp.concatenate([cv.reshape(B, P, KV_W).astype(bf), vc.astype(bf), fill(KV_W)], axis=1))
    tq = min(Q_TILE, _round_up(T, LANES))
    t_pad = _round_up(T, tq)
    padq = lambda a: jnp.pad(a, ((0, 0), (0, 0), (0, t_pad - T)))
    n_sel = min(TOPK_MAX, n_real // 4)
    o = _attn_layer(padq(qt), padq(qit), padq(wit), kp_all, vt_all, kib_all,
                    tq=tq, tq_real=min(T, tq), n_keys_real=n_real, pos0=pos0, n_sel=n_sel)
    o = o[:, :T].reshape(B * T, Q_W)

    y = _mlp_layer(x2.reshape(B * T, D), row(norm_mlp[1]), w_up[1], w_down[1], tm, FF_CHUNK,
                   attn=(o, w_o), final_g=row(norm_final)).reshape(B, T, D)
    return y, pool_new, k_new, v_new, ki_new


def kernel(x_prompt, x_sample, state_pool, cache_k, cache_v, cache_kidx, norm_mix, norm_mlp, norm_final,
           pool_w, pool_scale, attn_w_in, attn_w_o, mlp_w_up, mlp_w_down):
    bf = jnp.bfloat16
    shared = (norm_mix, norm_mlp, norm_final, pool_w, pool_scale, *_pack_w_in(attn_w_in[0]),
              attn_w_o[0].astype(bf), mlp_w_up.astype(bf), mlp_w_down.astype(bf))
    y_p, pool_p, k_p, v_p, ki_p = _trunk(x_prompt, 0, None, None, *shared)
    y_s, pool_s, k_s, v_s, ki_s = _trunk(
        x_sample, cache_k.shape[2], state_pool[0], (cache_k[0], cache_v[0], cache_kidx[0]), *shared)
    return (y_p, y_s, pool_p, pool_s, k_p, v_p, ki_p, k_s, v_s, ki_s)
```

```python
import functools
import math
import struct

import jax
import jax.numpy as jnp
from jax import lax
from jax.experimental import pallas as pl
from jax.experimental.pallas import tpu as pltpu

D_MODEL = 1024
CHUNK = 64
POOL_WINDOWS = (2, 4, 8, 16)
POOL_GROUP = D_MODEL // len(POOL_WINDOWS)
POOL_STATE = max(POOL_WINDOWS) - 1
N_HEADS = 16
HEAD_DIM = 64
N_KV_HEADS = 4
KV_GROUP = N_HEADS // N_KV_HEADS
N_IDX_HEADS = 8
IDX_DIM = 64
TOPK_MAX = 256
ROPE_THETA = 10000.0
D_FF = 4 * D_MODEL
EPS = 1e-6
NEG = -1e30
Q_W = N_HEADS * HEAD_DIM
KV_W = N_KV_HEADS * HEAD_DIM
QI_W = N_IDX_HEADS * IDX_DIM
LOG2E = math.log2(math.e)

SUBLANES = 8
PACKED_ROWS = 2 * SUBLANES
LANES = 128
KV_PAD_W = N_KV_HEADS * LANES
V_SLOT = HEAD_DIM + PACKED_ROWS
V_ROWS = N_KV_HEADS * V_SLOT
KEY_BLOCK = 256
Q_TILE = 256
BLOCK_TIERS = (8, 4, 2, 1)
ROW_TILE = 1024
PROJ_TILE = 1024
FF_CHUNK = 512

OFF_KC = 0
OFF_VC = OFF_KC + KV_W
OFF_KW = OFF_VC + KV_W
PROJ_ROWMAJOR_W = OFF_KW + LANES
ROW_Q = 0
ROW_QI = ROW_Q + Q_W
ROW_V = ROW_QI + QI_W
ROW_WI = ROW_V + KV_W
PROJ_TRANSPOSED_ROWS = ROW_WI + PACKED_ROWS

VMEM_LIMIT = 56 * 1024 * 1024

_NT = (((1,), (1,)), ((), ()))


def _f32_order_key(x):
    b = struct.unpack("<i", struct.pack("<f", x))[0]
    return b ^ ((b >> 31) & 0x7FFFFFFF)


def _signed16(v):
    return v - (1 << 16) if v >= (1 << 15) else v


NEG_KEY = _f32_order_key(NEG)
NEG_KEY_HI = NEG_KEY >> 16
NEG_KEY_LO = _signed16((NEG_KEY & 0xFFFF) ^ 0x8000)


def _rms(x, g):
    return x * lax.rsqrt(jnp.mean(x * x, axis=-1, keepdims=True) + EPS) * g


def _const_spec(shape):
    n = len(shape)
    return pl.BlockSpec(shape, lambda *_: (0,) * n, pipeline_mode=pl.Buffered(1))


def _tree(op, parts):
    parts = list(parts)
    while len(parts) > 1:
        parts = [op(parts[i], parts[i + 1]) if i + 1 < len(parts) else parts[i] for i in range(0, len(parts), 2)]
    return parts[0]


def _fold_rows(op, x, rows=SUBLANES):
    return _tree(op, [x[r:r + rows] for r in range(0, x.shape[0], rows)])


def _pool_kernel(x_ref, past_ref, g_ref, w_ref, scale_ref, x1_ref, st_ref, hp_ref, sa_ref, sb_ref, *, tt, pos0):
    t = pl.program_id(1)
    halo = POOL_STATE + 1
    lead = SUBLANES
    top = lead + halo

    @pl.when(t == 0)
    def _():
        zeros = jnp.zeros((lead, hp_ref.shape[1]), jnp.float32)
        hp_ref[0:lead, :] = zeros
        sa_ref[0:lead, :] = zeros
        sb_ref[0:lead, :] = zeros
        hp_ref[lead:top, :] = past_ref[0]

    x = x_ref[0]
    h = _rms(x, g_ref[...])
    hp_ref[top:top + tt, :] = h

    n = halo + tt
    g1, g2, g3 = POOL_GROUP, 2 * POOL_GROUP, 3 * POOL_GROUP
    sa_ref[lead:lead + n, :] = hp_ref[lead:lead + n, :] + hp_ref[lead - 1:lead - 1 + n, :]
    sb_ref[lead:lead + n, g1:] = sa_ref[lead:lead + n, g1:] + sa_ref[lead - 2:lead - 2 + n, g1:]
    sa_ref[lead:lead + n, g2:] = sb_ref[lead:lead + n, g2:] + sb_ref[lead - 4:lead - 4 + n, g2:]
    window_sums = (
        sa_ref[top:top + tt, 0:g1],
        sb_ref[top:top + tt, g1:g2],
        sa_ref[top:top + tt, g2:g3],
        sa_ref[top:top + tt, g3:] + sa_ref[top - 8:top - 8 + tt, g3:],
    )
    pos = pos0 + t * tt + lax.broadcasted_iota(jnp.int32, (tt, 1), 0)
    for g, win in enumerate(POOL_WINDOWS):
        c0, c1 = g * POOL_GROUP, (g + 1) * POOL_GROUP
        cnt = jnp.minimum(pos + 1, win).astype(jnp.float32)
        diff = window_sums[g] / cnt - h[:, c0:c1]
        y = jnp.dot(diff.astype(jnp.bfloat16), w_ref[g], preferred_element_type=jnp.float32)
        x1_ref[0, :, c0:c1] = x[:, c0:c1] + y * scale_ref[:, c0:c1]
    tail = hp_ref[lead + tt:top + tt, :]
    st_ref[0] = tail
    hp_ref[lead:top, :] = tail


def _pool_layer(x, past16, g, w, scale, pos0, tt):
    B, T, D = x.shape
    assert T % tt == 0 and D == D_MODEL, (x.shape, tt)
    halo = POOL_STATE + 1
    plane = pltpu.VMEM((SUBLANES + halo + tt, D), jnp.float32)
    return pl.pallas_call(
        functools.partial(_pool_kernel, tt=tt, pos0=pos0),
        grid=(B, T // tt),
        in_specs=[
            pl.BlockSpec((1, tt, D), lambda b, t: (b, t, 0)),
            pl.BlockSpec((1, halo, D), lambda b, t: (b, 0, 0)),
            _const_spec((1, D)),
            _const_spec(w.shape),
            _const_spec((1, D)),
        ],
        out_specs=[
            pl.BlockSpec((1, tt, D), lambda b, t: (b, t, 0)),
            pl.BlockSpec((1, halo, D), lambda b, t: (b, 0, 0)),
        ],
        out_shape=[
            jax.ShapeDtypeStruct((B, T, D), jnp.float32),
            jax.ShapeDtypeStruct((B, halo, D), jnp.float32),
        ],
        scratch_shapes=[plane, plane, plane],
        compiler_params=pltpu.CompilerParams(
            dimension_semantics=("arbitrary", "arbitrary"), vmem_limit_bytes=VMEM_LIMIT),
        name="pool_mixer",
    )(x, past16, g, w, scale)


def _mlp_body(x, g_ref, up_ref, down_ref, fc):
    h = _rms(x, g_ref[...]).astype(jnp.bfloat16)
    acc = x
    for c in range(D_FF // fc):
        u = jnp.dot(h, up_ref[:, c * fc:(c + 1) * fc], preferred_element_type=jnp.float32)
        u = jnp.maximum(u, 0.0)
        a = (u * u).astype(jnp.bfloat16)
        acc = acc + jnp.dot(a, down_ref[c * fc:(c + 1) * fc, :], preferred_element_type=jnp.float32)
    return acc


def _mlp_kernel(*refs, fc, with_attn, with_final):
    refs = list(refs)
    x_ref = refs.pop(0)
    if with_attn:
        o_ref = refs.pop(0)
        wo_ref = refs.pop(0)
    g_ref = refs.pop(0)
    up_ref = refs.pop(0)
    down_ref = refs.pop(0)
    if with_final:
        gf_ref = refs.pop(0)
    out_ref = refs.pop(0)

    x = x_ref[...]
    if with_attn:
        x = x + jnp.dot(o_ref[...], wo_ref[...], preferred_element_type=jnp.float32)
    acc = _mlp_body(x, g_ref, up_ref, down_ref, fc)
    if with_final:
        acc = _rms(acc, gf_ref[...])
    out_ref[...] = acc


def _mlp_layer(x, g, up, down, tm, fc, attn=None, final_g=None):
    N, D = x.shape
    assert N % tm == 0 and D_FF % fc == 0, (x.shape, tm, fc)
    row = lambda i: (i, 0)
    args, specs = [x], [pl.BlockSpec((tm, D), row)]
    if attn is not None:
        o, wo = attn
        args += [o, wo]
        specs += [pl.BlockSpec((tm, Q_W), row), _const_spec(wo.shape)]
    args += [g, up, down]
    specs += [_const_spec((1, D)), _const_spec(up.shape), _const_spec(down.shape)]
    if final_g is not None:
        args.append(final_g)
        specs.append(_const_spec((1, D)))
    return pl.pallas_call(
        functools.partial(_mlp_kernel, fc=fc, with_attn=attn is not None, with_final=final_g is not None),
        grid=(N // tm,),
        in_specs=specs,
        out_specs=pl.BlockSpec((tm, D), row),
        out_shape=jax.ShapeDtypeStruct((N, D), jnp.float32),
        compiler_params=pltpu.CompilerParams(
            dimension_semantics=("arbitrary",), vmem_limit_bytes=VMEM_LIMIT),
        name="mlp_attn_out" if attn is not None else "mlp",
    )(*args)


def _rope_block(xb, tab_ref, special):
    o = LANES if special else 0
    c = tab_ref[:, o:o + LANES]
    s1 = tab_ref[:, 2 * LANES + o:3 * LANES + o]
    s2 = tab_ref[:, 4 * LANES + o:5 * LANES + o]
    return xb * c + pltpu.roll(xb, LANES - HEAD_DIM // 2, 1) * s1 + pltpu.roll(xb, HEAD_DIM // 2, 1) * s2


def _rope_rows(p, c, s):
    half = HEAD_DIM // 2
    r = p.reshape(p.shape[0] // HEAD_DIM, 2, half, p.shape[1])
    x1, x2 = r[:, 0], r[:, 1]
    out = jnp.stack([x1 * c - x2 * s, x2 * c + x1 * s], axis=1)
    return out.reshape(p.shape)


def _proj_kernel(x_ref, g_ref, w_ref, wt_ref, tab_ref, tabt_ref,
                 kc_ref, vc_ref, kp_ref, kw_ref, kib_ref, qt_ref, qit_ref, wit_ref, vt_ref):
    h = _rms(x_ref[0], g_ref[...]).astype(jnp.bfloat16)

    def proj(off, width):
        return jnp.dot(h, w_ref[:, off:off + width], preferred_element_type=jnp.float32)

    def rope(p, special=False):
        return jnp.concatenate(
            [_rope_block(p[:, j:j + LANES], tab_ref, special) for j in range(0, p.shape[1], LANES)], axis=-1)

    kc = rope(proj(OFF_KC, KV_W))
    kc_ref[0] = kc
    vc_ref[0] = proj(OFF_VC, KV_W)
    gap = jnp.zeros((kc.shape[0], LANES - HEAD_DIM), jnp.float32)
    kp = jnp.concatenate(
        [piece for g in range(N_KV_HEADS) for piece in (kc[:, g * HEAD_DIM:(g + 1) * HEAD_DIM], gap)], axis=1)
    kp_ref[0] = kp.astype(kp_ref.dtype)
    kw = rope(proj(OFF_KW, LANES), special=True)
    kw_ref[0] = kw
    kib_ref[0] = kw.astype(kib_ref.dtype)

    def proj_t(row, n):
        return lax.dot_general(wt_ref[row:row + n, :], h, _NT, preferred_element_type=jnp.float32)

    half = HEAD_DIM // 2
    c, s = tabt_ref[0:half, :], tabt_ref[half:HEAD_DIM, :]
    qt_ref[0] = _rope_rows(proj_t(ROW_Q, Q_W), c * LOG2E, s * LOG2E).astype(qt_ref.dtype)
    qit_ref[0] = _rope_rows(proj_t(ROW_QI, QI_W), c, s).astype(qit_ref.dtype)
    vw = proj_t(ROW_V, KV_W + PACKED_ROWS)
    wit_ref[0] = vw[KV_W:KV_W + N_IDX_HEADS, :] * (N_IDX_HEADS ** -0.5)
    n = vw.shape[1]
    tail = jnp.where(lax.broadcasted_iota(jnp.int32, (V_SLOT - HEAD_DIM, n), 0) == 0, 1.0, 0.0)
    vt = jnp.concatenate(
        [piece for g in range(N_KV_HEADS) for piece in (vw[g * HEAD_DIM:(g + 1) * HEAD_DIM], tail)], axis=0)
    kbw = vt_ref.shape[3]
    for c in range(vt_ref.shape[1]):
        vt_ref[0, c] = vt[:, c * kbw:(c + 1) * kbw].astype(vt_ref.dtype)


def _proj_layer(x, g, w_rowmajor, w_transposed, tab, tabt, tm):
    B, T, D = x.shape
    blk = lambda w: pl.BlockSpec((1, tm, w), lambda b, t: (b, t, 0))
    blk_t = lambda r: pl.BlockSpec((1, r, tm), lambda b, t: (b, 0, t))
    bf, f32 = jnp.bfloat16, jnp.float32
    sds = jax.ShapeDtypeStruct
    kbw = min(tm, KEY_BLOCK)
    assert T % tm == 0 and tm % kbw == 0, (x.shape, tm)
    return pl.pallas_call(
        _proj_kernel,
        grid=(B, T // tm),
        in_specs=[
            blk(D),
            _const_spec((1, D)),
            _const_spec(w_rowmajor.shape),
            _const_spec(w_transposed.shape),
            pl.BlockSpec((tm, tab.shape[1]), lambda b, t: (t, 0)),
            pl.BlockSpec((HEAD_DIM, tm), lambda b, t: (0, t)),
        ],
        out_specs=[blk(KV_W), blk(KV_W), blk(KV_PAD_W), blk(LANES), blk(LANES),
                   blk_t(Q_W), blk_t(QI_W), blk_t(N_IDX_HEADS),
                   pl.BlockSpec((1, tm // kbw, V_ROWS, kbw), lambda b, t: (b, t, 0, 0))],
        out_shape=[sds((B, T, KV_W), f32), sds((B, T, KV_W), f32), sds((B, T, KV_PAD_W), bf),
                   sds((B, T, LANES), f32), sds((B, T, LANES), bf),
                   sds((B, Q_W, T), bf), sds((B, QI_W, T), bf), sds((B, N_IDX_HEADS, T), f32),
                   sds((B, T // kbw, V_ROWS, kbw), bf)],
        compiler_params=pltpu.CompilerParams(
            dimension_semantics=("arbitrary", "arbitrary"), vmem_limit_bytes=VMEM_LIMIT),
        name="attn_in_proj",
    )(x, g, w_rowmajor, w_transposed, tab, tabt)


def _attn_kernel(qt_ref, qit_ref, wit_ref, kp_ref, vt_ref, kib_ref, o_ref,
                 key_ref, hi_ref, lo_ref, lo2_ref, f_ref, s0_ref, s1_ref, oacc_ref, *,
                 tq, tq_real, n_keys_real, pos0, n_sel):
    kb = KEY_BLOCK
    t = pl.program_id(1)
    lane_q = lax.broadcasted_iota(jnp.int32, (1, tq), 1)
    q_limit = ((pos0 + t * tq + lane_q) // CHUNK + 1) * CHUNK
    qvalid = lane_q < tq_real
    n_reach = jnp.minimum(((pos0 + (t + 1) * tq - 1) // CHUNK + 1) * CHUNK, n_keys_real)
    nkb = (n_reach + kb - 1) // kb
    n_unvisited = jnp.maximum(n_keys_real - nkb * kb, 0)
    key_row = lax.broadcasted_iota(jnp.int32, (kb, tq), 0)

    def block_start(j):
        return pl.multiple_of(j * kb, kb)

    def for_blocks(body, carry):
        start = 0
        for n in BLOCK_TIERS:
            def trip(i, c, n=n, start=start):
                for u in range(n):
                    c = body(start + n * i + u, c)
                return c
            trips = (nkb - start) // n
            carry = lax.fori_loop(0, trips, trip, carry)
            start = start + trips * n
        return carry

    def admissible(k0):
        return key_row < q_limit - k0

    wi = wit_ref[0]

    def score_block(j, carry):
        k0 = block_start(j)
        kx = kib_ref[0, pl.ds(k0, kb), 0:IDX_DIM]
        acc = jnp.zeros((kb, tq), jnp.float32)
        for hp in range(N_IDX_HEADS // 2):
            qi2 = jnp.concatenate(
                [qit_ref[0, (2 * hp + u) * IDX_DIM:(2 * hp + u + 1) * IDX_DIM, :] for u in range(2)], axis=1)
            lg = jnp.dot(kx, qi2, preferred_element_type=jnp.float32)
            for u in range(2):
                hh = 2 * hp + u
                acc = acc + jnp.maximum(lg[:, u * tq:(u + 1) * tq], 0.0) * wi[hh:hh + 1, :]
        sc = jnp.where(admissible(k0), acc, NEG)
        if n_keys_real % kb:
            sc = jnp.where(key_row < n_keys_real - k0, sc, -jnp.inf)
        bits = lax.bitcast_convert_type(sc, jnp.int32)
        key = bits ^ ((bits >> 31) & jnp.int32(0x7FFFFFFF))
        key_ref[pl.ds(k0, kb), :] = key
        hi_ref[pl.ds(k0, kb), :] = (key >> 16).astype(jnp.int16)
        lo_ref[pl.ds(k0, kb), :] = (key ^ 0x8000).astype(jnp.int16)
        return carry

    for_blocks(score_block, 0)

    def count16(ref, pred):
        def body(j, acc):
            hit = jnp.where(pred(ref[pl.ds(block_start(j), kb), :]), jnp.int16(1), jnp.int16(0))
            return acc + _fold_rows(jnp.add, hit, rows=PACKED_ROWS)
        acc = for_blocks(body, jnp.zeros((PACKED_ROWS, tq), jnp.int16))
        return jnp.sum(acc.astype(jnp.int32), axis=0, keepdims=True)

    def as_plane(v):
        tile = jnp.broadcast_to(v, (PACKED_ROWS, tq)).astype(jnp.int16)
        return jnp.concatenate([tile] * (kb // PACKED_ROWS), axis=0)

    def radix16(ref, n_wanted, n_all, unvisited_ge):
        def step(i, carry):
            thr, n_ge_thr, n_gt_thr = carry
            cand = thr + jnp.left_shift(jnp.int32(1), 15 - i)
            plane = as_plane(cand)
            n_ge = count16(ref, lambda blk: blk >= plane) + unvisited_ge(cand)
            take = n_ge >= n_wanted
            return jnp.where(take, cand, thr), jnp.where(take, n_ge, n_ge_thr), jnp.where(take, n_gt_thr, n_ge)
        start = (jnp.full((1, tq), -2 ** 15, jnp.int32), n_all, jnp.zeros((1, tq), jnp.int32))
        return lax.fori_loop(0, 16, step, start)

    n_all = jnp.broadcast_to(nkb * kb + n_unvisited, (1, tq))
    thr_hi, n_ge_hi, n_gt_hi = radix16(
        hi_ref, n_sel, n_all, lambda cand: jnp.where(cand <= NEG_KEY_HI, n_unvisited, 0))
    hi_plane = as_plane(thr_hi)

    def park_block(j, carry):
        rows = pl.ds(block_start(j), kb)
        lo2_ref[rows, :] = jnp.where(hi_ref[rows, :] == hi_plane, lo_ref[rows, :], jnp.int16(-2 ** 15))
        return carry

    for_blocks(park_block, 0)
    neg_in_bucket = thr_hi == NEG_KEY_HI
    thr_lo, n_ge_lo, n_gt_lo = radix16(
        lo2_ref, n_sel - n_gt_hi, n_ge_hi - n_gt_hi,
        lambda cand: jnp.where(neg_in_bucket & (cand <= NEG_KEY_LO), n_unvisited, 0))
    thr = thr_hi * 65536 + (thr_lo + 2 ** 15)
    n_gt = n_gt_hi + n_gt_lo
    n_eq = n_ge_lo - n_gt_lo
    need = n_sel - n_gt
    tie = (n_eq > need) & qvalid
    any_tie = jnp.max(tie.astype(jnp.int32)) > 0

    def mask_block(j, run, exact_ties):
        k0 = block_start(j)
        kblk = key_ref[pl.ds(k0, kb), :]
        if exact_ties:
            r = lax.broadcasted_iota(jnp.int32, (kb, kb), 0)
            c = lax.broadcasted_iota(jnp.int32, (kb, kb), 1)
            tri = jnp.where(r >= c, 1.0, 0.0).astype(jnp.bfloat16)
            eqb = kblk == thr
            eqf = jnp.where(eqb, 1.0, 0.0)
            rank = jnp.dot(tri, eqf.astype(jnp.bfloat16), preferred_element_type=jnp.float32) + run
            run = run + jnp.sum(eqf, axis=0, keepdims=True)
            sel = (kblk > thr) | (eqb & (rank <= need.astype(jnp.float32)))
        else:
            sel = kblk >= thr
        f = jnp.where(sel, jnp.where(admissible(k0), jnp.inf, NEG), -jnp.inf)
        f_ref[pl.ds(k0, kb), :] = f
        return f, run

    gw = KV_GROUP * tq
    s_refs = (s0_ref, s1_ref)

    def logits_block(g, j, f=None):
        k0 = block_start(j)
        q4 = jnp.concatenate(
            [qt_ref[0, (g * KV_GROUP + r) * HEAD_DIM:(g * KV_GROUP + r + 1) * HEAD_DIM, :] for r in range(KV_GROUP)],
            axis=1)
        kblk = kp_ref[0, pl.ds(k0, kb), g * LANES:g * LANES + HEAD_DIM]
        s = jnp.dot(kblk, q4, preferred_element_type=jnp.float32)
        if f is None:
            f = f_ref[pl.ds(k0, kb), :]
        s = jnp.minimum(s, jnp.concatenate([f] * KV_GROUP, axis=1))
        s_refs[g % 2][pl.ds(k0, kb), :] = s
        return _fold_rows(jnp.maximum, s)

    def values_block(g, j, m):
        p = jnp.exp2(s_refs[g % 2][pl.ds(block_start(j), kb), :] - m).astype(jnp.bfloat16)
        oacc_ref[...] += jnp.dot(vt_ref[0, j, g * V_SLOT:(g + 1) * V_SLOT, :], p,
                                 preferred_element_type=jnp.float32)

    m_init = jnp.full((SUBLANES, gw), -jnp.inf, jnp.float32)

    def first_pass(exact_ties):
        def body(j, carry):
            a, run = carry
            f, run = mask_block(j, run, exact_ties)
            return jnp.maximum(a, logits_block(0, j, f)), run
        return for_blocks(body, (m_init, jnp.zeros((1, tq), jnp.float32)))[0]

    m_acc = lax.cond(any_tie, lambda: first_pass(True), lambda: first_pass(False))
    for g in range(N_KV_HEADS):
        m = jnp.max(m_acc, axis=0, keepdims=True)
        oacc_ref[...] = jnp.zeros(oacc_ref.shape, jnp.float32)

        def fused(j, a, g=g, m=m):
            if g + 1 < N_KV_HEADS:
                a = jnp.maximum(a, logits_block(g + 1, j))
            values_block(g, j, m)
            return a

        m_acc = for_blocks(fused, m_init)
        oacc = oacc_ref[...]
        on = oacc[0:HEAD_DIM] * (1.0 / oacc[HEAD_DIM:HEAD_DIM + 1])
        for u in range(KV_GROUP // 2):
            pair = jnp.concatenate([on[:, (2 * u) * tq:(2 * u + 1) * tq], on[:, (2 * u + 1) * tq:(2 * u + 2) * tq]],
                                   axis=0)
            c0 = (g * KV_GROUP + 2 * u) * HEAD_DIM
            o_ref[0, :, c0:c0 + 2 * HEAD_DIM] = pair.T.astype(o_ref.dtype)


def _attn_layer(qt, qit, wit, kp, vt, kib, *, tq, tq_real, n_keys_real, pos0, n_sel):
    B, _, T = qt.shape
    n_keys = kp.shape[1]
    assert T % tq == 0 and tq % LANES == 0 and n_keys % KEY_BLOCK == 0 and n_keys_real <= n_keys, (qt.shape, kp.shape)
    assert n_sel <= n_keys_real, (n_sel, n_keys_real)
    qblk = lambda r: pl.BlockSpec((1, r, tq), lambda b, t: (b, 0, t))
    kblk = lambda w: pl.BlockSpec((1, n_keys, w), lambda b, t: (b, 0, 0))
    return pl.pallas_call(
        functools.partial(_attn_kernel, tq=tq, tq_real=tq_real, n_keys_real=n_keys_real, pos0=pos0, n_sel=n_sel),
        grid=(B, T // tq),
        in_specs=[qblk(Q_W), qblk(QI_W), qblk(N_IDX_HEADS), kblk(KV_PAD_W),
                  pl.BlockSpec((1,) + vt.shape[1:], lambda b, t: (b, 0, 0, 0)), kblk(LANES)],
        out_specs=pl.BlockSpec((1, tq, Q_W), lambda b, t: (b, t, 0)),
        out_shape=jax.ShapeDtypeStruct((B, T, Q_W), jnp.bfloat16),
        scratch_shapes=[pltpu.VMEM((n_keys, tq), jnp.int32)] + [pltpu.VMEM((n_keys, tq), jnp.int16)] * 3 + [
                        pltpu.VMEM((n_keys, tq), jnp.float32),
                        pltpu.VMEM((n_keys, KV_GROUP * tq), jnp.float32),
                        pltpu.VMEM((n_keys, KV_GROUP * tq), jnp.float32),
                        pltpu.VMEM((V_SLOT, KV_GROUP * tq), jnp.float32)],
        compiler_params=pltpu.CompilerParams(
            dimension_semantics=("arbitrary", "arbitrary"), vmem_limit_bytes=VMEM_LIMIT),
        name="dsa_attention",
    )(qt, qit, wit, kp, vt, kib)


def _pad_heads(a):
    lead = a.shape[:-1]
    a = a.reshape(*lead, N_KV_HEADS, HEAD_DIM)
    a = jnp.pad(a, [(0, 0)] * len(lead) + [(0, 0), (0, LANES - HEAD_DIM)])
    return a.reshape(*lead, KV_PAD_W)


def _pack_w_in(w_in):
    o = 0
    wq = w_in[:, o:o + Q_W] * (HEAD_DIM ** -0.5); o += Q_W
    wk = w_in[:, o:o + KV_W]; o += KV_W
    wv = w_in[:, o:o + KV_W]; o += KV_W
    wqi = w_in[:, o:o + QI_W] * (IDX_DIM ** -0.5); o += QI_W
    wki = w_in[:, o:o + IDX_DIM]; o += IDX_DIM
    wwi = w_in[:, o:o + N_IDX_HEADS]
    wkw = jnp.pad(jnp.concatenate([wki, wwi], axis=1), ((0, 0), (0, LANES - IDX_DIM - N_IDX_HEADS)))
    rowmajor = jnp.concatenate([wk, wv, wkw], axis=1)
    wwi_rows = jnp.pad(wwi, ((0, 0), (0, PACKED_ROWS - N_IDX_HEADS)))
    transposed = jnp.concatenate([wq, wqi, wv, wwi_rows], axis=1).T
    assert rowmajor.shape[1] == PROJ_ROWMAJOR_W and transposed.shape[0] == PROJ_TRANSPOSED_ROWS
    return rowmajor.astype(jnp.bfloat16), transposed.astype(jnp.bfloat16)


def _rope_tables(pos):
    inv_freq = 1.0 / (ROPE_THETA ** (jnp.arange(0, HEAD_DIM, 2, dtype=jnp.float32) / HEAD_DIM))
    ang = pos.astype(jnp.float32)[:, None] * inv_freq[None, :]
    c, s = jnp.cos(ang), jnp.sin(ang)
    z = jnp.zeros_like(c)
    n = pos.shape[0]
    zpad = jnp.zeros((n, LANES - HEAD_DIM), jnp.float32)
    wi_scale = jnp.full((n, N_IDX_HEADS), N_IDX_HEADS ** -0.5, jnp.float32)
    c_kw = jnp.concatenate([c, c, wi_scale, jnp.zeros((n, LANES - HEAD_DIM - N_IDX_HEADS), jnp.float32)], axis=1)
    tab = jnp.concatenate([
        jnp.concatenate([c, c, c, c], axis=1), c_kw,
        jnp.concatenate([-s, z, -s, z], axis=1), jnp.concatenate([-s, z, zpad], axis=1),
        jnp.concatenate([z, s, z, s], axis=1), jnp.concatenate([z, s, zpad], axis=1),
    ], axis=1)
    tabt = jnp.concatenate([c, s], axis=1).T
    return tab, tabt


def _round_up(n, m):
    return (n + m - 1) // m * m


def _value_blocks_t(v):
    B, n, _ = v.shape
    nb = n // KEY_BLOCK
    vt = jnp.transpose(v.reshape(B, nb, KEY_BLOCK, N_KV_HEADS, HEAD_DIM), (0, 1, 3, 4, 2))
    ones = jnp.ones((B, nb, N_KV_HEADS, 1, KEY_BLOCK), v.dtype)
    zeros = jnp.zeros((B, nb, N_KV_HEADS, V_SLOT - HEAD_DIM - 1, KEY_BLOCK), v.dtype)
    return jnp.concatenate([vt, ones, zeros], axis=3).reshape(B, nb, V_ROWS, KEY_BLOCK)


def _trunk(x, pos0, pool_past, attn_past, norm_mix, norm_mlp, norm_final, pool_w, pool_scale,
           w_rowmajor, w_transposed, w_o, w_up, w_down):
    B, T, D = x.shape
    bf = jnp.bfloat16
    row = lambda v: v.reshape(1, D)

    if pool_past is None:
        past16 = jnp.zeros((B, POOL_STATE + 1, D), jnp.float32)
    else:
        past16 = jnp.pad(pool_past, ((0, 0), (1, 0), (0, 0)))
    tt = min(T, ROW_TILE)
    tm = min(B * T, ROW_TILE)
    x1, st = _pool_layer(x, past16, row(norm_mix[0]), pool_w[0].astype(bf), row(pool_scale[0]), pos0, tt)
    x2 = _mlp_layer(x1.reshape(B * T, D), row(norm_mlp[0]), w_up[0], w_down[0], tm, FF_CHUNK).reshape(B, T, D)
    pool_new = st[:, 1:][None]

    tab, tabt = _rope_tables(pos0 + jnp.arange(T, dtype=jnp.int32))
    kc, vc, kp, kw, kib, qt, qit, wit, vt = _proj_layer(
        x2, row(norm_mix[1]), w_rowmajor, w_transposed, tab, tabt, min(T, PROJ_TILE))
    k_new = kc.reshape(1, B, T, N_KV_HEADS, HEAD_DIM)
    v_new = vc.reshape(1, B, T, N_KV_HEADS, HEAD_DIM)
    ki_new = kw[:, :, :IDX_DIM][None]

    if attn_past is None:
        n_real = T
        kp_all, vt_all, kib_all = kp, vt, kib
    else:
        ck, cv, cki = attn_past
        P = ck.shape[1]
        n_real = P + T
        n_keys = _round_up(n_real, KEY_BLOCK)
        fill = lambda w: jnp.zeros((B, n_keys - n_real, w), bf)
        kp_all = jnp.concatenate([_pad_heads(ck.reshape(B, P, KV_W).astype(bf)), kp, fill(KV_PAD_W)], axis=1)
        kib_all = jnp.concatenate(
            [jnp.pad(cki.astype(bf), ((0, 0), (0, 0), (0, LANES - IDX_DIM))), kib, fill(LANES)], axis=1)
        vt_all = _value_blocks_t(
            jnp.concatenate([cv.reshape(B, P, KV_W).astype(bf), vc.astype(bf), fill(KV_W)], axis=1))
    tq = min(Q_TILE, _round_up(T, LANES))
    t_pad = _round_up(T, tq)
    padq = lambda a: jnp.pad(a, ((0, 0), (0, 0), (0, t_pad - T)))
    n_sel = min(TOPK_MAX, n_real // 4)
    o = _attn_layer(padq(qt), padq(qit), padq(wit), kp_all, vt_all, kib_all,
                    tq=tq, tq_real=min(T, tq), n_keys_real=n_real, pos0=pos0, n_sel=n_sel)
    o = o[:, :T].reshape(B * T, Q_W)

    y = _mlp_layer(x2.reshape(B * T, D), row(norm_mlp[1]), w_up[1], w_down[1], tm, FF_CHUNK,
                   attn=(o, w_o), final_g=row(norm_final)).reshape(B, T, D)
    return y, pool_new, k_new, v_new, ki_new


def kernel(x_prompt, x_sample, state_pool, cache_k, cache_v, cache_kidx, norm_mix, norm_mlp, norm_final,
           pool_w, pool_scale, attn_w_in, attn_w_o, mlp_w_up, mlp_w_down):
    bf = jnp.bfloat16
    shared = (norm_mix, norm_mlp, norm_final, pool_w, pool_scale, *_pack_w_in(attn_w_in[0]),
              attn_w_o[0].astype(bf), mlp_w_up.astype(bf), mlp_w_down.astype(bf))
    y_p, pool_p, k_p, v_p, ki_p = _trunk(x_prompt, 0, None, None, *shared)
    y_s, pool_s, k_s, v_s, ki_s = _trunk(
        x_sample, cache_k.shape[2], state_pool[0], (cache_k[0], cache_v[0], cache_kidx[0]), *shared)
    return (y_p, y_s, pool_p, pool_s, k_p, v_p, ki_p, k_s, v_s, ki_s)
```

```python
import functools
import math
import struct

import jax
import jax.numpy as jnp
from jax import lax
from jax.experimental import pallas as pl
from jax.experimental.pallas import tpu as pltpu

D_MODEL = 1024
CHUNK = 64
POOL_WINDOWS = (2, 4, 8, 16)
POOL_GROUP = D_MODEL // len(POOL_WINDOWS)
POOL_STATE = max(POOL_WINDOWS) - 1
N_HEADS = 16
HEAD_DIM = 64
N_KV_HEADS = 4
KV_GROUP = N_HEADS // N_KV_HEADS
N_IDX_HEADS = 8
IDX_DIM = 64
TOPK_MAX = 256
ROPE_THETA = 10000.0
D_FF = 4 * D_MODEL
EPS = 1e-6
NEG = -1e30
Q_W = N_HEADS * HEAD_DIM
KV_W = N_KV_HEADS * HEAD_DIM
QI_W = N_IDX_HEADS * IDX_DIM
LOG2E = math.log2(math.e)

SUBLANES = 8
PACKED_ROWS = 2 * SUBLANES
LANES = 128
KV_PAD_W = N_KV_HEADS * LANES
V_SLOT = HEAD_DIM + PACKED_ROWS
V_ROWS = N_KV_HEADS * V_SLOT
KEY_BLOCK = 256
Q_TILE = 256
BLOCK_TIERS = (4, 2, 1)
ROW_TILE = 1024
PROJ_TILE = 1024
FF_CHUNK = 512

OFF_KC = 0
OFF_VC = OFF_KC + KV_W
OFF_KW = OFF_VC + KV_W
PROJ_ROWMAJOR_W = OFF_KW + LANES
ROW_Q = 0
ROW_QI = ROW_Q + Q_W
ROW_V = ROW_QI + QI_W
ROW_WI = ROW_V + KV_W
PROJ_TRANSPOSED_ROWS = ROW_WI + PACKED_ROWS

VMEM_LIMIT = 56 * 1024 * 1024

_NT = (((1,), (1,)), ((), ()))


def _f32_order_key(x):
    b = struct.unpack("<i", struct.pack("<f", x))[0]
    return b ^ ((b >> 31) & 0x7FFFFFFF)


def _signed16(v):
    return v - (1 << 16) if v >= (1 << 15) else v


NEG_KEY = _f32_order_key(NEG)
NEG_KEY_HI = NEG_KEY >> 16
NEG_KEY_LO = _signed16((NEG_KEY & 0xFFFF) ^ 0x8000)


def _rms(x, g):
    return x * lax.rsqrt(jnp.mean(x * x, axis=-1, keepdims=True) + EPS) * g


def _const_spec(shape):
    n = len(shape)
    return pl.BlockSpec(shape, lambda *_: (0,) * n, pipeline_mode=pl.Buffered(1))


def _tree(op, parts):
    parts = list(parts)
    while len(parts) > 1:
        parts = [op(parts[i], parts[i + 1]) if i + 1 < len(parts) else parts[i] for i in range(0, len(parts), 2)]
    return parts[0]


def _fold_rows(op, x, rows=SUBLANES):
    return _tree(op, [x[r:r + rows] for r in range(0, x.shape[0], rows)])


def _pool_kernel(x_ref, past_ref, g_ref, w_ref, scale_ref, x1_ref, st_ref, hp_ref, sa_ref, sb_ref, *, tt, pos0):
    t = pl.program_id(1)
    halo = POOL_STATE + 1
    lead = SUBLANES
    top = lead + halo

    @pl.when(t == 0)
    def _():
        zeros = jnp.zeros((lead, hp_ref.shape[1]), jnp.float32)
        hp_ref[0:lead, :] = zeros
        sa_ref[0:lead, :] = zeros
        sb_ref[0:lead, :] = zeros
        hp_ref[lead:top, :] = past_ref[0]

    x = x_ref[0]
    h = _rms(x, g_ref[...])
    hp_ref[top:top + tt, :] = h

    n = halo + tt
    g1, g2, g3 = POOL_GROUP, 2 * POOL_GROUP, 3 * POOL_GROUP
    sa_ref[lead:lead + n, :] = hp_ref[lead:lead + n, :] + hp_ref[lead - 1:lead - 1 + n, :]
    sb_ref[lead:lead + n, g1:] = sa_ref[lead:lead + n, g1:] + sa_ref[lead - 2:lead - 2 + n, g1:]
    sa_ref[lead:lead + n, g2:] = sb_ref[lead:lead + n, g2:] + sb_ref[lead - 4:lead - 4 + n, g2:]
    window_sums = (
        sa_ref[top:top + tt, 0:g1],
        sb_ref[top:top + tt, g1:g2],
        sa_ref[top:top + tt, g2:g3],
        sa_ref[top:top + tt, g3:] + sa_ref[top - 8:top - 8 + tt, g3:],
    )
    pos = pos0 + t * tt + lax.broadcasted_iota(jnp.int32, (tt, 1), 0)
    for g, win in enumerate(POOL_WINDOWS):
        c0, c1 = g * POOL_GROUP, (g + 1) * POOL_GROUP
        cnt = jnp.minimum(pos + 1, win).astype(jnp.float32)
        diff = window_sums[g] / cnt - h[:, c0:c1]
        y = jnp.dot(diff.astype(jnp.bfloat16), w_ref[g], preferred_element_type=jnp.float32)
        x1_ref[0, :, c0:c1] = x[:, c0:c1] + y * scale_ref[:, c0:c1]
    tail = hp_ref[lead + tt:top + tt, :]
    st_ref[0] = tail
    hp_ref[lead:top, :] = tail


def _pool_layer(x, past16, g, w, scale, pos0, tt):
    B, T, D = x.shape
    assert T % tt == 0 and D == D_MODEL, (x.shape, tt)
    halo = POOL_STATE + 1
    plane = pltpu.VMEM((SUBLANES + halo + tt, D), jnp.float32)
    return pl.pallas_call(
        functools.partial(_pool_kernel, tt=tt, pos0=pos0),
        grid=(B, T // tt),
        in_specs=[
            pl.BlockSpec((1, tt, D), lambda b, t: (b, t, 0)),
            pl.BlockSpec((1, halo, D), lambda b, t: (b, 0, 0)),
            _const_spec((1, D)),
            _const_spec(w.shape),
            _const_spec((1, D)),
        ],
        out_specs=[
            pl.BlockSpec((1, tt, D), lambda b, t: (b, t, 0)),
            pl.BlockSpec((1, halo, D), lambda b, t: (b, 0, 0)),
        ],
        out_shape=[
            jax.ShapeDtypeStruct((B, T, D), jnp.float32),
            jax.ShapeDtypeStruct((B, halo, D), jnp.float32),
        ],
        scratch_shapes=[plane, plane, plane],
        compiler_params=pltpu.CompilerParams(
            dimension_semantics=("parallel", "arbitrary"), vmem_limit_bytes=VMEM_LIMIT),
        name="pool_mixer",
    )(x, past16, g, w, scale)


def _mlp_body(x, g_ref, up_ref, down_ref, fc):
    h = _rms(x, g_ref[...]).astype(jnp.bfloat16)
    acc = x
    for c in range(D_FF // fc):
        u = jnp.dot(h, up_ref[:, c * fc:(c + 1) * fc], preferred_element_type=jnp.float32)
        u = jnp.maximum(u, 0.0)
        a = (u * u).astype(jnp.bfloat16)
        acc = acc + jnp.dot(a, down_ref[c * fc:(c + 1) * fc, :], preferred_element_type=jnp.float32)
    return acc


def _mlp_kernel(*refs, fc, with_attn, with_final):
    refs = list(refs)
    x_ref = refs.pop(0)
    if with_attn:
        o_ref = refs.pop(0)
        wo_ref = refs.pop(0)
    g_ref = refs.pop(0)
    up_ref = refs.pop(0)
    down_ref = refs.pop(0)
    if with_final:
        gf_ref = refs.pop(0)
    out_ref = refs.pop(0)

    x = x_ref[...]
    if with_attn:
        x = x + jnp.dot(o_ref[...], wo_ref[...], preferred_element_type=jnp.float32)
    acc = _mlp_body(x, g_ref, up_ref, down_ref, fc)
    if with_final:
        acc = _rms(acc, gf_ref[...])
    out_ref[...] = acc


def _mlp_layer(x, g, up, down, tm, fc, attn=None, final_g=None):
    N, D = x.shape
    assert N % tm == 0 and D_FF % fc == 0, (x.shape, tm, fc)
    row = lambda i: (i, 0)
    args, specs = [x], [pl.BlockSpec((tm, D), row)]
    if attn is not None:
        o, wo = attn
        args += [o, wo]
        specs += [pl.BlockSpec((tm, Q_W), row), _const_spec(wo.shape)]
    args += [g, up, down]
    specs += [_const_spec((1, D)), _const_spec(up.shape), _const_spec(down.shape)]
    if final_g is not None:
        args.append(final_g)
        specs.append(_const_spec((1, D)))
    return pl.pallas_call(
        functools.partial(_mlp_kernel, fc=fc, with_attn=attn is not None, with_final=final_g is not None),
        grid=(N // tm,),
        in_specs=specs,
        out_specs=pl.BlockSpec((tm, D), row),
        out_shape=jax.ShapeDtypeStruct((N, D), jnp.float32),
        compiler_params=pltpu.CompilerParams(
            dimension_semantics=("parallel",), vmem_limit_bytes=VMEM_LIMIT),
        name="mlp_attn_out" if attn is not None else "mlp",
    )(*args)


def _rope_block(xb, tab_ref, special):
    o = LANES if special else 0
    c = tab_ref[:, o:o + LANES]
    s1 = tab_ref[:, 2 * LANES + o:3 * LANES + o]
    s2 = tab_ref[:, 4 * LANES + o:5 * LANES + o]
    return xb * c + pltpu.roll(xb, LANES - HEAD_DIM // 2, 1) * s1 + pltpu.roll(xb, HEAD_DIM // 2, 1) * s2


def _rope_rows(p, c, s):
    half = HEAD_DIM // 2
    r = p.reshape(p.shape[0] // HEAD_DIM, 2, half, p.shape[1])
    x1, x2 = r[:, 0], r[:, 1]
    out = jnp.stack([x1 * c - x2 * s, x2 * c + x1 * s], axis=1)
    return out.reshape(p.shape)


def _proj_kernel(x_ref, g_ref, w_ref, wt_ref, tab_ref, tabt_ref,
                 kc_ref, vc_ref, kp_ref, kw_ref, kib_ref, qt_ref, qit_ref, wit_ref, vt_ref):
    h = _rms(x_ref[0], g_ref[...]).astype(jnp.bfloat16)

    def proj(off, width):
        return jnp.dot(h, w_ref[:, off:off + width], preferred_element_type=jnp.float32)

    def rope(p, special=False):
        return jnp.concatenate(
            [_rope_block(p[:, j:j + LANES], tab_ref, special) for j in range(0, p.shape[1], LANES)], axis=-1)

    kc = rope(proj(OFF_KC, KV_W))
    kc_ref[0] = kc
    vc_ref[0] = proj(OFF_VC, KV_W)
    gap = jnp.zeros((kc.shape[0], LANES - HEAD_DIM), jnp.float32)
    kp = jnp.concatenate(
        [piece for g in range(N_KV_HEADS) for piece in (kc[:, g * HEAD_DIM:(g + 1) * HEAD_DIM], gap)], axis=1)
    kp_ref[0] = kp.astype(kp_ref.dtype)
    kw = rope(proj(OFF_KW, LANES), special=True)
    kw_ref[0] = kw
    kib_ref[0] = kw.astype(kib_ref.dtype)

    def proj_t(row, n):
        return lax.dot_general(wt_ref[row:row + n, :], h, _NT, preferred_element_type=jnp.float32)

    half = HEAD_DIM // 2
    c, s = tabt_ref[0:half, :], tabt_ref[half:HEAD_DIM, :]
    qt_ref[0] = _rope_rows(proj_t(ROW_Q, Q_W), c * LOG2E, s * LOG2E).astype(qt_ref.dtype)
    qit_ref[0] = _rope_rows(proj_t(ROW_QI, QI_W), c, s).astype(qit_ref.dtype)
    vw = proj_t(ROW_V, KV_W + PACKED_ROWS)
    wit_ref[0] = vw[KV_W:KV_W + N_IDX_HEADS, :] * (N_IDX_HEADS ** -0.5)
    n = vw.shape[1]
    tail = jnp.where(lax.broadcasted_iota(jnp.int32, (V_SLOT - HEAD_DIM, n), 0) == 0, 1.0, 0.0)
    vt = jnp.concatenate(
        [piece for g in range(N_KV_HEADS) for piece in (vw[g * HEAD_DIM:(g + 1) * HEAD_DIM], tail)], axis=0)
    kbw = vt_ref.shape[3]
    for c in range(vt_ref.shape[1]):
        vt_ref[0, c] = vt[:, c * kbw:(c + 1) * kbw].astype(vt_ref.dtype)


def _proj_layer(x, g, w_rowmajor, w_transposed, tab, tabt, tm):
    B, T, D = x.shape
    blk = lambda w: pl.BlockSpec((1, tm, w), lambda b, t: (b, t, 0))
    blk_t = lambda r: pl.BlockSpec((1, r, tm), lambda b, t: (b, 0, t))
    bf, f32 = jnp.bfloat16, jnp.float32
    sds = jax.ShapeDtypeStruct
    kbw = min(tm, KEY_BLOCK)
    assert T % tm == 0 and tm % kbw == 0, (x.shape, tm)
    return pl.pallas_call(
        _proj_kernel,
        grid=(B, T // tm),
        in_specs=[
            blk(D),
            _const_spec((1, D)),
            _const_spec(w_rowmajor.shape),
            _const_spec(w_transposed.shape),
            pl.BlockSpec((tm, tab.shape[1]), lambda b, t: (t, 0)),
            pl.BlockSpec((HEAD_DIM, tm), lambda b, t: (0, t)),
        ],
        out_specs=[blk(KV_W), blk(KV_W), blk(KV_PAD_W), blk(LANES), blk(LANES),
                   blk_t(Q_W), blk_t(QI_W), blk_t(N_IDX_HEADS),
                   pl.BlockSpec((1, tm // kbw, V_ROWS, kbw), lambda b, t: (b, t, 0, 0))],
        out_shape=[sds((B, T, KV_W), f32), sds((B, T, KV_W), f32), sds((B, T, KV_PAD_W), bf),
                   sds((B, T, LANES), f32), sds((B, T, LANES), bf),
                   sds((B, Q_W, T), bf), sds((B, QI_W, T), bf), sds((B, N_IDX_HEADS, T), f32),
                   sds((B, T // kbw, V_ROWS, kbw), bf)],
        compiler_params=pltpu.CompilerParams(
            dimension_semantics=("parallel", "parallel"), vmem_limit_bytes=VMEM_LIMIT),
        name="attn_in_proj",
    )(x, g, w_rowmajor, w_transposed, tab, tabt)


def _attn_kernel(qt_ref, qit_ref, wit_ref, kp_ref, vt_ref, kib_ref, o_ref,
                 key_ref, hi_ref, lo_ref, lo2_ref, f_ref, s0_ref, s1_ref, oacc_ref, *,
                 tq, tq_real, n_keys_real, pos0, n_sel):
    kb = KEY_BLOCK
    t = pl.program_id(1)
    lane_q = lax.broadcasted_iota(jnp.int32, (1, tq), 1)
    q_limit = ((pos0 + t * tq + lane_q) // CHUNK + 1) * CHUNK
    qvalid = lane_q < tq_real
    n_reach = jnp.minimum(((pos0 + (t + 1) * tq - 1) // CHUNK + 1) * CHUNK, n_keys_real)
    nkb = (n_reach + kb - 1) // kb
    n_unvisited = jnp.maximum(n_keys_real - nkb * kb, 0)
    key_row = lax.broadcasted_iota(jnp.int32, (kb, tq), 0)

    def block_start(j):
        return pl.multiple_of(j * kb, kb)

    def for_blocks(body, carry):
        start = 0
        for n in BLOCK_TIERS:
            def trip(i, c, n=n, start=start):
                for u in range(n):
                    c = body(start + n * i + u, c)
                return c
            trips = (nkb - start) // n
            carry = lax.fori_loop(0, trips, trip, carry)
            start = start + trips * n
        return carry

    def admissible(k0):
        return key_row < q_limit - k0

    wi = wit_ref[0]

    def score_block(j, carry):
        k0 = block_start(j)
        kx = kib_ref[0, pl.ds(k0, kb), 0:IDX_DIM]
        acc = jnp.zeros((kb, tq), jnp.float32)
        for hp in range(N_IDX_HEADS // 2):
            qi2 = jnp.concatenate(
                [qit_ref[0, (2 * hp + u) * IDX_DIM:(2 * hp + u + 1) * IDX_DIM, :] for u in range(2)], axis=1)
            lg = jnp.dot(kx, qi2, preferred_element_type=jnp.float32)
            for u in range(2):
                hh = 2 * hp + u
                acc = acc + jnp.maximum(lg[:, u * tq:(u + 1) * tq], 0.0) * wi[hh:hh + 1, :]
        sc = jnp.where(admissible(k0), acc, NEG)
        if n_keys_real % kb:
            sc = jnp.where(key_row < n_keys_real - k0, sc, -jnp.inf)
        bits = lax.bitcast_convert_type(sc, jnp.int32)
        key = bits ^ ((bits >> 31) & jnp.int32(0x7FFFFFFF))
        key_ref[pl.ds(k0, kb), :] = key
        hi_ref[pl.ds(k0, kb), :] = (key >> 16).astype(jnp.int16)
        lo_ref[pl.ds(k0, kb), :] = (key ^ 0x8000).astype(jnp.int16)
        return carry

    for_blocks(score_block, 0)

    def count16(ref, pred):
        def body(j, acc):
            hit = jnp.where(pred(ref[pl.ds(block_start(j), kb), :]), jnp.int16(1), jnp.int16(0))
            return acc + _fold_rows(jnp.add, hit, rows=PACKED_ROWS)
        acc = for_blocks(body, jnp.zeros((PACKED_ROWS, tq), jnp.int16))
        return jnp.sum(acc.astype(jnp.int32), axis=0, keepdims=True)

    def as_plane(v):
        tile = jnp.broadcast_to(v, (PACKED_ROWS, tq)).astype(jnp.int16)
        return jnp.concatenate([tile] * (kb // PACKED_ROWS), axis=0)

    def radix16(ref, n_wanted, n_all, unvisited_ge):
        def step(i, carry):
            thr, n_ge_thr, n_gt_thr = carry
            cand = thr + jnp.left_shift(jnp.int32(1), 15 - i)
            plane = as_plane(cand)
            n_ge = count16(ref, lambda blk: blk >= plane) + unvisited_ge(cand)
            take = n_ge >= n_wanted
            return jnp.where(take, cand, thr), jnp.where(take, n_ge, n_ge_thr), jnp.where(take, n_gt_thr, n_ge)
        start = (jnp.full((1, tq), -2 ** 15, jnp.int32), n_all, jnp.zeros((1, tq), jnp.int32))
        return lax.fori_loop(0, 16, step, start)

    n_all = jnp.broadcast_to(nkb * kb + n_unvisited, (1, tq))
    thr_hi, n_ge_hi, n_gt_hi = radix16(
        hi_ref, n_sel, n_all, lambda cand: jnp.where(cand <= NEG_KEY_HI, n_unvisited, 0))
    hi_plane = as_plane(thr_hi)

    def park_block(j, carry):
        rows = pl.ds(block_start(j), kb)
        lo2_ref[rows, :] = jnp.where(hi_ref[rows, :] == hi_plane, lo_ref[rows, :], jnp.int16(-2 ** 15))
        return carry

    for_blocks(park_block, 0)
    neg_in_bucket = thr_hi == NEG_KEY_HI
    thr_lo, n_ge_lo, n_gt_lo = radix16(
        lo2_ref, n_sel - n_gt_hi, n_ge_hi - n_gt_hi,
        lambda cand: jnp.where(neg_in_bucket & (cand <= NEG_KEY_LO), n_unvisited, 0))
    thr = thr_hi * 65536 + (thr_lo + 2 ** 15)
    n_gt = n_gt_hi + n_gt_lo
    n_eq = n_ge_lo - n_gt_lo
    need = n_sel - n_gt
    tie = (n_eq > need) & qvalid
    any_tie = jnp.max(tie.astype(jnp.int32)) > 0

    def mask_block(j, run, exact_ties):
        k0 = block_start(j)
        kblk = key_ref[pl.ds(k0, kb), :]
        if exact_ties:
            r = lax.broadcasted_iota(jnp.int32, (kb, kb), 0)
            c = lax.broadcasted_iota(jnp.int32, (kb, kb), 1)
            tri = jnp.where(r >= c, 1.0, 0.0).astype(jnp.bfloat16)
            eqb = kblk == thr
            eqf = jnp.where(eqb, 1.0, 0.0)
            rank = jnp.dot(tri, eqf.astype(jnp.bfloat16), preferred_element_type=jnp.float32) + run
            run = run + jnp.sum(eqf, axis=0, keepdims=True)
            sel = (kblk > thr) | (eqb & (rank <= need.astype(jnp.float32)))
        else:
            sel = kblk >= thr
        f = jnp.where(sel, jnp.where(admissible(k0), jnp.inf, NEG), -jnp.inf)
        f_ref[pl.ds(k0, kb), :] = f
        return f, run

    gw = KV_GROUP * tq
    s_refs = (s0_ref, s1_ref)

    def logits_block(g, j, f=None):
        k0 = block_start(j)
        q4 = jnp.concatenate(
            [qt_ref[0, (g * KV_GROUP + r) * HEAD_DIM:(g * KV_GROUP + r + 1) * HEAD_DIM, :] for r in range(KV_GROUP)],
            axis=1)
        kblk = kp_ref[0, pl.ds(k0, kb), g * LANES:g * LANES + HEAD_DIM]
        s = jnp.dot(kblk, q4, preferred_element_type=jnp.float32)
        if f is None:
            f = f_ref[pl.ds(k0, kb), :]
        s = jnp.minimum(s, jnp.concatenate([f] * KV_GROUP, axis=1))
        s_refs[g % 2][pl.ds(k0, kb), :] = s
        return _fold_rows(jnp.maximum, s)

    def values_block(g, j, m):
        p = jnp.exp2(s_refs[g % 2][pl.ds(block_start(j), kb), :] - m).astype(jnp.bfloat16)
        oacc_ref[...] += jnp.dot(vt_ref[0, j, g * V_SLOT:(g + 1) * V_SLOT, :], p,
                                 preferred_element_type=jnp.float32)

    m_init = jnp.full((SUBLANES, gw), -jnp.inf, jnp.float32)

    def first_pass(exact_ties):
        def body(j, carry):
            a, run = carry
            f, run = mask_block(j, run, exact_ties)
            return jnp.maximum(a, logits_block(0, j, f)), run
        return for_blocks(body, (m_init, jnp.zeros((1, tq), jnp.float32)))[0]

    m_acc = lax.cond(any_tie, lambda: first_pass(True), lambda: first_pass(False))
    for g in range(N_KV_HEADS):
        m = jnp.max(m_acc, axis=0, keepdims=True)
        oacc_ref[...] = jnp.zeros(oacc_ref.shape, jnp.float32)

        def fused(j, a, g=g, m=m):
            if g + 1 < N_KV_HEADS:
                a = jnp.maximum(a, logits_block(g + 1, j))
            values_block(g, j, m)
            return a

        m_acc = for_blocks(fused, m_init)
        oacc = oacc_ref[...]
        on = oacc[0:HEAD_DIM] * (1.0 / oacc[HEAD_DIM:HEAD_DIM + 1])
        for u in range(KV_GROUP // 2):
            pair = jnp.concatenate([on[:, (2 * u) * tq:(2 * u + 1) * tq], on[:, (2 * u + 1) * tq:(2 * u + 2) * tq]],
                                   axis=0)
            c0 = (g * KV_GROUP + 2 * u) * HEAD_DIM
            o_ref[0, :, c0:c0 + 2 * HEAD_DIM] = pair.T.astype(o_ref.dtype)


def _attn_layer(qt, qit, wit, kp, vt, kib, *, tq, tq_real, n_keys_real, pos0, n_sel):
    B, _, T = qt.shape
    n_keys = kp.shape[1]
    assert T % tq == 0 and tq % LANES == 0 and n_keys % KEY_BLOCK == 0 and n_keys_real <= n_keys, (qt.shape, kp.shape)
    assert n_sel <= n_keys_real, (n_sel, n_keys_real)
    qblk = lambda r: pl.BlockSpec((1, r, tq), lambda b, t: (b, 0, t))
    kblk = lambda w: pl.BlockSpec((1, n_keys, w), lambda b, t: (b, 0, 0))
    return pl.pallas_call(
        functools.partial(_attn_kernel, tq=tq, tq_real=tq_real, n_keys_real=n_keys_real, pos0=pos0, n_sel=n_sel),
        grid=(B, T // tq),
        in_specs=[qblk(Q_W), qblk(QI_W), qblk(N_IDX_HEADS), kblk(KV_PAD_W),
                  pl.BlockSpec((1,) + vt.shape[1:], lambda b, t: (b, 0, 0, 0)), kblk(LANES)],
        out_specs=pl.BlockSpec((1, tq, Q_W), lambda b, t: (b, t, 0)),
        out_shape=jax.ShapeDtypeStruct((B, T, Q_W), jnp.bfloat16),
        scratch_shapes=[pltpu.VMEM((n_keys, tq), jnp.int32)] + [pltpu.VMEM((n_keys, tq), jnp.int16)] * 3 + [
                        pltpu.VMEM((n_keys, tq), jnp.float32),
                        pltpu.VMEM((n_keys, KV_GROUP * tq), jnp.float32),
                        pltpu.VMEM((n_keys, KV_GROUP * tq), jnp.float32),
                        pltpu.VMEM((V_SLOT, KV_GROUP * tq), jnp.float32)],
        compiler_params=pltpu.CompilerParams(
            dimension_semantics=("parallel", "parallel"), vmem_limit_bytes=VMEM_LIMIT),
        name="dsa_attention",
    )(qt, qit, wit, kp, vt, kib)


def _pad_heads(a):
    lead = a.shape[:-1]
    a = a.reshape(*lead, N_KV_HEADS, HEAD_DIM)
    a = jnp.pad(a, [(0, 0)] * len(lead) + [(0, 0), (0, LANES - HEAD_DIM)])
    return a.reshape(*lead, KV_PAD_W)


def _pack_w_in(w_in):
    o = 0
    wq = w_in[:, o:o + Q_W] * (HEAD_DIM ** -0.5); o += Q_W
    wk = w_in[:, o:o + KV_W]; o += KV_W
    wv = w_in[:, o:o + KV_W]; o += KV_W
    wqi = w_in[:, o:o + QI_W] * (IDX_DIM ** -0.5); o += QI_W
    wki = w_in[:, o:o + IDX_DIM]; o += IDX_DIM
    wwi = w_in[:, o:o + N_IDX_HEADS]
    wkw = jnp.pad(jnp.concatenate([wki, wwi], axis=1), ((0, 0), (0, LANES - IDX_DIM - N_IDX_HEADS)))
    rowmajor = jnp.concatenate([wk, wv, wkw], axis=1)
    wwi_rows = jnp.pad(wwi, ((0, 0), (0, PACKED_ROWS - N_IDX_HEADS)))
    transposed = jnp.concatenate([wq, wqi, wv, wwi_rows], axis=1).T
    assert rowmajor.shape[1] == PROJ_ROWMAJOR_W and transposed.shape[0] == PROJ_TRANSPOSED_ROWS
    return rowmajor.astype(jnp.bfloat16), transposed.astype(jnp.bfloat16)


def _rope_tables(pos):
    inv_freq = 1.0 / (ROPE_THETA ** (jnp.arange(0, HEAD_DIM, 2, dtype=jnp.float32) / HEAD_DIM))
    ang = pos.astype(jnp.float32)[:, None] * inv_freq[None, :]
    c, s = jnp.cos(ang), jnp.sin(ang)
    z = jnp.zeros_like(c)
    n = pos.shape[0]
    zpad = jnp.zeros((n, LANES - HEAD_DIM), jnp.float32)
    wi_scale = jnp.full((n, N_IDX_HEADS), N_IDX_HEADS ** -0.5, jnp.float32)
    c_kw = jnp.concatenate([c, c, wi_scale, jnp.zeros((n, LANES - HEAD_DIM - N_IDX_HEADS), jnp.float32)], axis=1)
    tab = jnp.concatenate([
        jnp.concatenate([c, c, c, c], axis=1), c_kw,
        jnp.concatenate([-s, z, -s, z], axis=1), jnp.concatenate([-s, z, zpad], axis=1),
        jnp.concatenate([z, s, z, s], axis=1), jnp.concatenate([z, s, zpad], axis=1),
    ], axis=1)
    tabt = jnp.concatenate([c, s], axis=1).T
    return tab, tabt


def _round_up(n, m):
    return (n + m - 1) // m * m


def _value_blocks_t(v):
    B, n, _ = v.shape
    nb = n // KEY_BLOCK
    vt = jnp.transpose(v.reshape(B, nb, KEY_BLOCK, N_KV_HEADS, HEAD_DIM), (0, 1, 3, 4, 2))
    ones = jnp.ones((B, nb, N_KV_HEADS, 1, KEY_BLOCK), v.dtype)
    zeros = jnp.zeros((B, nb, N_KV_HEADS, V_SLOT - HEAD_DIM - 1, KEY_BLOCK), v.dtype)
    return jnp.concatenate([vt, ones, zeros], axis=3).reshape(B, nb, V_ROWS, KEY_BLOCK)


def _trunk(x, pos0, pool_past, attn_past, norm_mix, norm_mlp, norm_final, pool_w, pool_scale,
           w_rowmajor, w_transposed, w_o, w_up, w_down):
    B, T, D = x.shape
    bf = jnp.bfloat16
    row = lambda v: v.reshape(1, D)

    if pool_past is None:
        past16 = jnp.zeros((B, POOL_STATE + 1, D), jnp.float32)
    else:
        past16 = jnp.pad(pool_past, ((0, 0), (1, 0), (0, 0)))
    tt = min(T, ROW_TILE)
    tm = min(B * T, ROW_TILE)
    x1, st = _pool_layer(x, past16, row(norm_mix[0]), pool_w[0].astype(bf), row(pool_scale[0]), pos0, tt)
    x2 = _mlp_layer(x1.reshape(B * T, D), row(norm_mlp[0]), w_up[0], w_down[0], tm, FF_CHUNK).reshape(B, T, D)
    pool_new = st[:, 1:][None]

    tab, tabt = _rope_tables(pos0 + jnp.arange(T, dtype=jnp.int32))
    kc, vc, kp, kw, kib, qt, qit, wit, vt = _proj_layer(
        x2, row(norm_mix[1]), w_rowmajor, w_transposed, tab, tabt, min(T, PROJ_TILE))
    k_new = kc.reshape(1, B, T, N_KV_HEADS, HEAD_DIM)
    v_new = vc.reshape(1, B, T, N_KV_HEADS, HEAD_DIM)
    ki_new = kw[:, :, :IDX_DIM][None]

    if attn_past is None:
        n_real = T
        kp_all, vt_all, kib_all = kp, vt, kib
    else:
        ck, cv, cki = attn_past
        P = ck.shape[1]
        n_real = P + T
        n_keys = _round_up(n_real, KEY_BLOCK)
        fill = lambda w: jnp.zeros((B, n_keys - n_real, w), bf)
        kp_all = jnp.concatenate([_pad_heads(ck.reshape(B, P, KV_W).astype(bf)), kp, fill(KV_PAD_W)], axis=1)
        kib_all = jnp.concatenate(
            [jnp.pad(cki.astype(bf), ((0, 0), (0, 0), (0, LANES - IDX_DIM))), kib, fill(LANES)], axis=1)
        vt_all = _value_blocks_t(
            jnp.concatenate([cv.reshape(B, P, KV_W).astype(bf), vc.astype(bf), fill(KV_W)], axis=1))
    tq = min(Q_TILE, _round_up(T, LANES))
    t_pad = _round_up(T, tq)
    padq = lambda a: jnp.pad(a, ((0, 0), (0, 0), (0, t_pad - T)))
    n_sel = min(TOPK_MAX, n_real // 4)
    o = _attn_layer(padq(qt), padq(qit), padq(wit), kp_all, vt_all, kib_all,
                    tq=tq, tq_real=min(T, tq), n_keys_real=n_real, pos0=pos0, n_sel=n_sel)
    o = o[:, :T].reshape(B * T, Q_W)

    y = _mlp_layer(x2.reshape(B * T, D), row(norm_mlp[1]), w_up[1], w_down[1], tm, FF_CHUNK,
                   attn=(o, w_o), final_g=row(norm_final)).reshape(B, T, D)
    return y, pool_new, k_new, v_new, ki_new


def kernel(x_prompt, x_sample, state_pool, cache_k, cache_v, cache_kidx, norm_mix, norm_mlp, norm_final,
           pool_w, pool_scale, attn_w_in, attn_w_o, mlp_w_up, mlp_w_down):
    bf = jnp.bfloat16
    shared = (norm_mix, norm_mlp, norm_final, pool_w, pool_scale, *_pack_w_in(attn_w_in[0]),
              attn_w_o[0].astype(bf), mlp_w_up.astype(bf), mlp_w_down.astype(bf))
    y_p, pool_p, k_p, v_p, ki_p = _trunk(x_prompt, 0, None, None, *shared)
    y_s, pool_s, k_s, v_s, ki_s = _trunk(
        x_sample, cache_k.shape[2], state_pool[0], (cache_k[0], cache_v[0], cache_kidx[0]), *shared)
    return (y_p, y_s, pool_p, pool_s, k_p, v_p, ki_p, k_s, v_s, ki_s)
```
